```python
import jax, jax.numpy as jnp
from jax import lax
import numpy as np

D_MODEL = 2048
BATCH = 4
SEQ = 2048
DEPTH = 2
DEC_BATCH = 128
DEC_SEQ = 1
PAST_LEN = 16384
PAGE_SIZE = 128

N_MIXERS = 4
GROUP_WIDTH = D_MODEL // N_MIXERS
HEADS = 4
HEAD_DIM = GROUP_WIDTH // HEADS
GLA_KEY_DIM = HEAD_DIM // 2
GLA_RANK = 16
GLA_GATE_DENOM = 16.0
CONV_W = 4
CHUNK = 64
ROPE_BASE = 10000.0
X_HEADS = 4
X_HEAD_DIM = D_MODEL // X_HEADS
MEM_LEN = 256
D_FF = 4 * D_MODEL
EPS = 1e-6

IN_SPLITS = (
    ("ret_q", GROUP_WIDTH), ("ret_k", GROUP_WIDTH), ("ret_v", GROUP_WIDTH), ("ret_g", GROUP_WIDTH),
    ("ml_q", GROUP_WIDTH), ("ml_k", GROUP_WIDTH), ("ml_v", GROUP_WIDTH), ("ml_o", GROUP_WIDTH),
    ("ml_i", HEADS), ("ml_f", HEADS),
    ("gd_q", GROUP_WIDTH), ("gd_k", GROUP_WIDTH), ("gd_v", GROUP_WIDTH), ("gd_z", GROUP_WIDTH),
    ("gd_b", HEADS), ("gd_a", HEADS),
    ("gl_q", HEADS * GLA_KEY_DIM), ("gl_k", HEADS * GLA_KEY_DIM), ("gl_v", GROUP_WIDTH),
    ("gl_z", GROUP_WIDTH), ("gl_r", GLA_RANK),
)
IN_WIDTH = sum(w for _, w in IN_SPLITS)
STATE_NAMES = ("ret", "ml_C", "ml_n", "ml_m", "gdn", "gdn_conv", "gla")

kernel_name = "hybrid_parallel_heads_ret_mlstm_gdn_gla_step"


def _split_cols(t):
    out = {}
    off = 0
    for name, w in IN_SPLITS:
        out[name] = t[..., off:off + w]
        off += w
    return out


def _rms(x, g=None):
    xf = x.astype(jnp.float32)
    y = xf * lax.rsqrt(jnp.mean(xf * xf, axis=-1, keepdims=True) + EPS)
    if g is not None:
        y = y * g.astype(jnp.float32)
    return y


def _l2n(t):
    return t * lax.rsqrt(jnp.sum(t * t, axis=-1, keepdims=True) + EPS)


def _rope(t, pos):
    half = t.shape[-1] // 2
    inv = ROPE_BASE ** (-jnp.arange(half, dtype=jnp.float32) / half)
    ang = pos[:, None] * inv[None, :]
    cos = jnp.cos(ang)[None, :, None, :]
    sin = jnp.sin(ang)[None, :, None, :]
    t1, t2 = t[..., :half], t[..., half:]
    return jnp.concatenate([t1 * cos - t2 * sin, t1 * sin + t2 * cos], axis=-1)


def _chunk_len(T):
    return CHUNK if T % CHUNK == 0 else T


def _chunks(t, L):
    B, T, H = t.shape[:3]
    rest = t.shape[3:]
    t = t.reshape((B, T // L, L, H) + rest)
    return t.transpose((1, 0, 3, 2) + tuple(range(4, 4 + len(rest))))


def _unchunk(t):
    NC, B, H, L, d = t.shape
    return t.transpose(1, 0, 3, 2, 4).reshape(B, NC * L, H, d)


def _masks(L):
    tril = jnp.tril(jnp.ones((L, L), dtype=bool))
    strict = jnp.tril(jnp.ones((L, L), dtype=bool), k=-1)
    return tril, strict


def _retention(q, k, v, S0, pos):
    L = _chunk_len(q.shape[1])
    q = _rope(q, pos)
    k = _rope(k, pos) * HEAD_DIM ** -0.5
    log_gamma = jnp.log(1.0 - 2.0 ** (-5.0 - jnp.arange(HEADS, dtype=jnp.float32)))
    idx = jnp.arange(L, dtype=jnp.float32)
    diff = idx[:, None] - idx[None, :]
    decay = jnp.where(diff >= 0, jnp.exp(jnp.maximum(diff, 0.0) * log_gamma[:, None, None]), 0.0)
    q_dec = jnp.exp((idx + 1.0) * log_gamma[:, None])[..., None]
    k_dec = jnp.exp((L - 1.0 - idx) * log_gamma[:, None])[..., None]
    c_dec = jnp.exp(L * log_gamma)[:, None, None]

    def step(S, c):
        qc, kc, vc = c
        s = jnp.einsum('bhld,bhmd->bhlm', qc, kc) * decay
        o = jnp.einsum('bhlm,bhme->bhle', s, vc) + jnp.einsum('bhld,bhde->bhle', qc * q_dec, S)
        S = S * c_dec + jnp.einsum('bhld,bhle->bhde', kc * k_dec, vc)
        return S, o

    S, o = lax.scan(step, S0, (_chunks(q, L), _chunks(k, L), _chunks(v, L)))
    return _unchunk(o), S


def _mlstm(q, k, v, i_pre, f_pre, C0, n0, m0):
    L = _chunk_len(q.shape[1])
    tril, _ = _masks(L)
    k = k * HEAD_DIM ** -0.5
    logf = jax.nn.log_sigmoid(f_pre)

    def step(carry, c):
        C, n, m = carry
        qc, kc, vc, ic, fc = c
        F = jnp.cumsum(fc, axis=-1)
        Dm = jnp.where(tril, F[..., :, None] - F[..., None, :] + ic[..., None, :], -jnp.inf)
        m_inter = m[..., None] + F
        m_t = jnp.maximum(m_inter, jnp.max(Dm, axis=-1))
        W = jnp.exp(Dm - m_t[..., None])
        g_inter = jnp.exp(m_inter - m_t)
        s = jnp.einsum('bhld,bhmd->bhlm', qc, kc) * W
        num = jnp.einsum('bhlm,bhme->bhle', s, vc) + g_inter[..., None] * jnp.einsum('bhld,bhde->bhle', qc, C)
        den = jnp.sum(s, axis=-1) + g_inter * jnp.einsum('bhld,bhd->bhl', qc, n)
        h = num / jnp.maximum(jnp.abs(den), jnp.exp(-m_t))[..., None]
        m_new = m_t[..., -1]
        wk = jnp.exp(F[..., -1:] - F + ic - m_new[..., None])[..., None]
        dec = jnp.exp(m + F[..., -1] - m_new)
        C = dec[..., None, None] * C + jnp.einsum('bhld,bhle->bhde', kc * wk, vc)
        n = dec[..., None] * n + jnp.sum(kc * wk, axis=2)
        return (C, n, m_new), h

    (C, n, m), h = lax.scan(step, (C0, n0, m0),
                            (_chunks(q, L), _chunks(k, L), _chunks(v, L), _chunks(i_pre, L), _chunks(logf, L)))
    return _unchunk(h), C, n, m


def _gated_delta(q, k, v, g, beta, S0):
    L = _chunk_len(q.shape[1])
    d = q.shape[-1]
    tril, strict = _masks(L)
    q = q * d ** -0.5
    eye = jnp.eye(L, dtype=jnp.float32)

    def step(S, c):
        qc, kc, vc, gc, bc = c
        G = jnp.cumsum(gc, axis=-1)
        diff = G[..., :, None] - G[..., None, :]
        decay = jnp.where(tril, jnp.exp(jnp.where(tril, diff, 0.0)), 0.0)
        kb = kc * bc[..., None]
        A = jnp.where(strict, jnp.einsum('bhld,bhmd->bhlm', kb, kc) * decay, 0.0)
        rhs = jnp.concatenate([vc * bc[..., None], kb * jnp.exp(G)[..., None]], axis=-1)
        sol = lax.linalg.triangular_solve(eye + A, rhs, left_side=True, lower=True)
        u, w = sol[..., :d], sol[..., d:]
        v_new = u - jnp.einsum('bhld,bhde->bhle', w, S)
        attn = jnp.einsum('bhld,bhmd->bhlm', qc, kc) * decay
        o = jnp.einsum('bhld,bhde->bhle', qc * jnp.exp(G)[..., None], S) + jnp.einsum('bhlm,bhme->bhle', attn, v_new)
        S = S * jnp.exp(G[..., -1])[..., None, None] + jnp.einsum(
            'bhld,bhle->bhde', kc * jnp.exp(G[..., -1:] - G)[..., None], v_new)
        return S, o

    S, o = lax.scan(step, S0, (_chunks(q, L), _chunks(k, L), _chunks(v, L), _chunks(g, L), _chunks(beta, L)))
    return _unchunk(o), S


def _gla(q, k, v, gk, S0):
    L = _chunk_len(q.shape[1])
    tril, _ = _masks(L)
    q = q * q.shape[-1] ** -0.5

    def step(S, c):
        qc, kc, vc, gc = c
        G = jnp.cumsum(gc, axis=2)
        diff = G[:, :, :, None, :] - G[:, :, None, :, :]
        w = jnp.exp(jnp.where(tril[:, :, None], diff, -jnp.inf))
        A = jnp.einsum('bhld,bhmd,bhlmd->bhlm', qc, kc, w)
        o = jnp.einsum('bhlm,bhme->bhle', A, vc) + jnp.einsum('bhld,bhde->bhle', qc * jnp.exp(G), S)
        S = S * jnp.exp(G[:, :, -1])[..., None] + jnp.einsum(
            'bhld,bhle->bhde', kc * jnp.exp(G[:, :, -1:] - G), vc)
        return S, o

    S, o = lax.scan(step, S0, (_chunks(q, L), _chunks(k, L), _chunks(v, L), _chunks(gk, L)))
    return _unchunk(o), S


def _token_mixers(h, pos, st, lw):
    B, T, _ = h.shape
    f32 = jnp.float32
    c = _split_cols(h @ lw["w_in"])

    def heads(t, d=HEAD_DIM):
        return t.astype(f32).reshape(B, T, HEADS, d)

    o_ret, s_ret = _retention(heads(c["ret_q"]), heads(c["ret_k"]), heads(c["ret_v"]),
                              st["ret"].astype(f32), pos)
    y_ret = _rms(o_ret).reshape(B, T, GROUP_WIDTH) * jax.nn.silu(c["ret_g"].astype(f32))

    i_pre = c["ml_i"].astype(f32) + lw["ml_ib"].astype(f32)
    f_pre = c["ml_f"].astype(f32) + lw["ml_fb"].astype(f32)
    h_ml, s_C, s_n, s_m = _mlstm(heads(c["ml_q"]), heads(c["ml_k"]), heads(c["ml_v"]), i_pre, f_pre,
                                 st["ml_C"].astype(f32), st["ml_n"].astype(f32), st["ml_m"].astype(f32))
    y_ml = (_rms(h_ml).reshape(B, T, GROUP_WIDTH) * lw["ml_norm"].astype(f32)) * jax.nn.sigmoid(c["ml_o"].astype(f32))

    qkv = jnp.concatenate([c["gd_q"], c["gd_k"], c["gd_v"]], axis=-1).astype(f32)
    full = jnp.concatenate([st["gdn_conv"].astype(f32), qkv], axis=1)
    wc = lw["gd_conv"].astype(f32)
    conv = full[:, 0:T] * wc[0]
    for j in range(1, CONV_W):
        conv = conv + full[:, j:j + T] * wc[j]
    conv = jax.nn.silu(conv)
    new_conv = full[:, T:]
    gq = _l2n(heads(conv[..., :GROUP_WIDTH]))
    gk = _l2n(heads(conv[..., GROUP_WIDTH:2 * GROUP_WIDTH]))
    gv = heads(conv[..., 2 * GROUP_WIDTH:])
    beta = jax.nn.sigmoid(c["gd_b"].astype(f32))
    g_log = -jnp.exp(lw["gd_A_log"].astype(f32)) * jax.nn.softplus(c["gd_a"].astype(f32) + lw["gd_dt_bias"].astype(f32))
    o_gd, s_gdn = _gated_delta(gq, gk, gv, g_log, beta, st["gdn"].astype(f32))
    y_gd = (_rms(o_gd) * lw["gd_norm"].astype(f32) * jax.nn.silu(heads(c["gd_z"]))).reshape(B, T, GROUP_WIDTH)

    gk_log = jax.nn.log_sigmoid(c["gl_r"].astype(f32) @ lw["gl_w2"].astype(f32) + lw["gl_b"].astype(f32)) / GLA_GATE_DENOM
    o_gl, s_gla = _gla(heads(c["gl_q"], GLA_KEY_DIM), heads(c["gl_k"], GLA_KEY_DIM), heads(c["gl_v"]),
                       heads(gk_log, GLA_KEY_DIM), st["gla"].astype(f32))
    y_gl = (_rms(o_gl) * lw["gl_norm"].astype(f32)).reshape(B, T, GROUP_WIDTH) * jax.nn.silu(c["gl_z"].astype(f32))

    y = jnp.concatenate([y_ret, y_ml, y_gd, y_gl], axis=-1).astype(h.dtype) @ lw["w_out"]
    new_st = {"ret": s_ret, "ml_C": s_C, "ml_n": s_n, "ml_m": s_m,
              "gdn": s_gdn, "gdn_conv": new_conv, "gla": s_gla}
    return y, new_st


def _cross_attn(h, mk, mv, w_q, w_o):
    B, T, _ = h.shape
    q = (h @ w_q).astype(jnp.float32).reshape(B, T, X_HEADS, X_HEAD_DIM)
    k = mk.astype(jnp.float32).reshape(B, MEM_LEN, X_HEADS, X_HEAD_DIM)
    v = mv.astype(jnp.float32).reshape(B, MEM_LEN, X_HEADS, X_HEAD_DIM)
    s = jnp.einsum('bthd,bmhd->bhtm', q, k) * X_HEAD_DIM ** -0.5
    p = jax.nn.softmax(s, axis=-1)
    o = jnp.einsum('bhtm,bmhd->bthd', p, v).reshape(B, T, D_MODEL)
    return o.astype(h.dtype) @ w_o


def _layer(x, pos, st, mk, mv, lw):
    dt = x.dtype
    h = _rms(x, lw["norm_mix_pre"]).astype(dt)
    y, new_st = _token_mixers(h, pos, st, lw)
    x = x + _rms(y, lw["norm_mix_post"]).astype(dt)
    h = _rms(x, lw["norm_x_pre"]).astype(dt)
    x = x + _rms(_cross_attn(h, mk, mv, lw["w_xq"], lw["w_xo"]), lw["norm_x_post"]).astype(dt)
    h = _rms(x, lw["norm_mlp_pre"]).astype(dt)
    u = jnp.square(jax.nn.relu(h @ lw["w_up"]))
    x = x + _rms(u @ lw["w_down"], lw["norm_mlp_post"]).astype(dt)
    return x, new_st


def _zero_state(B):
    f32 = jnp.float32
    return {"ret": jnp.zeros((B, HEADS, HEAD_DIM, HEAD_DIM), f32),
            "ml_C": jnp.zeros((B, HEADS, HEAD_DIM, HEAD_DIM), f32),
            "ml_n": jnp.zeros((B, HEADS, HEAD_DIM), f32),
            "ml_m": jnp.zeros((B, HEADS), f32),
            "gdn": jnp.zeros((B, HEADS, HEAD_DIM, HEAD_DIM), f32),
            "gdn_conv": jnp.zeros((B, CONV_W - 1, 3 * GROUP_WIDTH), f32),
            "gla": jnp.zeros((B, HEADS, GLA_KEY_DIM, HEAD_DIM), f32)}


def setup_inputs(seed: int = 0) -> dict:
    key = jax.random.key(seed)
    ks = iter(jax.random.split(key, 64))
    f32 = jnp.float32

    def nrm(shape, scale):
        return jax.random.normal(next(ks), shape, f32) * scale

    def gain(shape):
        return 1.0 + 0.1 * jax.random.normal(next(ks), shape, f32)

    H, d, GW = HEADS, HEAD_DIM, GROUP_WIDTH
    return {
        "x_prompt": nrm((BATCH, SEQ, D_MODEL), 1.0),
        "x_sample": nrm((DEC_BATCH, DEC_SEQ, D_MODEL), 1.0),
        "state_ret": nrm((DEPTH, DEC_BATCH, H, d, d), 0.3),
        "state_mlstm_C": nrm((DEPTH, DEC_BATCH, H, d, d), 0.3),
        "state_mlstm_n": nrm((DEPTH, DEC_BATCH, H, d), 0.3),
        "state_mlstm_m": nrm((DEPTH, DEC_BATCH, H), 0.5),
        "state_gdn": nrm((DEPTH, DEC_BATCH, H, d, d), 0.1),
        "state_gdn_conv": nrm((DEPTH, DEC_BATCH, CONV_W - 1, 3 * GW), 1.0),
        "state_gla": nrm((DEPTH, DEC_BATCH, H, GLA_KEY_DIM, d), 0.3),
        "cache_mem_k": nrm((DEPTH, DEC_BATCH, MEM_LEN, D_MODEL), 1.0),
        "cache_mem_v": nrm((DEPTH, DEC_BATCH, MEM_LEN, D_MODEL), 1.0),
        "mem_prompt": nrm((BATCH, MEM_LEN, D_MODEL), 1.0),
        "norm_mix_pre": gain((DEPTH, D_MODEL)),
        "norm_mix_post": gain((DEPTH, D_MODEL)),
        "w_in": nrm((DEPTH, D_MODEL, IN_WIDTH), D_MODEL ** -0.5),
        "ml_ib": -1.0 + 0.3 * jax.random.normal(next(ks), (DEPTH, H), f32),
        "ml_fb": 3.0 + 0.3 * jax.random.normal(next(ks), (DEPTH, H), f32),
        "ml_norm": gain((DEPTH, GW)),
        "gd_conv": nrm((DEPTH, CONV_W, 3 * GW), CONV_W ** -0.5),
        "gd_A_log": jnp.log(jax.random.uniform(next(ks), (DEPTH, H), f32, minval=0.5, maxval=4.0)),
        "gd_dt_bias": -2.0 + 0.3 * jax.random.normal(next(ks), (DEPTH, H), f32),
        "gd_norm": gain((DEPTH, d)),
        "gl_w2": nrm((DEPTH, GLA_RANK, H * GLA_KEY_DIM), GLA_RANK ** -0.5),
        "gl_b": nrm((DEPTH, H * GLA_KEY_DIM), 0.01),
        "gl_norm": gain((DEPTH, d)),
        "w_out": nrm((DEPTH, N_MIXERS * GW, D_MODEL), (N_MIXERS * GW) ** -0.5),
        "norm_x_pre": gain((DEPTH, D_MODEL)),
        "norm_x_post": gain((DEPTH, D_MODEL)),
        "norm_mem": gain((DEPTH, D_MODEL)),
        "w_xq": nrm((DEPTH, D_MODEL, D_MODEL), D_MODEL ** -0.5),
        "w_xk": nrm((DEPTH, D_MODEL, D_MODEL), D_MODEL ** -0.5),
        "w_xv": nrm((DEPTH, D_MODEL, D_MODEL), D_MODEL ** -0.5),
        "w_xo": nrm((DEPTH, D_MODEL, D_MODEL), D_MODEL ** -0.5),
        "norm_mlp_pre": gain((DEPTH, D_MODEL)),
        "norm_mlp_post": gain((DEPTH, D_MODEL)),
        "w_up": nrm((DEPTH, D_MODEL, D_FF), D_MODEL ** -0.5),
        "w_down": nrm((DEPTH, D_FF, D_MODEL), D_FF ** -0.5),
    }


def reference(x_prompt, x_sample, state_ret, state_mlstm_C, state_mlstm_n, state_mlstm_m,
              state_gdn, state_gdn_conv, state_gla, cache_mem_k, cache_mem_v, mem_prompt,
              norm_mix_pre, norm_mix_post, w_in, ml_ib, ml_fb, ml_norm, gd_conv, gd_A_log,
              gd_dt_bias, gd_norm, gl_w2, gl_b, gl_norm, w_out, norm_x_pre, norm_x_post,
              norm_mem, w_xq, w_xk, w_xv, w_xo, norm_mlp_pre, norm_mlp_post, w_up, w_down):
    f32 = jnp.float32
    Bp, Tp = x_prompt.shape[0], x_prompt.shape[1]
    Ts = x_sample.shape[1]
    pos_p = jnp.arange(Tp, dtype=f32)
    pos_s = jnp.arange(Ts, dtype=f32) + PAST_LEN
    xp, xs = x_prompt, x_sample
    new_p = {name: [] for name in STATE_NAMES + ("mem_k", "mem_v")}
    new_s = {name: [] for name in STATE_NAMES}
    for l in range(DEPTH):
        lw = {"norm_mix_pre": norm_mix_pre[l], "norm_mix_post": norm_mix_post[l], "w_in": w_in[l],
              "ml_ib": ml_ib[l], "ml_fb": ml_fb[l], "ml_norm": ml_norm[l], "gd_conv": gd_conv[l],
              "gd_A_log": gd_A_log[l], "gd_dt_bias": gd_dt_bias[l], "gd_norm": gd_norm[l],
              "gl_w2": gl_w2[l], "gl_b": gl_b[l], "gl_norm": gl_norm[l], "w_out": w_out[l],
              "norm_x_pre": norm_x_pre[l], "norm_x_post": norm_x_post[l], "w_xq": w_xq[l], "w_xo": w_xo[l],
              "norm_mlp_pre": norm_mlp_pre[l], "norm_mlp_post": norm_mlp_post[l],
              "w_up": w_up[l], "w_down": w_down[l]}
        mem_h = _rms(mem_prompt, norm_mem[l]).astype(mem_prompt.dtype)
        mk_p = mem_h @ w_xk[l]
        mv_p = mem_h @ w_xv[l]
        xp, st_p = _layer(xp, pos_p, _zero_state(Bp), mk_p, mv_p, lw)
        st_in = {"ret": state_ret[l], "ml_C": state_mlstm_C[l], "ml_n": state_mlstm_n[l],
                 "ml_m": state_mlstm_m[l], "gdn": state_gdn[l], "gdn_conv": state_gdn_conv[l],
                 "gla": state_gla[l]}
        xs, st_s = _layer(xs, pos_s, st_in, cache_mem_k[l], cache_mem_v[l], lw)
        for name in STATE_NAMES:
            new_p[name].append(st_p[name])
            new_s[name].append(st_s[name])
        new_p["mem_k"].append(mk_p)
        new_p["mem_v"].append(mv_p)

    def stk(lst, dt):
        return jnp.stack(lst, axis=0).astype(dt)

    pd = x_prompt.dtype
    return (xp, xs,
            stk(new_p["ret"], pd), stk(new_p["ml_C"], pd), stk(new_p["ml_n"], pd), stk(new_p["ml_m"], pd),
            stk(new_p["gdn"], pd), stk(new_p["gdn_conv"], pd), stk(new_p["gla"], pd),
            stk(new_p["mem_k"], pd), stk(new_p["mem_v"], pd),
            stk(new_s["ret"], state_ret.dtype), stk(new_s["ml_C"], state_mlstm_C.dtype),
            stk(new_s["ml_n"], state_mlstm_n.dtype), stk(new_s["ml_m"], state_mlstm_m.dtype),
            stk(new_s["gdn"], state_gdn.dtype), stk(new_s["gdn_conv"], state_gdn_conv.dtype),
            stk(new_s["gla"], state_gla.dtype))
```

```python
import functools
import math

import jax
import jax.numpy as jnp
from jax import lax
from jax.experimental import pallas as pl
from jax.experimental.pallas import tpu as pltpu

f32 = jnp.float32
bf16 = jnp.bfloat16

D_MODEL = 2048
HEADS = 4
HEAD_DIM = 128
GROUP_WIDTH = 512
GLA_KEY_DIM = 64
GLA_RANK = 16
GLA_GATE_DENOM = 16.0
CONV_W = 4
CHUNK = 64
ROPE_BASE = 10000.0
X_HEADS = 4
X_HEAD_DIM = 512
MEM_LEN = 256
D_FF = 8192
EPS = 1e-6
PAST_LEN = 16384
NEG_INF = float("-inf")

LANES = 128
SUBLANES = 8

C_OFF = {
    "ret_q": 0, "ret_k": 512, "ret_v": 1024, "ret_g": 1536,
    "ml_q": 2048, "ml_k": 2560, "ml_v": 3072, "ml_o": 3584,
    "gd_q": 4096, "gd_k": 4608, "gd_v": 5120, "gd_z": 5632,
    "gl_q": 6144, "gl_k": 6400, "gl_v": 6656, "gl_z": 7168,
    "small": 7680,
}
C_WIDTH = 8192
L_MLI, L_MLF, L_GDB, L_GDA, L_GLR = 0, 4, 8, 12, 16

ROW_TILE = 640
MIX_TB = 256
SAMPLE_G = 8
VMEM_LIMIT = 56 * 1024 * 1024


def _cparams(sem):
    return pltpu.CompilerParams(dimension_semantics=sem, vmem_limit_bytes=VMEM_LIMIT)


def _dot(a, b):
    return jnp.dot(a.astype(bf16), b.astype(bf16), preferred_element_type=f32)


def _dot_nt(a, b):
    return lax.dot_general(a.astype(bf16), b.astype(bf16), (((1,), (1,)), ((), ())),
                           preferred_element_type=f32)


def _dot_tn(a, b):
    return lax.dot_general(a.astype(bf16), b.astype(bf16), (((0,), (0,)), ((), ())),
                           preferred_element_type=f32)


def _dot_exact(a, b):
    return jnp.dot(a, b, preferred_element_type=f32, precision=lax.Precision.HIGHEST)


def _rms_rows(x):
    return x * lax.rsqrt(jnp.mean(x * x, axis=-1, keepdims=True) + EPS)


def _l2n_rows(x):
    return x * lax.rsqrt(jnp.sum(x * x, axis=-1, keepdims=True) + EPS)


def _silu(x):
    return x * jax.nn.sigmoid(x)


def _rope(x, cosf, sinf):
    return x * cosf + pltpu.roll(x, HEAD_DIM // 2, 1) * sinf


def _chunk_masks(tb):
    r = lax.broadcasted_iota(jnp.int32, (tb, tb), 0)
    c = lax.broadcasted_iota(jnp.int32, (tb, tb), 1)
    same = (r // CHUNK) == (c // CHUNK)
    tril = jnp.logical_and(same, r >= c)
    strict = jnp.logical_and(same, r > c)
    return r, c, same, tril, strict


def _norm_matmul_kernel(x_ref, g_ref, w_ref, o_ref, h_ref, *, act):
    @pl.when(pl.program_id(1) == 0)
    def _():
        x = x_ref[...]
        y = x * lax.rsqrt(jnp.mean(x * x, axis=-1, keepdims=True) + EPS)
        h_ref[...] = (y * g_ref[...]).astype(bf16)

    y = jnp.dot(h_ref[...], w_ref[...], preferred_element_type=f32)
    if act == "relu2":
        y = jnp.square(jnp.maximum(y, 0.0))
    o_ref[...] = y.astype(o_ref.dtype)


def _norm_matmul(x, g, w, *, tm, tn, out_dtype, act=None):
    m, k = x.shape
    n = w.shape[1]
    assert m % tm == 0 and n % tn == 0
    return pl.pallas_call(
        functools.partial(_norm_matmul_kernel, act=act),
        grid=(m // tm, n // tn),
        in_specs=[pl.BlockSpec((tm, k), lambda i, j: (i, 0)),
                  pl.BlockSpec((1, k), lambda i, j: (0, 0)),
                  pl.BlockSpec((k, tn), lambda i, j: (0, j))],
        out_specs=pl.BlockSpec((tm, tn), lambda i, j: (i, j)),
        out_shape=jax.ShapeDtypeStruct((m, n), out_dtype),
        scratch_shapes=[pltpu.VMEM((tm, k), bf16)],
        compiler_params=_cparams(("parallel", "arbitrary")),
    )(x, g.reshape(1, k), w)


def _matmul_norm_res_kernel(a_ref, w_ref, g_ref, r_ref, o_ref, acc_ref):
    kk = pl.program_id(1)

    @pl.when(kk == 0)
    def _():
        acc_ref[...] = jnp.zeros_like(acc_ref)

    acc_ref[...] += jnp.dot(a_ref[...], w_ref[...], preferred_element_type=f32)

    @pl.when(kk == pl.num_programs(1) - 1)
    def _():
        y = acc_ref[...]
        y = y * lax.rsqrt(jnp.mean(y * y, axis=-1, keepdims=True) + EPS)
        o_ref[...] = r_ref[...] + y * g_ref[...]


def _matmul_norm_res(a, w, g, res, *, tm, tk):
    m, k = a.shape
    n = w.shape[1]
    assert m % tm == 0 and k % tk == 0
    return pl.pallas_call(
        _matmul_norm_res_kernel,
        grid=(m // tm, k // tk),
        in_specs=[pl.BlockSpec((tm, tk), lambda i, j: (i, j)),
                  pl.BlockSpec((tk, n), lambda i, j: (j, 0)),
                  pl.BlockSpec((1, n), lambda i, j: (0, 0)),
                  pl.BlockSpec((tm, n), lambda i, j: (i, 0))],
        out_specs=pl.BlockSpec((tm, n), lambda i, j: (i, 0)),
        out_shape=jax.ShapeDtypeStruct((m, n), f32),
        scratch_shapes=[pltpu.VMEM((tm, n), f32)],
        compiler_params=_cparams(("parallel", "arbitrary")),
    )(a, w, g.reshape(1, n), res)


def _xattn_prompt_kernel(q_ref, k_ref, v_ref, o_ref, kb_ref, vb_ref):
    @pl.when(pl.program_id(1) == 0)
    def _():
        kb_ref[...] = k_ref[0].astype(bf16)
        vb_ref[...] = v_ref[0].astype(bf16)

    scale = X_HEAD_DIM ** -0.5
    for h in range(X_HEADS):
        sl = slice(h * X_HEAD_DIM, (h + 1) * X_HEAD_DIM)
        s = lax.dot_general(q_ref[:, sl], kb_ref[:, sl], (((1,), (1,)), ((), ())),
                            preferred_element_type=f32) * scale
        m = jnp.max(s, axis=-1, keepdims=True)
        p = jnp.exp(s - m)
        l = jnp.sum(p, axis=-1, keepdims=True)
        o = jnp.dot(p.astype(bf16), vb_ref[:, sl], preferred_element_type=f32) / l
        o_ref[:, sl] = o.astype(o_ref.dtype)


def _xattn_prompt(q_all, mk, mv, *, batch, seq, tq):
    nt = seq // tq
    return pl.pallas_call(
        _xattn_prompt_kernel,
        grid=(batch, nt),
        in_specs=[pl.BlockSpec((tq, D_MODEL), lambda b, t: (b * nt + t, 0)),
                  pl.BlockSpec((1, MEM_LEN, D_MODEL), lambda b, t: (b, 0, 0)),
                  pl.BlockSpec((1, MEM_LEN, D_MODEL), lambda b, t: (b, 0, 0))],
        out_specs=pl.BlockSpec((tq, D_MODEL), lambda b, t: (b * nt + t, 0)),
        out_shape=jax.ShapeDtypeStruct((batch * seq, D_MODEL), bf16),
        scratch_shapes=[pltpu.VMEM((MEM_LEN, D_MODEL), bf16), pltpu.VMEM((MEM_LEN, D_MODEL), bf16)],
        compiler_params=_cparams(("parallel", "arbitrary")),
    )(q_all, mk, mv)


def _xattn_sample_kernel(q_ref, k_ref, v_ref, o_ref):
    scale = X_HEAD_DIM ** -0.5
    q = q_ref[0].astype(f32)
    for h in range(X_HEADS):
        sl = slice(h * X_HEAD_DIM, (h + 1) * X_HEAD_DIM)
        s = jnp.sum(k_ref[0, :, sl] * q[:, sl], axis=-1, keepdims=True) * scale
        m = jnp.max(s, axis=0, keepdims=True)
        p = jnp.exp(s - m)
        l = jnp.sum(p, axis=0, keepdims=True)
        o = jnp.sum(p * v_ref[0, :, sl], axis=0, keepdims=True) / l
        o_ref[0, :, sl] = o.astype(o_ref.dtype)


def _xattn_sample(q_s, ck, cv):
    nb = q_s.shape[0]
    return pl.pallas_call(
        _xattn_sample_kernel,
        grid=(nb,),
        in_specs=[pl.BlockSpec((1, 1, D_MODEL), lambda b: (b, 0, 0)),
                  pl.BlockSpec((1, MEM_LEN, D_MODEL), lambda b: (b, 0, 0)),
                  pl.BlockSpec((1, MEM_LEN, D_MODEL), lambda b: (b, 0, 0))],
        out_specs=pl.BlockSpec((1, 1, D_MODEL), lambda b: (b, 0, 0)),
        out_shape=jax.ShapeDtypeStruct((nb, 1, D_MODEL), bf16),
        compiler_params=_cparams(("parallel",)),
    )(q_s, ck, cv)


def _ret_prompt_kernel(q_ref, k_ref, v_ref, g_ref, cos_ref, sin_ref, y_ref, s_out, s_s, *, tb):
    t = pl.program_id(1)
    nc = tb // CHUNK

    @pl.when(t == 0)
    def _():
        s_s[...] = jnp.zeros_like(s_s)

    cosf = cos_ref[...]
    sinf = sin_ref[...]
    r, c, _, tril, _ = _chunk_masks(tb)
    diff = (r - c).astype(f32)
    ridx = (lax.broadcasted_iota(jnp.int32, (tb, 1), 0) % CHUNK).astype(f32)
    for h in range(HEADS):
        hs = slice(h * HEAD_DIM, (h + 1) * HEAD_DIM)
        lg = math.log(1.0 - 2.0 ** (-5.0 - h))
        decay = jnp.where(tril, jnp.exp(jnp.maximum(diff, 0.0) * lg), 0.0)
        qr = _rope(q_ref[:, hs], cosf, sinf)
        kr = _rope(k_ref[:, hs], cosf, sinf) * HEAD_DIM ** -0.5
        vb = v_ref[:, hs].astype(bf16)
        s = _dot_nt(qr, kr) * decay
        o_in = _dot(s, vb)
        qd = (qr * jnp.exp((ridx + 1.0) * lg)).astype(bf16)
        kd = (kr * jnp.exp((CHUNK - 1.0 - ridx) * lg)).astype(bf16)
        cdec = math.exp(CHUNK * lg)
        st = s_s[h]
        outs = []
        for cc in range(nc):
            sl = slice(cc * CHUNK, (cc + 1) * CHUNK)
            outs.append(o_in[sl] + _dot(qd[sl], st))
            st = st * cdec + _dot_tn(kd[sl], vb[sl])
        s_s[h] = st
        o = jnp.concatenate(outs, axis=0)
        y_ref[:, hs] = (_rms_rows(o) * _silu(g_ref[:, hs])).astype(y_ref.dtype)

    @pl.when(t == pl.num_programs(1) - 1)
    def _():
        s_out[0] = s_s[...]


def _mlstm_prompt_kernel(q_ref, k_ref, v_ref, og_ref, sm_ref, gb_ref, nw_ref,
                         y_ref, c_out, n_out, m_out, c_s, n_s, m_s, *, tb):
    t = pl.program_id(1)
    nc = tb // CHUNK

    @pl.when(t == 0)
    def _():
        c_s[...] = jnp.zeros_like(c_s)
        n_s[...] = jnp.zeros_like(n_s)
        m_s[...] = jnp.zeros_like(m_s)

    _, _, _, tril, _ = _chunk_masks(tb)
    ltri = tril.astype(f32)
    pre = sm_ref[...] + gb_ref[...]
    fall = _dot_exact(ltri, jax.nn.log_sigmoid(pre))
    imf_t = (pre - pltpu.roll(fall, LANES - (L_MLF - L_MLI), 1)).T
    for h in range(HEADS):
        hs = slice(h * HEAD_DIM, (h + 1) * HEAD_DIM)
        fcol = fall[:, L_MLF + h:L_MLF + h + 1]
        icol = pre[:, L_MLI + h:L_MLI + h + 1]
        dm = jnp.where(tril, fcol + imf_t[h:h + 1, :], NEG_INF)
        dmax = jnp.max(dm, axis=-1, keepdims=True)
        mp = m_s[h:h + 1, 0:1]
        m_prev, m_int, m_cur = [], [], []
        for cc in range(nc):
            sl = slice(cc * CHUNK, (cc + 1) * CHUNK)
            mi = mp + fcol[sl]
            mt = jnp.maximum(mi, dmax[sl])
            m_prev.append(mp)
            m_int.append(mi)
            m_cur.append(mt)
            mp = mt[CHUNK - 1:CHUNK, :]
        m_t = jnp.concatenate(m_cur, axis=0)
        w = jnp.exp(dm - m_t)
        qh = q_ref[:, hs]
        kh = k_ref[:, hs] * HEAD_DIM ** -0.5
        vb = v_ref[:, hs].astype(bf16)
        s = _dot_nt(qh, kh) * w
        num_in = _dot(s, vb)
        den_in = jnp.sum(s, axis=-1, keepdims=True)
        cst = c_s[h]
        nst = n_s[h:h + 1, :]
        outs = []
        for cc in range(nc):
            sl = slice(cc * CHUNK, (cc + 1) * CHUNK)
            mi, mt = m_int[cc], m_cur[cc]
            gi = jnp.exp(mi - mt)
            qc = qh[sl]
            num = num_in[sl] + gi * _dot(qc, cst)
            den = den_in[sl] + gi * jnp.sum(qc * nst, axis=-1, keepdims=True)
            outs.append(num / jnp.maximum(jnp.abs(den), jnp.exp(-mt)))
            m_new = mt[CHUNK - 1:CHUNK, :]
            fl = fcol[sl][CHUNK - 1:CHUNK, :]
            wk = jnp.exp(fl - fcol[sl] + icol[sl] - m_new)
            dec = jnp.exp(m_prev[cc] + fl - m_new)
            kw = kh[sl] * wk
            cst = dec * cst + _dot_tn(kw, vb[sl])
            nst = dec * nst + jnp.sum(kw, axis=0, keepdims=True)
        c_s[h] = cst
        n_s[h:h + 1, :] = nst
        m_s[h:h + 1, :] = jnp.broadcast_to(mp, (1, LANES))
        hh = jnp.concatenate(outs, axis=0)
        y = _rms_rows(hh) * nw_ref[:, hs] * jax.nn.sigmoid(og_ref[:, hs])
        y_ref[:, hs] = y.astype(y_ref.dtype)

    @pl.when(t == pl.num_programs(1) - 1)
    def _():
        c_out[0] = c_s[...]
        n_out[0] = n_s[0:HEADS, :]
        m_out[0] = m_s[...]


def _gdn_prompt_kernel(q_ref, k_ref, v_ref, z_ref, sm_ref, gb_ref, al_ref, wc_ref, cs_ref, nw_ref,
                       y_ref, s_out, conv_out, s_s, xb_s, vn_s, *, tb):
    t = pl.program_id(1)
    nc = tb // CHUNK
    gw = GROUP_WIDTH
    pad = SUBLANES
    hist = CONV_W - 1

    @pl.when(t == 0)
    def _():
        s_s[...] = jnp.zeros_like(s_s)
        for p in range(3):
            xb_s[p, 0:pad, :] = jnp.zeros((pad, gw), f32)
            xb_s[p, pad - hist:pad, :] = cs_ref[0, :, p * gw:(p + 1) * gw]

    @pl.when(t > 0)
    def _():
        for p in range(3):
            xb_s[p, 0:pad, :] = xb_s[p, tb:tb + pad, :]

    xb_s[0, pad:, :] = q_ref[...]
    xb_s[1, pad:, :] = k_ref[...]
    xb_s[2, pad:, :] = v_ref[...]

    conv = []
    for p in range(3):
        acc = xb_s[p, pad - hist:pad - hist + tb, :] * wc_ref[0:1, p * gw:(p + 1) * gw]
        for j in range(1, CONV_W):
            acc = acc + xb_s[p, pad - hist + j:pad - hist + j + tb, :] * wc_ref[j:j + 1, p * gw:(p + 1) * gw]
        conv.append(_silu(acc))

    _, _, _, tril, strict = _chunk_masks(tb)
    ltri = tril.astype(f32)
    pre = sm_ref[...] + gb_ref[...]
    beta_all = jax.nn.sigmoid(pre)
    g_all = -jnp.exp(al_ref[...]) * jax.nn.softplus(pre)
    gall = _dot_exact(ltri, g_all)
    gall_t = gall.T
    for h in range(HEADS):
        hs = slice(h * HEAD_DIM, (h + 1) * HEAD_DIM)
        gcol = gall[:, L_GDA + h:L_GDA + h + 1]
        grow = gall_t[L_GDA + h:L_GDA + h + 1, :]
        bcol = beta_all[:, L_GDB + h:L_GDB + h + 1]
        decay = jnp.where(tril, jnp.exp(jnp.where(tril, gcol - grow, 0.0)), 0.0)
        qh = _l2n_rows(conv[0][:, hs]) * HEAD_DIM ** -0.5
        kh = _l2n_rows(conv[1][:, hs])
        vh = conv[2][:, hs]
        kb = kh * bcol
        egc = jnp.exp(gcol)
        pj = jnp.where(strict, -(_dot_nt(kb, kh) * decay), 0.0)
        x = jnp.concatenate([vh * bcol, kb * egc], axis=1)
        n_terms = CHUNK.bit_length() - 1
        for j in range(n_terms):
            x = x + _dot(pj, x)
            if j + 1 < n_terms:
                pj = _dot(pj, pj)
        u = x[:, :HEAD_DIM]
        w = x[:, HEAD_DIM:]
        attn = (_dot_nt(qh, kh) * decay).astype(bf16)
        qe = qh * egc
        st = s_s[h]
        vn_s[...] = jnp.zeros_like(vn_s)
        outs = []
        for cc in range(nc):
            sl = slice(cc * CHUNK, (cc + 1) * CHUNK)
            vnew = u[sl] - _dot(w[sl], st)
            vn_s[sl, :] = vnew.astype(bf16)
            outs.append(_dot(qe[sl], st) + jnp.dot(attn[sl, :], vn_s[...], preferred_element_type=f32))
            gl = gcol[sl][CHUNK - 1:CHUNK, :]
            kdec = kh[sl] * jnp.exp(gl - gcol[sl])
            st = st * jnp.exp(gl) + _dot_tn(kdec, vnew)
        s_s[h] = st
        o = jnp.concatenate(outs, axis=0)
        y = _rms_rows(o) * nw_ref[...] * _silu(z_ref[:, hs])
        y_ref[:, hs] = y.astype(y_ref.dtype)

    @pl.when(t == pl.num_programs(1) - 1)
    def _():
        s_out[0] = s_s[...]
        for p in range(3):
            conv_out[0, :, p * gw:(p + 1) * gw] = xb_s[p, tb + pad - hist:tb + pad, :]


def _gla_prompt_kernel(q_ref, k_ref, v_ref, z_ref, sm_ref, w2_ref, b_ref, nw_ref,
                       y_ref, s_out, s_s, *, tb):
    t = pl.program_id(1)
    nc = tb // CHUNK
    dk = GLA_KEY_DIM
    sub = 16
    nsub = CHUNK // sub

    @pl.when(t == 0)
    def _():
        s_s[...] = jnp.zeros_like(s_s)

    r, c, same, tril, _ = _chunk_masks(tb)
    ltri = tril.astype(f32)
    pre = _dot_exact(sm_ref[...], w2_ref[...]) + b_ref[...]
    gk = jax.nn.log_sigmoid(pre) * (1.0 / GLA_GATE_DENOM)
    g = _dot_exact(ltri, gk)
    q = q_ref[...] * dk ** -0.5
    k = k_ref[...]
    rpos = lax.broadcasted_iota(jnp.int32, (tb, 1), 0) % CHUNK
    lane = lax.broadcasted_iota(jnp.int32, (1, LANES), 1)
    lane_lo = lane < dk

    a_off = [jnp.zeros((tb, tb), f32) for _ in range(HEADS)]
    g3 = g.reshape(nc, CHUNK, HEADS * dk)
    for i in range(1, nsub):
        ri = jnp.broadcast_to(g3[:, sub * i:sub * i + 1, :], (nc, CHUNK, HEADS * dk)).reshape(tb, HEADS * dk)
        qs = q * jnp.exp(jnp.where(rpos // sub == i, g - ri, NEG_INF))
        ks = k * jnp.exp(jnp.where(rpos < sub * i, ri - g, NEG_INF))
        for h in range(HEADS):
            ps = slice((h // 2) * LANES, (h // 2 + 1) * LANES)
            keep = lane_lo if h % 2 == 0 else jnp.logical_not(lane_lo)
            a_off[h] = a_off[h] + _dot_nt(jnp.where(keep, qs[:, ps], 0.0), ks[:, ps])

    a_diag = [jnp.zeros((tb, tb), f32) for _ in range(HEADS)]
    for j in range(sub):
        ksh = k if j == 0 else pltpu.roll(k, j, 0)
        gsh = g if j == 0 else pltpu.roll(g, j, 0)
        valid = (rpos % sub) >= j
        prod = q * ksh * jnp.exp(jnp.where(valid, g - gsh, NEG_INF))
        on_diag = c == (r - j)
        for h in range(HEADS):
            a = jnp.sum(prod[:, h * dk:(h + 1) * dk], axis=-1, keepdims=True)
            a_diag[h] = a_diag[h] + jnp.where(on_diag, a, 0.0)

    eg = jnp.exp(g)
    qe = q * eg
    g_t = g.T
    for p in range(HEADS // 2):
        ps = slice(p * LANES, (p + 1) * LANES)
        st = s_s[p]
        o_in = []
        vbs = []
        for hh in range(2):
            h = 2 * p + hh
            hs = slice(h * HEAD_DIM, (h + 1) * HEAD_DIM)
            vbs.append(v_ref[:, hs].astype(bf16))
            a = a_diag[h] + jnp.where(same, a_off[h], 0.0)
            o_in.append(_dot(a, vbs[hh]))
        outs = [[], []]
        for cc in range(nc):
            sl = slice(cc * CHUNK, (cc + 1) * CHUNK)
            last = cc * CHUNK + CHUNK - 1
            gl_row = g[last:last + 1, ps]
            gl_col = g_t[ps, last:last + 1]
            kdec = k[sl, ps] * jnp.exp(gl_row - g[sl, ps])
            upd = jnp.zeros((LANES, HEAD_DIM), f32)
            for hh in range(2):
                keep = lane_lo if hh == 0 else jnp.logical_not(lane_lo)
                outs[hh].append(o_in[hh][sl] + _dot(jnp.where(keep, qe[sl, ps], 0.0), st))
                upd = upd + _dot_tn(jnp.where(keep, kdec, 0.0), vbs[hh][sl])
            st = st * jnp.exp(gl_col) + upd
        s_s[p] = st
        for hh in range(2):
            h = 2 * p + hh
            hs = slice(h * HEAD_DIM, (h + 1) * HEAD_DIM)
            o = jnp.concatenate(outs[hh], axis=0)
            y = _rms_rows(o) * nw_ref[...] * _silu(z_ref[:, hs])
            y_ref[:, hs] = y.astype(y_ref.dtype)

    @pl.when(t == pl.num_programs(1) - 1)
    def _():
        s_out[0] = s_s[...]


def _prompt_mixers(c_all, lp, *, batch, seq, conv_zero):
    tb = MIX_TB
    nt = seq // tb
    grid = (batch, nt)
    rows = batch * seq

    def cspec(name, width):
        blk = C_OFF[name] // width
        assert C_OFF[name] % width == 0
        return pl.BlockSpec((tb, width), lambda b, t: (b * nt + t, blk))

    def const2(shape):
        return pl.BlockSpec(shape, lambda b, t: (0, 0))

    yspec = pl.BlockSpec((tb, GROUP_WIDTH), lambda b, t: (b * nt + t, 0))
    yshape = jax.ShapeDtypeStruct((rows, GROUP_WIDTH), bf16)
    sspec = pl.BlockSpec((1, HEADS, HEAD_DIM, HEAD_DIM), lambda b, t: (b, 0, 0, 0))
    sshape = jax.ShapeDtypeStruct((batch, HEADS, HEAD_DIM, HEAD_DIM), f32)
    params = _cparams(("parallel", "arbitrary"))
    gw = GROUP_WIDTH

    y_ret, s_ret = pl.pallas_call(
        functools.partial(_ret_prompt_kernel, tb=tb), grid=grid,
        in_specs=[cspec("ret_q", gw), cspec("ret_k", gw), cspec("ret_v", gw), cspec("ret_g", gw),
                  pl.BlockSpec((tb, HEAD_DIM), lambda b, t: (t, 0)),
                  pl.BlockSpec((tb, HEAD_DIM), lambda b, t: (t, 0))],
        out_specs=[yspec, sspec], out_shape=[yshape, sshape],
        scratch_shapes=[pltpu.VMEM((HEADS, HEAD_DIM, HEAD_DIM), f32)],
        compiler_params=params,
    )(c_all, c_all, c_all, c_all, lp["cos_p"], lp["sin_p"])

    y_ml, s_c, s_n, s_m = pl.pallas_call(
        functools.partial(_mlstm_prompt_kernel, tb=tb), grid=grid,
        in_specs=[cspec("ml_q", gw), cspec("ml_k", gw), cspec("ml_v", gw), cspec("ml_o", gw),
                  cspec("small", LANES), const2((1, LANES)), const2((1, gw))],
        out_specs=[yspec, sspec,
                   pl.BlockSpec((1, HEADS, HEAD_DIM), lambda b, t: (b, 0, 0)),
                   pl.BlockSpec((1, SUBLANES, LANES), lambda b, t: (b, 0, 0))],
        out_shape=[yshape, sshape,
                   jax.ShapeDtypeStruct((batch, HEADS, HEAD_DIM), f32),
                   jax.ShapeDtypeStruct((batch, SUBLANES, LANES), f32)],
        scratch_shapes=[pltpu.VMEM((HEADS, HEAD_DIM, HEAD_DIM), f32),
                        pltpu.VMEM((SUBLANES, HEAD_DIM), f32),
                        pltpu.VMEM((SUBLANES, LANES), f32)],
        compiler_params=params,
    )(c_all, c_all, c_all, c_all, c_all, lp["gate_bias"], lp["ml_norm"])

    y_gd, s_gdn, s_conv = pl.pallas_call(
        functools.partial(_gdn_prompt_kernel, tb=tb), grid=grid,
        in_specs=[cspec("gd_q", gw), cspec("gd_k", gw), cspec("gd_v", gw), cspec("gd_z", gw),
                  cspec("small", LANES), const2((1, LANES)), const2((1, LANES)),
                  const2((CONV_W, 3 * gw)),
                  pl.BlockSpec((1, CONV_W - 1, 3 * gw), lambda b, t: (b, 0, 0)),
                  const2((1, HEAD_DIM))],
        out_specs=[yspec, sspec, pl.BlockSpec((1, CONV_W - 1, 3 * gw), lambda b, t: (b, 0, 0))],
        out_shape=[yshape, sshape, jax.ShapeDtypeStruct((batch, CONV_W - 1, 3 * gw), f32)],
        scratch_shapes=[pltpu.VMEM((HEADS, HEAD_DIM, HEAD_DIM), f32),
                        pltpu.VMEM((3, tb + SUBLANES, gw), f32),
                        pltpu.VMEM((tb, HEAD_DIM), bf16)],
        compiler_params=params,
    )(c_all, c_all, c_all, c_all, c_all, lp["gate_bias"], lp["alog_row"], lp["gd_conv"], conv_zero,
      lp["gd_norm"])

    kw = HEADS * GLA_KEY_DIM
    y_gl, s_gla = pl.pallas_call(
        functools.partial(_gla_prompt_kernel, tb=tb), grid=grid,
        in_specs=[cspec("gl_q", kw), cspec("gl_k", kw), cspec("gl_v", gw), cspec("gl_z", gw),
                  cspec("small", LANES), const2((LANES, kw)), const2((1, kw)), const2((1, HEAD_DIM))],
        out_specs=[yspec, pl.BlockSpec((1, HEADS // 2, LANES, HEAD_DIM), lambda b, t: (b, 0, 0, 0))],
        out_shape=[yshape, jax.ShapeDtypeStruct((batch, HEADS // 2, LANES, HEAD_DIM), f32)],
        scratch_shapes=[pltpu.VMEM((HEADS // 2, LANES, HEAD_DIM), f32)],
        compiler_params=params,
    )(c_all, c_all, c_all, c_all, c_all, lp["gl_w2p"], lp["gl_b"], lp["gl_norm"])

    states = {"ret": s_ret, "ml_C": s_c, "ml_n": s_n, "ml_m": s_m[:, :HEADS, 0],
              "gdn": s_gdn, "gdn_conv": s_conv,
              "gla": s_gla.reshape(batch, HEADS, GLA_KEY_DIM, HEAD_DIM)}
    return (y_ret, y_ml, y_gd, y_gl), states


def _cols(x):
    g = x.shape[0]
    return jnp.concatenate([x, jnp.zeros((LANES - g, LANES), f32)], axis=0).T


def _sample_mixer_kernel(c_ref, cos_ref, sin_ref, gb_ref, al_ref, wc_ref, mlnw_ref, gdnw_ref, glnw_ref,
                         w2_ref, glb_ref,
                         ret_ref, mlc_ref, mln_ref, mlm_ref, gdn_ref, conv_ref, gla_ref,
                         y_ref, ret_o, mlc_o, mln_o, mlm_o, gdn_o, conv_o, gla_o):
    G = SAMPLE_G
    gw = GROUP_WIDTH
    dk = GLA_KEY_DIM

    def cblk(name, width):
        return c_ref[:, C_OFF[name]:C_OFF[name] + width]

    small = cblk("small", LANES)
    pre = small + gb_ref[...]
    cosf = cos_ref[...]
    sinf = sin_ref[...]

    for h in range(HEADS):
        hs = slice(h * HEAD_DIM, (h + 1) * HEAD_DIM)
        gamma = 1.0 - 2.0 ** (-5.0 - h)
        q = _rope(cblk("ret_q", gw)[:, hs], cosf, sinf)
        k = _rope(cblk("ret_k", gw)[:, hs], cosf, sinf) * HEAD_DIM ** -0.5
        v = cblk("ret_v", gw)[:, hs]
        qt, kt = _cols(q), _cols(k)
        rows = []
        for j in range(G):
            s_new = gamma * ret_ref[j, h] + kt[:, j:j + 1] * v[j:j + 1, :]
            ret_o[j, h] = s_new
            rows.append(jnp.sum(qt[:, j:j + 1] * s_new, axis=0, keepdims=True))
        o = jnp.concatenate(rows, axis=0)
        y = _rms_rows(o) * _silu(cblk("ret_g", gw)[:, hs])
        y_ref[:, 0 * gw + h * HEAD_DIM:0 * gw + (h + 1) * HEAD_DIM] = y.astype(y_ref.dtype)

    m_old = mlm_ref[...]
    f_al = pltpu.roll(jax.nn.log_sigmoid(pre), LANES - (L_MLF - L_MLI), 1)
    m_int = m_old + f_al
    m_t = jnp.maximum(m_int, pre)
    w_in = jnp.exp(pre - m_t)
    g_in = jnp.exp(m_int - m_t)
    e_neg = jnp.exp(-m_t)
    mlm_o[...] = m_t
    for h in range(HEADS):
        hs = slice(h * HEAD_DIM, (h + 1) * HEAD_DIM)
        q = cblk("ml_q", gw)[:, hs]
        k = cblk("ml_k", gw)[:, hs] * HEAD_DIM ** -0.5
        v = cblk("ml_v", gw)[:, hs]
        qt, kt = _cols(q), _cols(k)
        rows = []
        for j in range(G):
            wj = w_in[j:j + 1, h:h + 1]
            gj = g_in[j:j + 1, h:h + 1]
            c_new = gj * mlc_ref[j, h] + (kt[:, j:j + 1] * wj) * v[j:j + 1, :]
            mlc_o[j, h] = c_new
            n_new = gj * mln_ref[j, h:h + 1, :] + wj * k[j:j + 1, :]
            mln_o[j, h:h + 1, :] = n_new
            num = jnp.sum(qt[:, j:j + 1] * c_new, axis=0, keepdims=True)
            den = jnp.sum(q[j:j + 1, :] * n_new, axis=-1, keepdims=True)
            rows.append(num / jnp.maximum(jnp.abs(den), e_neg[j:j + 1, h:h + 1]))
        hh = jnp.concatenate(rows, axis=0)
        y = _rms_rows(hh) * mlnw_ref[:, hs] * jax.nn.sigmoid(cblk("ml_o", gw)[:, hs])
        y_ref[:, 1 * gw + h * HEAD_DIM:1 * gw + (h + 1) * HEAD_DIM] = y.astype(y_ref.dtype)

    beta_all = jax.nn.sigmoid(pre)
    eg_all = jnp.exp(-jnp.exp(al_ref[...]) * jax.nn.softplus(pre))
    conv = []
    for p, name in enumerate(("gd_q", "gd_k", "gd_v")):
        ps = slice(p * gw, (p + 1) * gw)
        x = cblk(name, gw)
        acc = x * wc_ref[CONV_W - 1:CONV_W, ps]
        for j in range(CONV_W - 1):
            acc = acc + conv_ref[j, :, ps] * wc_ref[j:j + 1, ps]
        conv.append(_silu(acc))
        for j in range(CONV_W - 2):
            conv_o[j, :, ps] = conv_ref[j + 1, :, ps]
        conv_o[CONV_W - 2, :, ps] = x
    for h in range(HEADS):
        hs = slice(h * HEAD_DIM, (h + 1) * HEAD_DIM)
        q = _l2n_rows(conv[0][:, hs]) * HEAD_DIM ** -0.5
        k = _l2n_rows(conv[1][:, hs])
        v = conv[2][:, hs]
        qt, kt = _cols(q), _cols(k)
        rows = []
        for j in range(G):
            bj = beta_all[j:j + 1, L_GDB + h:L_GDB + h + 1]
            ej = eg_all[j:j + 1, L_GDA + h:L_GDA + h + 1]
            s_old = gdn_ref[j, h]
            ks = jnp.sum(kt[:, j:j + 1] * s_old, axis=0, keepdims=True)
            v_new = bj * (v[j:j + 1, :] - ej * ks)
            s_new = ej * s_old + kt[:, j:j + 1] * v_new
            gdn_o[j, h] = s_new
            rows.append(jnp.sum(qt[:, j:j + 1] * s_new, axis=0, keepdims=True))
        o = jnp.concatenate(rows, axis=0)
        y = _rms_rows(o) * gdnw_ref[...] * _silu(cblk("gd_z", gw)[:, hs])
        y_ref[:, 2 * gw + h * HEAD_DIM:2 * gw + (h + 1) * HEAD_DIM] = y.astype(y_ref.dtype)

    gk = jax.nn.log_sigmoid(_dot_exact(small, w2_ref[...]) + glb_ref[...]) * (1.0 / GLA_GATE_DENOM)
    eg = jnp.exp(gk)
    q_all = cblk("gl_q", HEADS * dk) * dk ** -0.5
    k_all = cblk("gl_k", HEADS * dk)
    upper = lax.broadcasted_iota(jnp.int32, (LANES, 1), 0) < dk
    for p in range(HEADS // 2):
        ps = slice(p * LANES, (p + 1) * LANES)
        qt, kt, et = _cols(q_all[:, ps]), _cols(k_all[:, ps]), _cols(eg[:, ps])
        v0 = cblk("gl_v", gw)[:, (2 * p) * HEAD_DIM:(2 * p + 1) * HEAD_DIM]
        v1 = cblk("gl_v", gw)[:, (2 * p + 1) * HEAD_DIM:(2 * p + 2) * HEAD_DIM]
        rows0, rows1 = [], []
        for j in range(G):
            v2 = jnp.where(upper, v0[j:j + 1, :], v1[j:j + 1, :])
            s_new = et[:, j:j + 1] * gla_ref[j, p] + kt[:, j:j + 1] * v2
            gla_o[j, p] = s_new
            qs = qt[:, j:j + 1] * s_new
            rows0.append(jnp.sum(qs[:dk], axis=0, keepdims=True))
            rows1.append(jnp.sum(qs[dk:], axis=0, keepdims=True))
        for hh, rows in enumerate((rows0, rows1)):
            h = 2 * p + hh
            hs = slice(h * HEAD_DIM, (h + 1) * HEAD_DIM)
            o = jnp.concatenate(rows, axis=0)
            y = _rms_rows(o) * glnw_ref[...] * _silu(cblk("gl_z", gw)[:, hs])
            y_ref[:, 3 * gw + h * HEAD_DIM:3 * gw + (h + 1) * HEAD_DIM] = y.astype(y_ref.dtype)


def _sample_mixers(c_all, lp, st, *, row0, nb):
    G = SAMPLE_G
    gw = GROUP_WIDTH
    assert row0 % G == 0 and nb % G == 0
    blk0 = row0 // G

    def const2(shape):
        return pl.BlockSpec(shape, lambda i: (0, 0))

    s4 = pl.BlockSpec((G, HEADS, HEAD_DIM, HEAD_DIM), lambda i: (i, 0, 0, 0))
    s4shape = jax.ShapeDtypeStruct((nb, HEADS, HEAD_DIM, HEAD_DIM), f32)
    nspec = pl.BlockSpec((G, HEADS, HEAD_DIM), lambda i: (i, 0, 0))
    mspec = pl.BlockSpec((G, LANES), lambda i: (i, 0))
    cvspec = pl.BlockSpec((CONV_W - 1, G, 3 * gw), lambda i: (0, i, 0))
    glspec = pl.BlockSpec((G, HEADS // 2, LANES, HEAD_DIM), lambda i: (i, 0, 0, 0))
    kw = HEADS * GLA_KEY_DIM

    m_pad = jnp.pad(st["ml_m"], ((0, 0), (0, LANES - HEADS)))
    conv_t = jnp.transpose(st["gdn_conv"], (1, 0, 2))
    gla_p = st["gla"].reshape(nb, HEADS // 2, LANES, HEAD_DIM)

    outs = pl.pallas_call(
        _sample_mixer_kernel,
        grid=(nb // G,),
        in_specs=[pl.BlockSpec((G, C_WIDTH), lambda i: (blk0 + i, 0)),
                  const2((1, HEAD_DIM)), const2((1, HEAD_DIM)), const2((1, LANES)), const2((1, LANES)),
                  const2((CONV_W, 3 * gw)), const2((1, gw)), const2((1, HEAD_DIM)), const2((1, HEAD_DIM)),
                  const2((LANES, kw)), const2((1, kw)),
                  s4, s4, nspec, mspec, s4, cvspec, glspec],
        out_specs=[pl.BlockSpec((G, D_MODEL), lambda i: (i, 0)), s4, s4, nspec, mspec, s4, cvspec, glspec],
        out_shape=[jax.ShapeDtypeStruct((nb, D_MODEL), bf16), s4shape, s4shape,
                   jax.ShapeDtypeStruct((nb, HEADS, HEAD_DIM), f32),
                   jax.ShapeDtypeStruct((nb, LANES), f32), s4shape,
                   jax.ShapeDtypeStruct((CONV_W - 1, nb, 3 * gw), f32),
                   jax.ShapeDtypeStruct((nb, HEADS // 2, LANES, HEAD_DIM), f32)],
        compiler_params=_cparams(("parallel",)),
    )(c_all, lp["cos_s"], lp["sin_s"], lp["gate_bias"], lp["alog_row"], lp["gd_conv"], lp["ml_norm"],
      lp["gd_norm"], lp["gl_norm"], lp["gl_w2p"], lp["gl_b"],
      st["ret"], st["ml_C"], st["ml_n"], m_pad, st["gdn"], conv_t, gla_p)
    y, ret, mlc, mln, mlm, gdn, conv, gla = outs
    new = {"ret": ret, "ml_C": mlc, "ml_n": mln, "ml_m": mlm[:, :HEADS], "gdn": gdn,
           "gdn_conv": jnp.transpose(conv, (1, 0, 2)),
           "gla": gla.reshape(nb, HEADS, GLA_KEY_DIM, HEAD_DIM)}
    return y, new


def _rope_tables(pos):
    half = HEAD_DIM // 2
    inv = ROPE_BASE ** (-jnp.arange(half, dtype=f32) / half)
    ang = pos[:, None] * inv[None, :]
    cos, sin = jnp.cos(ang), jnp.sin(ang)
    return jnp.concatenate([cos, cos], axis=-1), jnp.concatenate([-sin, sin], axis=-1)


def _permute_w_in(w):
    d = w.shape[0]
    big = jnp.concatenate([w[:, 0:4096], w[:, 4104:6152], w[:, 6160:7696]], axis=1)
    small = jnp.concatenate([w[:, 4096:4104], w[:, 6152:6160], w[:, 7696:7712]], axis=1)
    pad = jnp.zeros((d, C_WIDTH - big.shape[1] - small.shape[1]), w.dtype)
    return jnp.concatenate([big, small, pad], axis=1)


def _lane_row(pieces):
    row = jnp.zeros((1, LANES), f32)
    for off, val in pieces:
        row = lax.dynamic_update_slice(row, val.reshape(1, -1).astype(f32), (0, off))
    return row


def kernel(x_prompt, x_sample, state_ret, state_mlstm_C, state_mlstm_n, state_mlstm_m, state_gdn, state_gdn_conv, state_gla, cache_mem_k, cache_mem_v, mem_prompt, norm_mix_pre, norm_mix_post, w_in, ml_ib, ml_fb, ml_norm, gd_conv, gd_A_log, gd_dt_bias, gd_norm, gl_w2, gl_b, gl_norm, w_out, norm_x_pre, norm_x_post, norm_mem, w_xq, w_xk, w_xv, w_xo, norm_mlp_pre, norm_mlp_post, w_up, w_down):
    bp, tp, d = x_prompt.shape
    bs, ts, _ = x_sample.shape
    depth = w_in.shape[0]
    assert ts == 1 and d == D_MODEL and tp % MIX_TB == 0
    rows_p = bp * tp
    rows = rows_p + bs
    assert rows % ROW_TILE == 0 and rows_p % SAMPLE_G == 0

    x = jnp.concatenate([x_prompt.reshape(rows_p, d), x_sample.reshape(bs, d)], axis=0)
    cos_p, sin_p = _rope_tables(jnp.arange(tp, dtype=f32))
    cos_s, sin_s = _rope_tables(jnp.arange(ts, dtype=f32) + PAST_LEN)
    conv_zero = jnp.zeros((bp, CONV_W - 1, 3 * GROUP_WIDTH), f32)
    mem2d = mem_prompt.reshape(bp * MEM_LEN, d)

    new_p = {n: [] for n in ("ret", "ml_C", "ml_n", "ml_m", "gdn", "gdn_conv", "gla", "mem_k", "mem_v")}
    new_s = {n: [] for n in ("ret", "ml_C", "ml_n", "ml_m", "gdn", "gdn_conv", "gla")}
    for l in range(depth):
        lp = {
            "cos_p": cos_p, "sin_p": sin_p, "cos_s": cos_s, "sin_s": sin_s,
            "gate_bias": _lane_row([(L_MLI, ml_ib[l]), (L_MLF, ml_fb[l]), (L_GDA, gd_dt_bias[l])]),
            "alog_row": _lane_row([(L_GDA, gd_A_log[l])]),
            "ml_norm": ml_norm[l].reshape(1, GROUP_WIDTH),
            "gd_norm": gd_norm[l].reshape(1, HEAD_DIM),
            "gl_norm": gl_norm[l].reshape(1, HEAD_DIM),
            "gd_conv": gd_conv[l],
            "gl_w2p": jnp.zeros((LANES, HEADS * GLA_KEY_DIM), f32).at[L_GLR:L_GLR + GLA_RANK].set(gl_w2[l]),
            "gl_b": gl_b[l].reshape(1, HEADS * GLA_KEY_DIM),
        }
        w_in_b = _permute_w_in(w_in[l]).astype(bf16)

        mk_p = _norm_matmul(mem2d, norm_mem[l], w_xk[l].astype(bf16), tm=512, tn=1024, out_dtype=f32)
        mv_p = _norm_matmul(mem2d, norm_mem[l], w_xv[l].astype(bf16), tm=512, tn=1024, out_dtype=f32)
        mk_p = mk_p.reshape(bp, MEM_LEN, d)
        mv_p = mv_p.reshape(bp, MEM_LEN, d)

        c_all = _norm_matmul(x, norm_mix_pre[l], w_in_b, tm=ROW_TILE, tn=1024, out_dtype=f32)
        y_p, st_p = _prompt_mixers(c_all, lp, batch=bp, seq=tp, conv_zero=conv_zero)
        st_in = {"ret": state_ret[l], "ml_C": state_mlstm_C[l], "ml_n": state_mlstm_n[l],
                 "ml_m": state_mlstm_m[l], "gdn": state_gdn[l], "gdn_conv": state_gdn_conv[l],
                 "gla": state_gla[l]}
        y_s, st_s = _sample_mixers(c_all, lp, st_in, row0=rows_p, nb=bs)
        y_all = jnp.concatenate([jnp.concatenate(y_p, axis=1), y_s], axis=0)
        x = _matmul_norm_res(y_all, w_out[l].astype(bf16), norm_mix_post[l], x, tm=ROW_TILE, tk=D_MODEL)

        q_all = _norm_matmul(x, norm_x_pre[l], w_xq[l].astype(bf16), tm=ROW_TILE, tn=1024, out_dtype=bf16)
        o_p = _xattn_prompt(q_all, mk_p, mv_p, batch=bp, seq=tp, tq=512)
        o_s = _xattn_sample(q_all[rows_p:].reshape(bs, 1, d), cache_mem_k[l], cache_mem_v[l])
        o_all = jnp.concatenate([o_p, o_s.reshape(bs, d)], axis=0)
        x = _matmul_norm_res(o_all, w_xo[l].astype(bf16), norm_x_post[l], x, tm=ROW_TILE, tk=D_MODEL)

        u = _norm_matmul(x, norm_mlp_pre[l], w_up[l].astype(bf16), tm=ROW_TILE, tn=1024, out_dtype=bf16,
                         act="relu2")
        x = _matmul_norm_res(u, w_down[l].astype(bf16), norm_mlp_post[l], x, tm=ROW_TILE, tk=1024)

        for n in new_s:
            new_p[n].append(st_p[n])
            new_s[n].append(st_s[n])
        new_p["mem_k"].append(mk_p)
        new_p["mem_v"].append(mv_p)

    def stk(lst):
        return jnp.stack(lst, axis=0)

    xp = x[:rows_p].reshape(bp, tp, d)
    xs = x[rows_p:].reshape(bs, ts, d)
    return (xp, xs,
            stk(new_p["ret"]), stk(new_p["ml_C"]), stk(new_p["ml_n"]), stk(new_p["ml_m"]),
            stk(new_p["gdn"]), stk(new_p["gdn_conv"]), stk(new_p["gla"]),
            stk(new_p["mem_k"]), stk(new_p["mem_v"]),
            stk(new_s["ret"]), stk(new_s["ml_C"]), stk(new_s["ml_n"]), stk(new_s["ml_m"]),
            stk(new_s["gdn"]), stk(new_s["gdn_conv"]), stk(new_s["gla"]))
```

```python
import functools
import math

import jax
import jax.numpy as jnp
from jax import lax
from jax.experimental import pallas as pl
from jax.experimental.pallas import tpu as pltpu

f32 = jnp.float32
bf16 = jnp.bfloat16

D_MODEL = 2048
HEADS = 4
HEAD_DIM = 128
GROUP_WIDTH = 512
GLA_KEY_DIM = 64
GLA_RANK = 16
GLA_GATE_DENOM = 16.0
CONV_W = 4
CHUNK = 64
ROPE_BASE = 10000.0
X_HEADS = 4
X_HEAD_DIM = 512
MEM_LEN = 256
D_FF = 8192
EPS = 1e-6
PAST_LEN = 16384
NEG_INF = float("-inf")

LANES = 128
SUBLANES = 8

C_OFF = {
    "ret_q": 0, "ret_k": 512, "ret_v": 1024, "ret_g": 1536,
    "ml_q": 2048, "ml_k": 2560, "ml_v": 3072, "ml_o": 3584,
    "gd_q": 4096, "gd_k": 4608, "gd_v": 5120, "gd_z": 5632,
    "gl_q": 6144, "gl_k": 6400, "gl_v": 6656, "gl_z": 7168,
    "small": 7680,
}
C_WIDTH = 8192
L_MLI, L_MLF, L_GDB, L_GDA, L_GLR = 0, 4, 8, 12, 16

ROW_TILE = 640
MIX_TB = 256
SAMPLE_G = 8
VMEM_LIMIT = 56 * 1024 * 1024


def _cparams(sem):
    return pltpu.CompilerParams(dimension_semantics=sem, vmem_limit_bytes=VMEM_LIMIT)


def _dot(a, b):
    return jnp.dot(a.astype(bf16), b.astype(bf16), preferred_element_type=f32)


def _dot_nt(a, b):
    return lax.dot_general(a.astype(bf16), b.astype(bf16), (((1,), (1,)), ((), ())),
                           preferred_element_type=f32)


def _dot_tn(a, b):
    return lax.dot_general(a.astype(bf16), b.astype(bf16), (((0,), (0,)), ((), ())),
                           preferred_element_type=f32)


def _dot_exact(a, b):
    return jnp.dot(a, b, preferred_element_type=f32, precision=lax.Precision.HIGHEST)


def _rms_rows(x):
    return x * lax.rsqrt(jnp.mean(x * x, axis=-1, keepdims=True) + EPS)


def _l2n_rows(x):
    return x * lax.rsqrt(jnp.sum(x * x, axis=-1, keepdims=True) + EPS)


def _silu(x):
    return x * jax.nn.sigmoid(x)


def _rope(x, cosf, sinf):
    return x * cosf + pltpu.roll(x, HEAD_DIM // 2, 1) * sinf


def _chunk_masks(tb):
    r = lax.broadcasted_iota(jnp.int32, (tb, tb), 0)
    c = lax.broadcasted_iota(jnp.int32, (tb, tb), 1)
    same = (r // CHUNK) == (c // CHUNK)
    tril = jnp.logical_and(same, r >= c)
    strict = jnp.logical_and(same, r > c)
    return r, c, same, tril, strict


def _norm_matmul_kernel(x_ref, g_ref, w_ref, *rest, act):
    o_ref, h_ref = rest[-2:]

    @pl.when(pl.program_id(1) == 0)
    def _():
        x = x_ref[...]
        y = x * lax.rsqrt(jnp.mean(x * x, axis=-1, keepdims=True) + EPS)
        h_ref[...] = (y * g_ref[...]).astype(bf16)

    y = jnp.dot(h_ref[...], w_ref[...], preferred_element_type=f32)
    if act == "relu2":
        y = jnp.square(jnp.maximum(y, 0.0))
    o_ref[...] = y.astype(o_ref.dtype)


def _norm_matmul(x, g, w, *, tm, tn, out_dtype, name, act=None, layer_slab=None):
    m, k = x.shape
    n = w.shape[1]
    assert m % tm == 0 and n % tn == 0
    in_specs = [pl.BlockSpec((tm, k), lambda i, j: (i, 0)),
                pl.BlockSpec((1, k), lambda i, j: (0, 0)),
                pl.BlockSpec((k, tn), lambda i, j: (0, j))]
    args = [x, g.reshape(1, k), w]
    aliases = {}
    if layer_slab is None:
        out_spec = pl.BlockSpec((tm, tn), lambda i, j: (i, j))
        out_shape = jax.ShapeDtypeStruct((m, n), out_dtype)
    else:
        layer, depth, prev = layer_slab
        out_spec = pl.BlockSpec((None, tm, tn), lambda i, j: (layer, i, j))
        out_shape = jax.ShapeDtypeStruct((depth, m, n), out_dtype)
        if prev is not None:
            in_specs.append(pl.BlockSpec(memory_space=pl.ANY))
            args.append(prev)
            aliases = {len(args) - 1: 0}
    return pl.pallas_call(
        functools.partial(_norm_matmul_kernel, act=act),
        grid=(m // tm, n // tn),
        in_specs=in_specs,
        out_specs=out_spec,
        out_shape=out_shape,
        scratch_shapes=[pltpu.VMEM((tm, k), bf16)],
        input_output_aliases=aliases,
        compiler_params=_cparams(("parallel", "arbitrary")),
        name=name,
    )(*args)


def _matmul_norm_res_kernel(a_ref, w_ref, g_ref, r_ref, o_ref, acc_ref):
    kk = pl.program_id(1)

    @pl.when(kk == 0)
    def _():
        acc_ref[...] = jnp.zeros_like(acc_ref)

    acc_ref[...] += jnp.dot(a_ref[...], w_ref[...], preferred_element_type=f32)

    @pl.when(kk == pl.num_programs(1) - 1)
    def _():
        y = acc_ref[...]
        y = y * lax.rsqrt(jnp.mean(y * y, axis=-1, keepdims=True) + EPS)
        o_ref[...] = r_ref[...] + y * g_ref[...]


def _matmul_norm_res(a, w, g, res, *, tm, tk, name):
    m, k = a.shape
    n = w.shape[1]
    assert m % tm == 0 and k % tk == 0
    return pl.pallas_call(
        _matmul_norm_res_kernel,
        grid=(m // tm, k // tk),
        in_specs=[pl.BlockSpec((tm, tk), lambda i, j: (i, j)),
                  pl.BlockSpec((tk, n), lambda i, j: (j, 0)),
                  pl.BlockSpec((1, n), lambda i, j: (0, 0)),
                  pl.BlockSpec((tm, n), lambda i, j: (i, 0))],
        out_specs=pl.BlockSpec((tm, n), lambda i, j: (i, 0)),
        out_shape=jax.ShapeDtypeStruct((m, n), f32),
        scratch_shapes=[pltpu.VMEM((tm, n), f32)],
        compiler_params=_cparams(("parallel", "arbitrary")),
        name=name,
    )(a, w, g.reshape(1, n), res)


def _xattn_prompt_kernel(q_ref, k_ref, v_ref, o_ref, kb_ref, vb_ref):
    @pl.when(pl.program_id(1) == 0)
    def _():
        kb_ref[...] = k_ref[...].astype(bf16)
        vb_ref[...] = v_ref[...].astype(bf16)

    scale = X_HEAD_DIM ** -0.5
    for h in range(X_HEADS):
        sl = slice(h * X_HEAD_DIM, (h + 1) * X_HEAD_DIM)
        s = lax.dot_general(q_ref[:, sl], kb_ref[:, sl], (((1,), (1,)), ((), ())),
                            preferred_element_type=f32) * scale
        m = jnp.max(s, axis=-1, keepdims=True)
        p = jnp.exp(s - m)
        l = jnp.sum(p, axis=-1, keepdims=True)
        o = jnp.dot(p.astype(bf16), vb_ref[:, sl], preferred_element_type=f32) / l
        o_ref[:, sl] = o.astype(o_ref.dtype)


def _xattn_prompt(q_all, mk, mv, *, layer, batch, seq, tq):
    nt = seq // tq
    return pl.pallas_call(
        _xattn_prompt_kernel,
        grid=(batch, nt),
        in_specs=[pl.BlockSpec((tq, D_MODEL), lambda b, t: (b * nt + t, 0)),
                  pl.BlockSpec((None, MEM_LEN, D_MODEL), lambda b, t: (layer, b, 0)),
                  pl.BlockSpec((None, MEM_LEN, D_MODEL), lambda b, t: (layer, b, 0))],
        out_specs=pl.BlockSpec((tq, D_MODEL), lambda b, t: (b * nt + t, 0)),
        out_shape=jax.ShapeDtypeStruct(q_all.shape, bf16),
        scratch_shapes=[pltpu.VMEM((MEM_LEN, D_MODEL), bf16), pltpu.VMEM((MEM_LEN, D_MODEL), bf16)],
        compiler_params=_cparams(("parallel", "arbitrary")),
        name="xattn_prompt",
    )(q_all, mk, mv)


def _xattn_sample_kernel(q_ref, k_ref, v_ref, prev_ref, o_ref, acc_ref):
    del prev_ref
    j = pl.program_id(1)
    scale = X_HEAD_DIM ** -0.5
    rows = lax.broadcasted_iota(jnp.int32, (SAMPLE_G, 1), 0)
    q = jnp.sum(jnp.where(rows == j, q_ref[...].astype(f32), 0.0), axis=0, keepdims=True)
    for h in range(X_HEADS):
        sl = slice(h * X_HEAD_DIM, (h + 1) * X_HEAD_DIM)
        s = jnp.sum(k_ref[:, sl] * q[:, sl], axis=-1, keepdims=True) * scale
        m = jnp.max(s, axis=0, keepdims=True)
        p = jnp.exp(s - m)
        l = jnp.sum(p, axis=0, keepdims=True)
        acc_ref[pl.ds(j, 1), sl] = jnp.sum(p * v_ref[:, sl], axis=0, keepdims=True) / l

    @pl.when(j == SAMPLE_G - 1)
    def _():
        o_ref[...] = acc_ref[...].astype(o_ref.dtype)


def _xattn_sample(q_all, ck, cv, o_prev, *, layer, row0, nb):
    G = SAMPLE_G
    blk0 = row0 // G
    return pl.pallas_call(
        _xattn_sample_kernel,
        grid=(nb // G, G),
        in_specs=[pl.BlockSpec((G, D_MODEL), lambda g, j: (blk0 + g, 0)),
                  pl.BlockSpec((None, None, MEM_LEN, D_MODEL), lambda g, j: (layer, g * G + j, 0, 0)),
                  pl.BlockSpec((None, None, MEM_LEN, D_MODEL), lambda g, j: (layer, g * G + j, 0, 0)),
                  pl.BlockSpec(memory_space=pl.ANY)],
        out_specs=pl.BlockSpec((G, D_MODEL), lambda g, j: (blk0 + g, 0)),
        out_shape=jax.ShapeDtypeStruct(o_prev.shape, o_prev.dtype),
        scratch_shapes=[pltpu.VMEM((G, D_MODEL), f32)],
        input_output_aliases={3: 0},
        compiler_params=_cparams(("parallel", "arbitrary")),
        name="xattn_sample",
    )(q_all, ck, cv, o_prev)


def _ret_prompt_body(q_ref, k_ref, v_ref, g_ref, cos_ref, sin_ref, y_ref, s_out, s_s, *, tb, y_off):
    t = pl.program_id(1)
    nc = tb // CHUNK

    @pl.when(t == 0)
    def _():
        s_s[...] = jnp.zeros_like(s_s)

    cosf = cos_ref[...]
    sinf = sin_ref[...]
    r, c, _, tril, _ = _chunk_masks(tb)
    diff = (r - c).astype(f32)
    ridx = (lax.broadcasted_iota(jnp.int32, (tb, 1), 0) % CHUNK).astype(f32)
    for h in range(HEADS):
        hs = slice(h * HEAD_DIM, (h + 1) * HEAD_DIM)
        lg = math.log(1.0 - 2.0 ** (-5.0 - h))
        decay = jnp.where(tril, jnp.exp(jnp.maximum(diff, 0.0) * lg), 0.0)
        qr = _rope(q_ref[:, hs], cosf, sinf)
        kr = _rope(k_ref[:, hs], cosf, sinf) * HEAD_DIM ** -0.5
        vb = v_ref[:, hs].astype(bf16)
        s = _dot_nt(qr, kr) * decay
        o_in = _dot(s, vb)
        qd = (qr * jnp.exp((ridx + 1.0) * lg)).astype(bf16)
        kd = (kr * jnp.exp((CHUNK - 1.0 - ridx) * lg)).astype(bf16)
        cdec = math.exp(CHUNK * lg)
        st = s_s[h]
        outs = []
        for cc in range(nc):
            sl = slice(cc * CHUNK, (cc + 1) * CHUNK)
            outs.append(o_in[sl] + _dot(qd[sl], st))
            st = st * cdec + _dot_tn(kd[sl], vb[sl])
        s_s[h] = st
        o = jnp.concatenate(outs, axis=0)
        y = _rms_rows(o) * _silu(g_ref[:, hs])
        y_ref[:, y_off + h * HEAD_DIM:y_off + (h + 1) * HEAD_DIM] = y.astype(y_ref.dtype)

    @pl.when(t == pl.num_programs(1) - 1)
    def _():
        s_out[0] = s_s[...]


def _mlstm_prompt_body(q_ref, k_ref, v_ref, og_ref, sm_ref, gb_ref, nw_ref,
                       y_ref, c_out, n_out, m_out, c_s, n_s, m_s, *, tb, y_off):
    t = pl.program_id(1)
    nc = tb // CHUNK

    @pl.when(t == 0)
    def _():
        c_s[...] = jnp.zeros_like(c_s)
        n_s[...] = jnp.zeros_like(n_s)
        m_s[...] = jnp.zeros_like(m_s)

    _, _, _, tril, _ = _chunk_masks(tb)
    ltri = tril.astype(f32)
    pre = sm_ref[...] + gb_ref[...]
    fall = _dot_exact(ltri, jax.nn.log_sigmoid(pre))
    imf_t = (pre - pltpu.roll(fall, LANES - (L_MLF - L_MLI), 1)).T
    for h in range(HEADS):
        hs = slice(h * HEAD_DIM, (h + 1) * HEAD_DIM)
        fcol = fall[:, L_MLF + h:L_MLF + h + 1]
        icol = pre[:, L_MLI + h:L_MLI + h + 1]
        dm = jnp.where(tril, fcol + imf_t[h:h + 1, :], NEG_INF)
        dmax = jnp.max(dm, axis=-1, keepdims=True)
        mp = m_s[h:h + 1, 0:1]
        m_prev, m_int, m_cur = [], [], []
        for cc in range(nc):
            sl = slice(cc * CHUNK, (cc + 1) * CHUNK)
            mi = mp + fcol[sl]
            mt = jnp.maximum(mi, dmax[sl])
            m_prev.append(mp)
            m_int.append(mi)
            m_cur.append(mt)
            mp = mt[CHUNK - 1:CHUNK, :]
        m_t = jnp.concatenate(m_cur, axis=0)
        w = jnp.exp(dm - m_t)
        qh = q_ref[:, hs]
        kh = k_ref[:, hs] * HEAD_DIM ** -0.5
        vb = v_ref[:, hs].astype(bf16)
        s = _dot_nt(qh, kh) * w
        num_in = _dot(s, vb)
        den_in = jnp.sum(s, axis=-1, keepdims=True)
        cst = c_s[h]
        nst = n_s[h:h + 1, :]
        outs = []
        for cc in range(nc):
            sl = slice(cc * CHUNK, (cc + 1) * CHUNK)
            mi, mt = m_int[cc], m_cur[cc]
            gi = jnp.exp(mi - mt)
            qc = qh[sl]
            num = num_in[sl] + gi * _dot(qc, cst)
            den = den_in[sl] + gi * jnp.sum(qc * nst, axis=-1, keepdims=True)
            outs.append(num / jnp.maximum(jnp.abs(den), jnp.exp(-mt)))
            m_new = mt[CHUNK - 1:CHUNK, :]
            fl = fcol[sl][CHUNK - 1:CHUNK, :]
            wk = jnp.exp(fl - fcol[sl] + icol[sl] - m_new)
            dec = jnp.exp(m_prev[cc] + fl - m_new)
            kw = kh[sl] * wk
            cst = dec * cst + _dot_tn(kw, vb[sl])
            nst = dec * nst + jnp.sum(kw, axis=0, keepdims=True)
        c_s[h] = cst
        n_s[h:h + 1, :] = nst
        m_s[h:h + 1, :] = jnp.broadcast_to(mp, (1, LANES))
        hh = jnp.concatenate(outs, axis=0)
        y = _rms_rows(hh) * nw_ref[:, hs] * jax.nn.sigmoid(og_ref[:, hs])
        y_ref[:, y_off + h * HEAD_DIM:y_off + (h + 1) * HEAD_DIM] = y.astype(y_ref.dtype)

    @pl.when(t == pl.num_programs(1) - 1)
    def _():
        c_out[0] = c_s[...]
        n_out[0] = n_s[0:HEADS, :]
        m_out[0] = m_s[...]


def _unit_lower_inverse_minus_eye(a, r, c):
    tb = a.shape[0]
    base = SUBLANES
    nb = tb // base
    col_in_blk = c - (r // base) * base
    x = jnp.where(r == c, 1.0, 0.0)
    for j in range(base - 1):
        coef = jnp.sum(jnp.where(col_in_blk == j, a, 0.0), axis=-1, keepdims=True)
        xj = jnp.broadcast_to(x.reshape(nb, base, tb)[:, j:j + 1, :], (nb, base, tb)).reshape(tb, tb)
        x = x - coef * xj
    s = base
    while s < CHUNK:
        couple = jnp.logical_and((r // (2 * s)) == (c // (2 * s)), (r // s) != (c // s))
        x = x - _dot(x, _dot(jnp.where(couple, a, 0.0), x))
        s *= 2
    return jnp.where(r == c, 0.0, x)


def _gdn_prompt_body(q_ref, k_ref, v_ref, z_ref, sm_ref, gb_ref, al_ref, wc_ref, cs_ref, nw_ref,
                     y_ref, s_out, conv_out, s_s, xb_s, vn_s, *, tb, y_off):
    t = pl.program_id(1)
    nc = tb // CHUNK
    gw = GROUP_WIDTH
    pad = SUBLANES
    hist = CONV_W - 1

    @pl.when(t == 0)
    def _():
        s_s[...] = jnp.zeros_like(s_s)
        for p in range(3):
            xb_s[p, 0:pad, :] = jnp.zeros((pad, gw), f32)
            xb_s[p, pad - hist:pad, :] = cs_ref[0, :, p * gw:(p + 1) * gw]

    @pl.when(t > 0)
    def _():
        for p in range(3):
            xb_s[p, 0:pad, :] = xb_s[p, tb:tb + pad, :]

    xb_s[0, pad:, :] = q_ref[...]
    xb_s[1, pad:, :] = k_ref[...]
    xb_s[2, pad:, :] = v_ref[...]

    conv = []
    for p in range(3):
        acc = xb_s[p, pad - hist:pad - hist + tb, :] * wc_ref[0:1, p * gw:(p + 1) * gw]
        for j in range(1, CONV_W):
            acc = acc + xb_s[p, pad - hist + j:pad - hist + j + tb, :] * wc_ref[j:j + 1, p * gw:(p + 1) * gw]
        conv.append(_silu(acc))

    r, c, _, tril, strict = _chunk_masks(tb)
    ltri = tril.astype(f32)
    pre = sm_ref[...] + gb_ref[...]
    beta_all = jax.nn.sigmoid(pre)
    g_all = -jnp.exp(al_ref[...]) * jax.nn.softplus(pre)
    gall = _dot_exact(ltri, g_all)
    gall_t = gall.T
    for h in range(HEADS):
        hs = slice(h * HEAD_DIM, (h + 1) * HEAD_DIM)
        gcol = gall[:, L_GDA + h:L_GDA + h + 1]
        grow = gall_t[L_GDA + h:L_GDA + h + 1, :]
        bcol = beta_all[:, L_GDB + h:L_GDB + h + 1]
        decay = jnp.where(tril, jnp.exp(jnp.where(tril, gcol - grow, 0.0)), 0.0)
        qh = _l2n_rows(conv[0][:, hs]) * HEAD_DIM ** -0.5
        kh = _l2n_rows(conv[1][:, hs])
        vh = conv[2][:, hs]
        kb = kh * bcol
        egc = jnp.exp(gcol)
        a = jnp.where(strict, _dot_nt(kb, kh) * decay, 0.0)
        rhs = jnp.concatenate([vh * bcol, kb * egc], axis=1)
        x = rhs + _dot(_unit_lower_inverse_minus_eye(a, r, c), rhs)
        u = x[:, :HEAD_DIM]
        w = x[:, HEAD_DIM:]
        attn = (_dot_nt(qh, kh) * decay).astype(bf16)
        qe = qh * egc
        st = s_s[h]
        vn_s[...] = jnp.zeros_like(vn_s)
        outs = []
        for cc in range(nc):
            sl = slice(cc * CHUNK, (cc + 1) * CHUNK)
            vnew = u[sl] - _dot(w[sl], st)
            vn_s[sl, :] = vnew.astype(bf16)
            outs.append(_dot(qe[sl], st) + jnp.dot(attn[sl, :], vn_s[...], preferred_element_type=f32))
            gl = gcol[sl][CHUNK - 1:CHUNK, :]
            kdec = kh[sl] * jnp.exp(gl - gcol[sl])
            st = st * jnp.exp(gl) + _dot_tn(kdec, vnew)
        s_s[h] = st
        o = jnp.concatenate(outs, axis=0)
        y = _rms_rows(o) * nw_ref[...] * _silu(z_ref[:, hs])
        y_ref[:, y_off + h * HEAD_DIM:y_off + (h + 1) * HEAD_DIM] = y.astype(y_ref.dtype)

    @pl.when(t == pl.num_programs(1) - 1)
    def _():
        s_out[0] = s_s[...]
        for p in range(3):
            conv_out[0, :, p * gw:(p + 1) * gw] = xb_s[p, tb + pad - hist:tb + pad, :]


def _gla_prompt_body(q_ref, k_ref, v_ref, z_ref, sm_ref, w2_ref, b_ref, nw_ref,
                     y_ref, s_out, s_s, *, tb, y_off):
    t = pl.program_id(1)
    nc = tb // CHUNK
    dk = GLA_KEY_DIM
    sub = 16
    nsub = CHUNK // sub

    @pl.when(t == 0)
    def _():
        s_s[...] = jnp.zeros_like(s_s)

    r, c, same, tril, _ = _chunk_masks(tb)
    ltri = tril.astype(f32)
    pre = _dot_exact(sm_ref[...], w2_ref[...]) + b_ref[...]
    gk = jax.nn.log_sigmoid(pre) * (1.0 / GLA_GATE_DENOM)
    g = _dot_exact(ltri, gk)
    q = q_ref[...] * dk ** -0.5
    k = k_ref[...]
    rpos = lax.broadcasted_iota(jnp.int32, (tb, 1), 0) % CHUNK
    lane = lax.broadcasted_iota(jnp.int32, (1, LANES), 1)
    lane_lo = lane < dk

    a_off = [jnp.zeros((tb, tb), f32) for _ in range(HEADS)]
    g3 = g.reshape(nc, CHUNK, HEADS * dk)
    for i in range(1, nsub):
        ri = jnp.broadcast_to(g3[:, sub * i:sub * i + 1, :], (nc, CHUNK, HEADS * dk)).reshape(tb, HEADS * dk)
        qs = q * jnp.exp(jnp.where(rpos // sub == i, g - ri, NEG_INF))
        ks = k * jnp.exp(jnp.where(rpos < sub * i, ri - g, NEG_INF))
        for h in range(HEADS):
            ps = slice((h // 2) * LANES, (h // 2 + 1) * LANES)
            keep = lane_lo if h % 2 == 0 else jnp.logical_not(lane_lo)
            a_off[h] = a_off[h] + _dot_nt(jnp.where(keep, qs[:, ps], 0.0), ks[:, ps])

    a_diag = [jnp.zeros((tb, tb), f32) for _ in range(HEADS)]
    for j in range(sub):
        ksh = k if j == 0 else pltpu.roll(k, j, 0)
        gsh = g if j == 0 else pltpu.roll(g, j, 0)
        valid = (rpos % sub) >= j
        prod = q * ksh * jnp.exp(jnp.where(valid, g - gsh, NEG_INF))
        on_diag = c == (r - j)
        for h in range(HEADS):
            a = jnp.sum(prod[:, h * dk:(h + 1) * dk], axis=-1, keepdims=True)
            a_diag[h] = a_diag[h] + jnp.where(on_diag, a, 0.0)

    eg = jnp.exp(g)
    qe = q * eg
    g_t = g.T
    for p in range(HEADS // 2):
        ps = slice(p * LANES, (p + 1) * LANES)
        st = s_s[p]
        o_in = []
        vbs = []
        for hh in range(2):
            h = 2 * p + hh
            hs = slice(h * HEAD_DIM, (h + 1) * HEAD_DIM)
            vbs.append(v_ref[:, hs].astype(bf16))
            a = a_diag[h] + jnp.where(same, a_off[h], 0.0)
            o_in.append(_dot(a, vbs[hh]))
        outs = [[], []]
        for cc in range(nc):
            sl = slice(cc * CHUNK, (cc + 1) * CHUNK)
            last = cc * CHUNK + CHUNK - 1
            gl_row = g[last:last + 1, ps]
            gl_col = g_t[ps, last:last + 1]
            kdec = k[sl, ps] * jnp.exp(gl_row - g[sl, ps])
            upd = jnp.zeros((LANES, HEAD_DIM), f32)
            for hh in range(2):
                keep = lane_lo if hh == 0 else jnp.logical_not(lane_lo)
                outs[hh].append(o_in[hh][sl] + _dot(jnp.where(keep, qe[sl, ps], 0.0), st))
                upd = upd + _dot_tn(jnp.where(keep, kdec, 0.0), vbs[hh][sl])
            st = st * jnp.exp(gl_col) + upd
        s_s[p] = st
        for hh in range(2):
            h = 2 * p + hh
            hs = slice(h * HEAD_DIM, (h + 1) * HEAD_DIM)
            o = jnp.concatenate(outs[hh], axis=0)
            y = _rms_rows(o) * nw_ref[...] * _silu(z_ref[:, hs])
            y_ref[:, y_off + h * HEAD_DIM:y_off + (h + 1) * HEAD_DIM] = y.astype(y_ref.dtype)

    @pl.when(t == pl.num_programs(1) - 1)
    def _():
        s_out[0] = s_s[...]


def _prompt_mixer_kernel(rq, rk, rv, rg, mq, mk, mv, mo, dq, dk, dv, dz, lq, lk, lv, lz, sm_ref,
                         cos_ref, sin_ref, gb_ref, al_ref, wc_ref, cs_ref, mlnw_ref, gdnw_ref, glnw_ref,
                         w2_ref, glb_ref,
                         y_ref, ret_o, mlc_o, mln_o, mlm_o, gdn_o, conv_o, gla_o,
                         ret_s, mlc_s, mln_s, mlm_s, gdn_s, xb_s, vn_s, gla_s, *, tb):
    gw = GROUP_WIDTH
    _ret_prompt_body(rq, rk, rv, rg, cos_ref, sin_ref, y_ref, ret_o, ret_s, tb=tb, y_off=0)
    _mlstm_prompt_body(mq, mk, mv, mo, sm_ref, gb_ref, mlnw_ref, y_ref, mlc_o, mln_o, mlm_o,
                       mlc_s, mln_s, mlm_s, tb=tb, y_off=gw)
    _gdn_prompt_body(dq, dk, dv, dz, sm_ref, gb_ref, al_ref, wc_ref, cs_ref, gdnw_ref, y_ref, gdn_o, conv_o,
                     gdn_s, xb_s, vn_s, tb=tb, y_off=2 * gw)
    _gla_prompt_body(lq, lk, lv, lz, sm_ref, w2_ref, glb_ref, glnw_ref, y_ref, gla_o, gla_s,
                     tb=tb, y_off=3 * gw)


def _prompt_mixers(c_all, lp, *, batch, seq, conv_zero):
    tb = MIX_TB
    nt = seq // tb
    gw = GROUP_WIDTH
    kw = HEADS * GLA_KEY_DIM

    def cspec(name, width):
        blk = C_OFF[name] // width
        assert C_OFF[name] % width == 0
        return pl.BlockSpec((tb, width), lambda b, t: (b * nt + t, blk))

    def const2(shape):
        return pl.BlockSpec(shape, lambda b, t: (0, 0))

    def per_batch(shape):
        return pl.BlockSpec((1,) + shape, lambda b, t: (b,) + (0,) * len(shape))

    names = [("ret_q", gw), ("ret_k", gw), ("ret_v", gw), ("ret_g", gw),
             ("ml_q", gw), ("ml_k", gw), ("ml_v", gw), ("ml_o", gw),
             ("gd_q", gw), ("gd_k", gw), ("gd_v", gw), ("gd_z", gw),
             ("gl_q", kw), ("gl_k", kw), ("gl_v", gw), ("gl_z", gw), ("small", LANES)]
    in_specs = [cspec(n, w) for n, w in names] + [
        pl.BlockSpec((tb, HEAD_DIM), lambda b, t: (t, 0)), pl.BlockSpec((tb, HEAD_DIM), lambda b, t: (t, 0)),
        const2((1, LANES)), const2((1, LANES)), const2((CONV_W, 3 * gw)), per_batch((CONV_W - 1, 3 * gw)),
        const2((1, gw)), const2((1, HEAD_DIM)), const2((1, HEAD_DIM)), const2((LANES, kw)), const2((1, kw))]
    state_shape = (HEADS, HEAD_DIM, HEAD_DIM)
    out_shapes = [(c_all.shape[0], D_MODEL), (batch,) + state_shape, (batch,) + state_shape,
                  (batch, HEADS, HEAD_DIM), (batch, SUBLANES, LANES), (batch,) + state_shape,
                  (batch, CONV_W - 1, 3 * gw), (batch, HEADS // 2, LANES, HEAD_DIM)]
    out_specs = [pl.BlockSpec((tb, D_MODEL), lambda b, t: (b * nt + t, 0))] + [
        per_batch(s[1:]) for s in out_shapes[1:]]
    outs = pl.pallas_call(
        functools.partial(_prompt_mixer_kernel, tb=tb),
        grid=(batch, nt),
        in_specs=in_specs,
        out_specs=out_specs,
        out_shape=[jax.ShapeDtypeStruct(out_shapes[0], bf16)] + [jax.ShapeDtypeStruct(s, f32) for s in out_shapes[1:]],
        scratch_shapes=[pltpu.VMEM(state_shape, f32), pltpu.VMEM(state_shape, f32),
                        pltpu.VMEM((SUBLANES, HEAD_DIM), f32), pltpu.VMEM((SUBLANES, LANES), f32),
                        pltpu.VMEM(state_shape, f32), pltpu.VMEM((3, tb + SUBLANES, gw), f32),
                        pltpu.VMEM((tb, HEAD_DIM), bf16), pltpu.VMEM((HEADS // 2, LANES, HEAD_DIM), f32)],
        compiler_params=_cparams(("parallel", "arbitrary")),
        name="prompt_mixers",
    )(*([c_all] * len(names)), lp["cos_p"], lp["sin_p"], lp["gate_bias"], lp["alog_row"], lp["gd_conv"],
      conv_zero, lp["ml_norm"], lp["gd_norm"], lp["gl_norm"], lp["gl_w2p"], lp["gl_b"])
    y, s_ret, s_c, s_n, s_m, s_gdn, s_conv, s_gla = outs
    states = {"ret": s_ret, "ml_C": s_c, "ml_n": s_n, "ml_m": s_m[:, :HEADS, 0],
              "gdn": s_gdn, "gdn_conv": s_conv,
              "gla": s_gla.reshape(batch, HEADS, GLA_KEY_DIM, HEAD_DIM)}
    return y, states


def _cols(x):
    g = x.shape[0]
    return jnp.concatenate([x, jnp.zeros((LANES - g, LANES), f32)], axis=0).T


def _sample_mixer_kernel(c_ref, cos_ref, sin_ref, gb_ref, al_ref, wc_ref, mlnw_ref, gdnw_ref, glnw_ref,
                         w2_ref, glb_ref,
                         ret_ref, mlc_ref, mln_ref, mlm_ref, gdn_ref, conv_ref, gla_ref, *rest):
    y_ref, ret_o, mlc_o, mln_o, mlm_o, gdn_o, conv_o, gla_o = rest[-8:]
    G = SAMPLE_G
    gw = GROUP_WIDTH
    dk = GLA_KEY_DIM

    def cblk(name, width):
        return c_ref[:, C_OFF[name]:C_OFF[name] + width]

    small = cblk("small", LANES)
    pre = small + gb_ref[...]
    cosf = cos_ref[...]
    sinf = sin_ref[...]

    for h in range(HEADS):
        hs = slice(h * HEAD_DIM, (h + 1) * HEAD_DIM)
        gamma = 1.0 - 2.0 ** (-5.0 - h)
        q = _rope(cblk("ret_q", gw)[:, hs], cosf, sinf)
        k = _rope(cblk("ret_k", gw)[:, hs], cosf, sinf) * HEAD_DIM ** -0.5
        v = cblk("ret_v", gw)[:, hs]
        qt, kt = _cols(q), _cols(k)
        rows = []
        for j in range(G):
            s_new = gamma * ret_ref[j, h] + kt[:, j:j + 1] * v[j:j + 1, :]
            ret_o[j, h] = s_new
            rows.append(jnp.sum(qt[:, j:j + 1] * s_new, axis=0, keepdims=True))
        o = jnp.concatenate(rows, axis=0)
        y = _rms_rows(o) * _silu(cblk("ret_g", gw)[:, hs])
        y_ref[:, 0 * gw + h * HEAD_DIM:0 * gw + (h + 1) * HEAD_DIM] = y.astype(y_ref.dtype)

    m_old = mlm_ref[...]
    f_al = pltpu.roll(jax.nn.log_sigmoid(pre), LANES - (L_MLF - L_MLI), 1)
    m_int = m_old + f_al
    m_t = jnp.maximum(m_int, pre)
    w_in = jnp.exp(pre - m_t)
    g_in = jnp.exp(m_int - m_t)
    e_neg = jnp.exp(-m_t)
    mlm_o[...] = m_t
    for h in range(HEADS):
        hs = slice(h * HEAD_DIM, (h + 1) * HEAD_DIM)
        q = cblk("ml_q", gw)[:, hs]
        k = cblk("ml_k", gw)[:, hs] * HEAD_DIM ** -0.5
        v = cblk("ml_v", gw)[:, hs]
        qt, kt = _cols(q), _cols(k)
        rows = []
        for j in range(G):
            wj = w_in[j:j + 1, h:h + 1]
            gj = g_in[j:j + 1, h:h + 1]
            c_new = gj * mlc_ref[j, h] + (kt[:, j:j + 1] * wj) * v[j:j + 1, :]
            mlc_o[j, h] = c_new
            n_new = gj * mln_ref[j, h:h + 1, :] + wj * k[j:j + 1, :]
            mln_o[j, h:h + 1, :] = n_new
            num = jnp.sum(qt[:, j:j + 1] * c_new, axis=0, keepdims=True)
            den = jnp.sum(q[j:j + 1, :] * n_new, axis=-1, keepdims=True)
            rows.append(num / jnp.maximum(jnp.abs(den), e_neg[j:j + 1, h:h + 1]))
        hh = jnp.concatenate(rows, axis=0)
        y = _rms_rows(hh) * mlnw_ref[:, hs] * jax.nn.sigmoid(cblk("ml_o", gw)[:, hs])
        y_ref[:, 1 * gw + h * HEAD_DIM:1 * gw + (h + 1) * HEAD_DIM] = y.astype(y_ref.dtype)

    beta_all = jax.nn.sigmoid(pre)
    eg_all = jnp.exp(-jnp.exp(al_ref[...]) * jax.nn.softplus(pre))
    conv = []
    for p, name in enumerate(("gd_q", "gd_k", "gd_v")):
        ps = slice(p * gw, (p + 1) * gw)
        x = cblk(name, gw)
        acc = x * wc_ref[CONV_W - 1:CONV_W, ps]
        for j in range(CONV_W - 1):
            acc = acc + conv_ref[j, :, ps] * wc_ref[j:j + 1, ps]
        conv.append(_silu(acc))
        for j in range(CONV_W - 2):
            conv_o[j, :, ps] = conv_ref[j + 1, :, ps]
        conv_o[CONV_W - 2, :, ps] = x
    for h in range(HEADS):
        hs = slice(h * HEAD_DIM, (h + 1) * HEAD_DIM)
        q = _l2n_rows(conv[0][:, hs]) * HEAD_DIM ** -0.5
        k = _l2n_rows(conv[1][:, hs])
        v = conv[2][:, hs]
        qt, kt = _cols(q), _cols(k)
        rows = []
        for j in range(G):
            bj = beta_all[j:j + 1, L_GDB + h:L_GDB + h + 1]
            ej = eg_all[j:j + 1, L_GDA + h:L_GDA + h + 1]
            s_old = gdn_ref[j, h]
            ks = jnp.sum(kt[:, j:j + 1] * s_old, axis=0, keepdims=True)
            v_new = bj * (v[j:j + 1, :] - ej * ks)
            s_new = ej * s_old + kt[:, j:j + 1] * v_new
            gdn_o[j, h] = s_new
            rows.append(jnp.sum(qt[:, j:j + 1] * s_new, axis=0, keepdims=True))
        o = jnp.concatenate(rows, axis=0)
        y = _rms_rows(o) * gdnw_ref[...] * _silu(cblk("gd_z", gw)[:, hs])
        y_ref[:, 2 * gw + h * HEAD_DIM:2 * gw + (h + 1) * HEAD_DIM] = y.astype(y_ref.dtype)

    gk = jax.nn.log_sigmoid(_dot_exact(small, w2_ref[...]) + glb_ref[...]) * (1.0 / GLA_GATE_DENOM)
    eg = jnp.exp(gk)
    q_all = cblk("gl_q", HEADS * dk) * dk ** -0.5
    k_all = cblk("gl_k", HEADS * dk)
    upper = lax.broadcasted_iota(jnp.int32, (LANES, 1), 0) < dk
    for p in range(HEADS // 2):
        ps = slice(p * LANES, (p + 1) * LANES)
        qt, kt, et = _cols(q_all[:, ps]), _cols(k_all[:, ps]), _cols(eg[:, ps])
        v0 = cblk("gl_v", gw)[:, (2 * p) * HEAD_DIM:(2 * p + 1) * HEAD_DIM]
        v1 = cblk("gl_v", gw)[:, (2 * p + 1) * HEAD_DIM:(2 * p + 2) * HEAD_DIM]
        rows0, rows1 = [], []
        for j in range(G):
            v2 = jnp.where(upper, v0[j:j + 1, :], v1[j:j + 1, :])
            s_new = et[:, j:j + 1] * gla_ref[j, p] + kt[:, j:j + 1] * v2
            gla_o[j, p] = s_new
            qs = qt[:, j:j + 1] * s_new
            rows0.append(jnp.sum(qs[:dk], axis=0, keepdims=True))
            rows1.append(jnp.sum(qs[dk:], axis=0, keepdims=True))
        for hh, rows in enumerate((rows0, rows1)):
            h = 2 * p + hh
            hs = slice(h * HEAD_DIM, (h + 1) * HEAD_DIM)
            o = jnp.concatenate(rows, axis=0)
            y = _rms_rows(o) * glnw_ref[...] * _silu(cblk("gl_z", gw)[:, hs])
            y_ref[:, 3 * gw + h * HEAD_DIM:3 * gw + (h + 1) * HEAD_DIM] = y.astype(y_ref.dtype)


def _sample_mixers(c_all, lp, st, y_prev, st_prev, *, layer, depth, row0, nb):
    G = SAMPLE_G
    gw = GROUP_WIDTH
    kw = HEADS * GLA_KEY_DIM
    assert row0 % G == 0 and nb % G == 0
    blk0 = row0 // G

    def const2(shape):
        return pl.BlockSpec(shape, lambda i: (0, 0))

    def slab(shape, batch_axis=0):
        def index(i):
            idx = [0] * len(shape)
            idx[batch_axis] = i
            return (layer,) + tuple(idx)
        return pl.BlockSpec((None,) + shape, index)

    state_specs = [slab((G, HEADS, HEAD_DIM, HEAD_DIM)), slab((G, HEADS, HEAD_DIM, HEAD_DIM)),
                   slab((G, HEADS, HEAD_DIM)), slab((G, LANES)), slab((G, HEADS, HEAD_DIM, HEAD_DIM)),
                   slab((CONV_W - 1, G, 3 * gw), batch_axis=1), slab((G, HEADS // 2, LANES, HEAD_DIM))]
    order = ("ret", "ml_C", "ml_n", "ml_m", "gdn", "gdn_conv", "gla")
    states = [st[n] for n in order]
    y_spec = pl.BlockSpec((G, D_MODEL), lambda i: (blk0 + i, 0))

    args = [c_all, lp["cos_s"], lp["sin_s"], lp["gate_bias"], lp["alog_row"], lp["gd_conv"], lp["ml_norm"],
            lp["gd_norm"], lp["gl_norm"], lp["gl_w2p"], lp["gl_b"]] + states
    in_specs = [pl.BlockSpec((G, C_WIDTH), lambda i: (blk0 + i, 0)),
                const2((1, HEAD_DIM)), const2((1, HEAD_DIM)), const2((1, LANES)), const2((1, LANES)),
                const2((CONV_W, 3 * gw)), const2((1, gw)), const2((1, HEAD_DIM)), const2((1, HEAD_DIM)),
                const2((LANES, kw)), const2((1, kw))] + state_specs
    inplace = [y_prev] + ([st_prev[n] for n in order] if st_prev is not None else [])
    aliases = {len(args) + k: k for k in range(len(inplace))}
    args += inplace
    in_specs += [pl.BlockSpec(memory_space=pl.ANY)] * len(inplace)

    outs = pl.pallas_call(
        _sample_mixer_kernel,
        grid=(nb // G,),
        in_specs=in_specs,
        out_specs=[y_spec] + state_specs,
        out_shape=[jax.ShapeDtypeStruct(y_prev.shape, y_prev.dtype)] + [
            jax.ShapeDtypeStruct(s.shape, s.dtype) for s in states],
        input_output_aliases=aliases,
        compiler_params=_cparams(("parallel",)),
        name="sample_mixers",
    )(*args)
    return outs[0], dict(zip(order, outs[1:]))


def _rope_tables(pos):
    half = HEAD_DIM // 2
    inv = ROPE_BASE ** (-jnp.arange(half, dtype=f32) / half)
    ang = pos[:, None] * inv[None, :]
    cos, sin = jnp.cos(ang), jnp.sin(ang)
    return jnp.concatenate([cos, cos], axis=-1), jnp.concatenate([-sin, sin], axis=-1)


def _permute_w_in(w):
    d = w.shape[0]
    big = jnp.concatenate([w[:, 0:4096], w[:, 4104:6152], w[:, 6160:7696]], axis=1)
    small = jnp.concatenate([w[:, 4096:4104], w[:, 6152:6160], w[:, 7696:7712]], axis=1)
    pad = jnp.zeros((d, C_WIDTH - big.shape[1] - small.shape[1]), w.dtype)
    return jnp.concatenate([big, small, pad], axis=1)


def _lane_row(pieces):
    row = jnp.zeros((1, LANES), f32)
    for off, val in pieces:
        row = lax.dynamic_update_slice(row, val.reshape(1, -1).astype(f32), (0, off))
    return row


def kernel(x_prompt, x_sample, state_ret, state_mlstm_C, state_mlstm_n, state_mlstm_m, state_gdn, state_gdn_conv, state_gla, cache_mem_k, cache_mem_v, mem_prompt, norm_mix_pre, norm_mix_post, w_in, ml_ib, ml_fb, ml_norm, gd_conv, gd_A_log, gd_dt_bias, gd_norm, gl_w2, gl_b, gl_norm, w_out, norm_x_pre, norm_x_post, norm_mem, w_xq, w_xk, w_xv, w_xo, norm_mlp_pre, norm_mlp_post, w_up, w_down):
    bp, tp, d = x_prompt.shape
    bs, ts, _ = x_sample.shape
    depth = w_in.shape[0]
    assert ts == 1 and d == D_MODEL and tp % MIX_TB == 0
    rows_p = bp * tp
    rows = rows_p + bs
    assert rows % ROW_TILE == 0 and rows_p % SAMPLE_G == 0

    x = jnp.concatenate([x_prompt.reshape(rows_p, d), x_sample.reshape(bs, d)], axis=0)
    cos_p, sin_p = _rope_tables(jnp.arange(tp, dtype=f32))
    cos_s, sin_s = _rope_tables(jnp.arange(ts, dtype=f32) + PAST_LEN)
    conv_zero = jnp.zeros((bp, CONV_W - 1, 3 * GROUP_WIDTH), f32)
    mem2d = mem_prompt.reshape(bp * MEM_LEN, d)
    st_in = {"ret": state_ret, "ml_C": state_mlstm_C, "ml_n": state_mlstm_n,
             "ml_m": jnp.pad(state_mlstm_m, ((0, 0), (0, 0), (0, LANES - HEADS))),
             "gdn": state_gdn, "gdn_conv": jnp.transpose(state_gdn_conv, (0, 2, 1, 3)),
             "gla": state_gla.reshape(depth, bs, HEADS // 2, LANES, HEAD_DIM)}

    new_p = {n: [] for n in ("ret", "ml_C", "ml_n", "ml_m", "gdn", "gdn_conv", "gla")}
    st_s = None
    mk_p = mv_p = None
    for l in range(depth):
        lp = {
            "cos_p": cos_p, "sin_p": sin_p, "cos_s": cos_s, "sin_s": sin_s,
            "gate_bias": _lane_row([(L_MLI, ml_ib[l]), (L_MLF, ml_fb[l]), (L_GDA, gd_dt_bias[l])]),
            "alog_row": _lane_row([(L_GDA, gd_A_log[l])]),
            "ml_norm": ml_norm[l].reshape(1, GROUP_WIDTH),
            "gd_norm": gd_norm[l].reshape(1, HEAD_DIM),
            "gl_norm": gl_norm[l].reshape(1, HEAD_DIM),
            "gd_conv": gd_conv[l],
            "gl_w2p": jnp.zeros((LANES, HEADS * GLA_KEY_DIM), f32).at[L_GLR:L_GLR + GLA_RANK].set(gl_w2[l]),
            "gl_b": gl_b[l].reshape(1, HEADS * GLA_KEY_DIM),
        }
        w_in_b = _permute_w_in(w_in[l]).astype(bf16)

        mk_p = _norm_matmul(mem2d, norm_mem[l], w_xk[l].astype(bf16), tm=512, tn=1024, out_dtype=f32,
                            name="mem_k", layer_slab=(l, depth, mk_p))
        mv_p = _norm_matmul(mem2d, norm_mem[l], w_xv[l].astype(bf16), tm=512, tn=1024, out_dtype=f32,
                            name="mem_v", layer_slab=(l, depth, mv_p))

        c_all = _norm_matmul(x, norm_mix_pre[l], w_in_b, tm=ROW_TILE, tn=1024, out_dtype=f32, name="w_in")
        y_all, st_p = _prompt_mixers(c_all, lp, batch=bp, seq=tp, conv_zero=conv_zero)
        y_all, st_s = _sample_mixers(c_all, lp, st_in, y_all, st_s, layer=l, depth=depth, row0=rows_p, nb=bs)
        x = _matmul_norm_res(y_all, w_out[l].astype(bf16), norm_mix_post[l], x, tm=ROW_TILE, tk=D_MODEL,
                             name="w_out")

        q_all = _norm_matmul(x, norm_x_pre[l], w_xq[l].astype(bf16), tm=ROW_TILE, tn=1024, out_dtype=bf16,
                             name="w_xq")
        o_all = _xattn_prompt(q_all, mk_p, mv_p, layer=l, batch=bp, seq=tp, tq=512)
        o_all = _xattn_sample(q_all, cache_mem_k, cache_mem_v, o_all, layer=l, row0=rows_p, nb=bs)
        x = _matmul_norm_res(o_all, w_xo[l].astype(bf16), norm_x_post[l], x, tm=ROW_TILE, tk=D_MODEL,
                             name="w_xo")

        u = _norm_matmul(x, norm_mlp_pre[l], w_up[l].astype(bf16), tm=ROW_TILE, tn=1024, out_dtype=bf16,
                         act="relu2", name="w_up")
        x = _matmul_norm_res(u, w_down[l].astype(bf16), norm_mlp_post[l], x, tm=ROW_TILE, tk=1024,
                             name="w_down")

        for n in new_p:
            new_p[n].append(st_p[n])

    def stk(lst):
        return jnp.stack(lst, axis=0)

    xp = x[:rows_p].reshape(bp, tp, d)
    xs = x[rows_p:].reshape(bs, ts, d)
    return (xp, xs,
            stk(new_p["ret"]), stk(new_p["ml_C"]), stk(new_p["ml_n"]), stk(new_p["ml_m"]),
            stk(new_p["gdn"]), stk(new_p["gdn_conv"]), stk(new_p["gla"]),
            mk_p.reshape(depth, bp, MEM_LEN, d), mv_p.reshape(depth, bp, MEM_LEN, d),
            st_s["ret"], st_s["ml_C"], st_s["ml_n"], st_s["ml_m"][:, :, :HEADS],
            st_s["gdn"], jnp.transpose(st_s["gdn_conv"], (0, 2, 1, 3)),
            st_s["gla"].reshape(depth, bs, HEADS, GLA_KEY_DIM, HEAD_DIM))
```

```python
import functools
import math

import jax
import jax.numpy as jnp
from jax import lax
from jax.experimental import pallas as pl
from jax.experimental.pallas import tpu as pltpu

f32 = jnp.float32
bf16 = jnp.bfloat16

D_MODEL = 2048
HEADS = 4
HEAD_DIM = 128
GROUP_WIDTH = 512
GLA_KEY_DIM = 64
GLA_RANK = 16
GLA_GATE_DENOM = 16.0
CONV_W = 4
CHUNK = 64
ROPE_BASE = 10000.0
X_HEADS = 4
X_HEAD_DIM = 512
MEM_LEN = 256
D_FF = 8192
EPS = 1e-6
PAST_LEN = 16384
NEG_INF = float("-inf")

LANES = 128
SUBLANES = 8

C_OFF = {
    "ret_q": 0, "ret_k": 512, "ret_v": 1024, "ret_g": 1536,
    "ml_q": 2048, "ml_k": 2560, "ml_v": 3072, "ml_o": 3584,
    "gd_q": 4096, "gd_k": 4608, "gd_v": 5120, "gd_z": 5632,
    "gl_q": 6144, "gl_k": 6400, "gl_v": 6656, "gl_z": 7168,
    "small": 7680,
}
C_WIDTH = 8192
L_MLI, L_MLF, L_GDB, L_GDA, L_GLR = 0, 4, 8, 12, 16

ROW_TILE = 640
MIX_TB = 256
SAMPLE_G = 8
GLA_SUB = 16
VMEM_LIMIT = 56 * 1024 * 1024


def _cparams(sem):
    return pltpu.CompilerParams(dimension_semantics=sem, vmem_limit_bytes=VMEM_LIMIT)


def _dot(a, b):
    return jnp.dot(a.astype(bf16), b.astype(bf16), preferred_element_type=f32)


def _dot_nt(a, b):
    return lax.dot_general(a.astype(bf16), b.astype(bf16), (((1,), (1,)), ((), ())),
                           preferred_element_type=f32)


def _dot_tn(a, b):
    return lax.dot_general(a.astype(bf16), b.astype(bf16), (((0,), (0,)), ((), ())),
                           preferred_element_type=f32)


def _dot_exact(a, b):
    return jnp.dot(a, b, preferred_element_type=f32, precision=lax.Precision.HIGHEST)


def _rms_rows(x):
    return x * lax.rsqrt(jnp.mean(x * x, axis=-1, keepdims=True) + EPS)


def _l2n_rows(x):
    return x * lax.rsqrt(jnp.sum(x * x, axis=-1, keepdims=True) + EPS)


def _silu(x):
    return x * jax.nn.sigmoid(x)


def _rope(x, cosf, sinf):
    return x * cosf + pltpu.roll(x, HEAD_DIM // 2, 1) * sinf


def _chunk_masks(tb):
    r = lax.broadcasted_iota(jnp.int32, (tb, tb), 0)
    c = lax.broadcasted_iota(jnp.int32, (tb, tb), 1)
    same = (r // CHUNK) == (c // CHUNK)
    tril = jnp.logical_and(same, r >= c)
    strict = jnp.logical_and(same, r > c)
    return r, c, same, tril, strict


def _rms_gain(x, g):
    return x * lax.rsqrt(jnp.mean(x * x, axis=-1, keepdims=True) + EPS) * g


def _embed_kernel(xp_ref, xs_ref, g_ref, x_ref, h_ref, *, n_prompt_blocks):
    i = pl.program_id(0)

    @pl.when(i < n_prompt_blocks)
    def _():
        x_ref[...] = xp_ref[...]

    @pl.when(i >= n_prompt_blocks)
    def _():
        x_ref[...] = xs_ref[...]

    h_ref[...] = _rms_gain(x_ref[...], g_ref[...]).astype(bf16)


def _embed(xp2d, xs2d, g, *, tm):
    mp, d = xp2d.shape
    ms = xs2d.shape[0]
    assert mp % tm == 0 and ms % tm == 0
    npb, nsb = mp // tm, ms // tm
    return pl.pallas_call(
        functools.partial(_embed_kernel, n_prompt_blocks=npb),
        grid=(npb + nsb,),
        in_specs=[pl.BlockSpec((tm, d), lambda i: (jnp.minimum(i, npb - 1), 0)),
                  pl.BlockSpec((tm, d), lambda i: (jnp.maximum(i - npb, 0), 0)),
                  pl.BlockSpec((1, d), lambda i: (0, 0))],
        out_specs=[pl.BlockSpec((tm, d), lambda i: (i, 0)), pl.BlockSpec((tm, d), lambda i: (i, 0))],
        out_shape=[jax.ShapeDtypeStruct((mp + ms, d), f32), jax.ShapeDtypeStruct((mp + ms, d), bf16)],
        compiler_params=_cparams(("parallel",)),
        name="embed",
    )(xp2d, xs2d, g.reshape(1, d))


def _matmul_kernel(h_ref, w_ref, *rest, act, cast_w):
    if cast_w:
        o_ref, wb_ref = rest

        @pl.when(pl.program_id(1) == 0)
        def _():
            wb_ref[...] = w_ref[...].astype(bf16)

        w = wb_ref[...]
    else:
        (o_ref,) = rest
        w = w_ref[...]
    y = jnp.dot(h_ref[...], w, preferred_element_type=f32)
    if act == "relu2":
        y = jnp.square(jnp.maximum(y, 0.0))
    o_ref[...] = y.astype(o_ref.dtype)


def _matmul(h, w, *, layer, tm, tn, out_dtype, name, act=None):
    m, k = h.shape
    n = w.shape[2]
    assert m % tm == 0 and n % tn == 0
    cast_w = w.dtype != bf16
    return pl.pallas_call(
        functools.partial(_matmul_kernel, act=act, cast_w=cast_w),
        grid=(n // tn, m // tm),
        in_specs=[pl.BlockSpec((tm, k), lambda j, i: (i, 0)),
                  pl.BlockSpec((None, k, tn), lambda j, i: (layer, 0, j))],
        out_specs=pl.BlockSpec((tm, tn), lambda j, i: (i, j)),
        out_shape=jax.ShapeDtypeStruct((m, n), out_dtype),
        scratch_shapes=[pltpu.VMEM((k, tn), bf16)] if cast_w else [],
        compiler_params=_cparams(("parallel", "arbitrary")),
        name=name,
    )(h, w)


def _norm_matmul_kernel(x_ref, g_ref, w_ref, *rest):
    o_ref, h_ref = rest[-2:]

    @pl.when(pl.program_id(1) == 0)
    def _():
        h_ref[...] = _rms_gain(x_ref[...], g_ref[...]).astype(bf16)

    o_ref[...] = jnp.dot(h_ref[...], w_ref[...].astype(bf16), preferred_element_type=f32)


def _norm_matmul_slab(x, g, w, prev, *, layer, tm, tn, name):
    m, k = x.shape
    depth, _, n = w.shape
    assert m % tm == 0 and n % tn == 0
    in_specs = [pl.BlockSpec((tm, k), lambda i, j: (i, 0)),
                pl.BlockSpec((1, k), lambda i, j: (0, 0)),
                pl.BlockSpec((None, k, tn), lambda i, j: (layer, 0, j))]
    args = [x, g.reshape(1, k), w]
    aliases = {}
    if prev is not None:
        in_specs.append(pl.BlockSpec(memory_space=pl.ANY))
        args.append(prev)
        aliases = {len(args) - 1: 0}
    return pl.pallas_call(
        _norm_matmul_kernel,
        grid=(m // tm, n // tn),
        in_specs=in_specs,
        out_specs=pl.BlockSpec((None, tm, tn), lambda i, j: (layer, i, j)),
        out_shape=jax.ShapeDtypeStruct((depth, m, n), f32),
        scratch_shapes=[pltpu.VMEM((tm, k), bf16)],
        input_output_aliases=aliases,
        compiler_params=_cparams(("parallel", "arbitrary")),
        name=name,
    )(*args)


def _matmul_norm_res_kernel(a_ref, w_ref, g_ref, r_ref, *rest, next_norm):
    if next_norm:
        gn_ref, o_ref, hn_ref, acc_ref = rest
    else:
        o_ref, acc_ref = rest
    kk = pl.program_id(1)

    @pl.when(kk == 0)
    def _():
        acc_ref[...] = jnp.zeros_like(acc_ref)

    acc_ref[...] += jnp.dot(a_ref[...], w_ref[...], preferred_element_type=f32)

    @pl.when(kk == pl.num_programs(1) - 1)
    def _():
        x_new = r_ref[...] + _rms_gain(acc_ref[...], g_ref[...])
        o_ref[...] = x_new
        if next_norm:
            hn_ref[...] = _rms_gain(x_new, gn_ref[...]).astype(bf16)


def _matmul_norm_res(a, w, g, res, g_next, *, layer, tm, tk, name):
    m, k = a.shape
    n = w.shape[2]
    assert m % tm == 0 and k % tk == 0
    next_norm = g_next is not None
    row = pl.BlockSpec((tm, n), lambda i, j: (i, 0))
    vec = pl.BlockSpec((1, n), lambda i, j: (0, 0))
    in_specs = [pl.BlockSpec((tm, tk), lambda i, j: (i, j)),
                pl.BlockSpec((None, tk, n), lambda i, j: (layer, j, 0)), vec, row]
    args = [a, w, g.reshape(1, n), res]
    out_specs = [row]
    out_shape = [jax.ShapeDtypeStruct((m, n), f32)]
    if next_norm:
        in_specs.append(vec)
        args.append(g_next.reshape(1, n))
        out_specs.append(row)
        out_shape.append(jax.ShapeDtypeStruct((m, n), bf16))
    outs = pl.pallas_call(
        functools.partial(_matmul_norm_res_kernel, next_norm=next_norm),
        grid=(m // tm, k // tk),
        in_specs=in_specs,
        out_specs=out_specs,
        out_shape=out_shape,
        scratch_shapes=[pltpu.VMEM((tm, n), f32)],
        compiler_params=_cparams(("parallel", "arbitrary")),
        name=name,
    )(*args)
    return (outs[0], outs[1]) if next_norm else (outs[0], None)


def _xattn_prompt_kernel(q_ref, k_ref, v_ref, o_ref, kb_ref, vb_ref):
    @pl.when(pl.program_id(1) == 0)
    def _():
        kb_ref[...] = k_ref[...].astype(bf16)
        vb_ref[...] = v_ref[...].astype(bf16)

    scale = X_HEAD_DIM ** -0.5
    for h in range(X_HEADS):
        sl = slice(h * X_HEAD_DIM, (h + 1) * X_HEAD_DIM)
        s = lax.dot_general(q_ref[:, sl], kb_ref[:, sl], (((1,), (1,)), ((), ())),
                            preferred_element_type=f32) * scale
        m = jnp.max(s, axis=-1, keepdims=True)
        p = jnp.exp(s - m)
        l = jnp.sum(p, axis=-1, keepdims=True)
        o = jnp.dot(p.astype(bf16), vb_ref[:, sl], preferred_element_type=f32) / l
        o_ref[:, sl] = o.astype(o_ref.dtype)


def _xattn_prompt(q_all, mk, mv, *, layer, batch, seq, tq):
    nt = seq // tq
    return pl.pallas_call(
        _xattn_prompt_kernel,
        grid=(batch, nt),
        in_specs=[pl.BlockSpec((tq, D_MODEL), lambda b, t: (b * nt + t, 0)),
                  pl.BlockSpec((None, MEM_LEN, D_MODEL), lambda b, t: (layer, b, 0)),
                  pl.BlockSpec((None, MEM_LEN, D_MODEL), lambda b, t: (layer, b, 0))],
        out_specs=pl.BlockSpec((tq, D_MODEL), lambda b, t: (b * nt + t, 0)),
        out_shape=jax.ShapeDtypeStruct(q_all.shape, bf16),
        scratch_shapes=[pltpu.VMEM((MEM_LEN, D_MODEL), bf16), pltpu.VMEM((MEM_LEN, D_MODEL), bf16)],
        compiler_params=_cparams(("parallel", "arbitrary")),
        name="xattn_prompt",
    )(q_all, mk, mv)


def _xattn_sample_kernel(q_ref, k_ref, v_ref, prev_ref, o_ref, acc_ref):
    del prev_ref
    j = pl.program_id(1)
    scale = X_HEAD_DIM ** -0.5
    rows = lax.broadcasted_iota(jnp.int32, (SAMPLE_G, 1), 0)
    q = jnp.sum(jnp.where(rows == j, q_ref[...].astype(f32), 0.0), axis=0, keepdims=True)
    for h in range(X_HEADS):
        sl = slice(h * X_HEAD_DIM, (h + 1) * X_HEAD_DIM)
        s = jnp.sum(k_ref[:, sl] * q[:, sl], axis=-1, keepdims=True) * scale
        m = jnp.max(s, axis=0, keepdims=True)
        p = jnp.exp(s - m)
        l = jnp.sum(p, axis=0, keepdims=True)
        acc_ref[pl.ds(j, 1), sl] = jnp.sum(p * v_ref[:, sl], axis=0, keepdims=True) / l

    @pl.when(j == SAMPLE_G - 1)
    def _():
        o_ref[...] = acc_ref[...].astype(o_ref.dtype)


def _xattn_sample(q_all, ck, cv, o_prev, *, layer, row0, nb):
    G = SAMPLE_G
    blk0 = row0 // G
    return pl.pallas_call(
        _xattn_sample_kernel,
        grid=(nb // G, G),
        in_specs=[pl.BlockSpec((G, D_MODEL), lambda g, j: (blk0 + g, 0)),
                  pl.BlockSpec((None, None, MEM_LEN, D_MODEL), lambda g, j: (layer, g * G + j, 0, 0)),
                  pl.BlockSpec((None, None, MEM_LEN, D_MODEL), lambda g, j: (layer, g * G + j, 0, 0)),
                  pl.BlockSpec(memory_space=pl.ANY)],
        out_specs=pl.BlockSpec((G, D_MODEL), lambda g, j: (blk0 + g, 0)),
        out_shape=jax.ShapeDtypeStruct(o_prev.shape, o_prev.dtype),
        scratch_shapes=[pltpu.VMEM((G, D_MODEL), f32)],
        input_output_aliases={3: 0},
        compiler_params=_cparams(("parallel", "arbitrary")),
        name="xattn_sample",
    )(q_all, ck, cv, o_prev)


def _ret_prompt_body(q_ref, k_ref, v_ref, g_ref, cos_ref, sin_ref, y_ref, s_out, s_s, *, tb, y_off):
    t = pl.program_id(1)
    nc = tb // CHUNK

    @pl.when(t == 0)
    def _():
        s_s[...] = jnp.zeros_like(s_s)

    cosf = cos_ref[...]
    sinf = sin_ref[...]
    r, c, _, tril, _ = _chunk_masks(tb)
    diff = (r - c).astype(f32)
    ridx = (lax.broadcasted_iota(jnp.int32, (tb, 1), 0) % CHUNK).astype(f32)
    for h in range(HEADS):
        hs = slice(h * HEAD_DIM, (h + 1) * HEAD_DIM)
        lg = math.log(1.0 - 2.0 ** (-5.0 - h))
        decay = jnp.where(tril, jnp.exp(jnp.maximum(diff, 0.0) * lg), 0.0)
        qr = _rope(q_ref[:, hs], cosf, sinf)
        kr = _rope(k_ref[:, hs], cosf, sinf) * HEAD_DIM ** -0.5
        vb = v_ref[:, hs].astype(bf16)
        s = _dot_nt(qr, kr) * decay
        o_in = _dot(s, vb)
        qd = (qr * jnp.exp((ridx + 1.0) * lg)).astype(bf16)
        kd = (kr * jnp.exp((CHUNK - 1.0 - ridx) * lg)).astype(bf16)
        cdec = math.exp(CHUNK * lg)
        st = s_s[h]
        outs = []
        for cc in range(nc):
            sl = slice(cc * CHUNK, (cc + 1) * CHUNK)
            outs.append(o_in[sl] + _dot(qd[sl], st))
            st = st * cdec + _dot_tn(kd[sl], vb[sl])
        s_s[h] = st
        o = jnp.concatenate(outs, axis=0)
        y = _rms_rows(o) * _silu(g_ref[:, hs])
        y_ref[:, y_off + h * HEAD_DIM:y_off + (h + 1) * HEAD_DIM] = y.astype(y_ref.dtype)

    @pl.when(t == pl.num_programs(1) - 1)
    def _():
        s_out[0] = s_s[...]


def _mlstm_prompt_body(q_ref, k_ref, v_ref, og_ref, sm_ref, gb_ref, nw_ref,
                       y_ref, c_out, n_out, m_out, c_s, n_s, m_s, *, tb, y_off):
    t = pl.program_id(1)
    nc = tb // CHUNK

    @pl.when(t == 0)
    def _():
        c_s[...] = jnp.zeros_like(c_s)
        n_s[...] = jnp.zeros_like(n_s)
        m_s[...] = jnp.zeros_like(m_s)

    _, _, _, tril, _ = _chunk_masks(tb)
    ltri = tril.astype(f32)
    pre = sm_ref[...] + gb_ref[...]
    fall = _dot_exact(ltri, jax.nn.log_sigmoid(pre))
    imf_t = (pre - pltpu.roll(fall, LANES - (L_MLF - L_MLI), 1)).T
    for h in range(HEADS):
        hs = slice(h * HEAD_DIM, (h + 1) * HEAD_DIM)
        fcol = fall[:, L_MLF + h:L_MLF + h + 1]
        icol = pre[:, L_MLI + h:L_MLI + h + 1]
        dm = jnp.where(tril, fcol + imf_t[h:h + 1, :], NEG_INF)
        dmax = jnp.max(dm, axis=-1, keepdims=True)
        mp = m_s[h:h + 1, 0:1]
        m_prev, m_int, m_cur = [], [], []
        for cc in range(nc):
            sl = slice(cc * CHUNK, (cc + 1) * CHUNK)
            mi = mp + fcol[sl]
            mt = jnp.maximum(mi, dmax[sl])
            m_prev.append(mp)
            m_int.append(mi)
            m_cur.append(mt)
            mp = mt[CHUNK - 1:CHUNK, :]
        m_t = jnp.concatenate(m_cur, axis=0)
        w = jnp.exp(dm - m_t)
        qh = q_ref[:, hs]
        kh = k_ref[:, hs] * HEAD_DIM ** -0.5
        vb = v_ref[:, hs].astype(bf16)
        s = _dot_nt(qh, kh) * w
        num_in = _dot(s, vb)
        den_in = jnp.sum(s, axis=-1, keepdims=True)
        cst = c_s[h]
        nst = n_s[h:h + 1, :]
        outs = []
        for cc in range(nc):
            sl = slice(cc * CHUNK, (cc + 1) * CHUNK)
            mi, mt = m_int[cc], m_cur[cc]
            gi = jnp.exp(mi - mt)
            qc = qh[sl]
            num = num_in[sl] + gi * _dot(qc, cst)
            den = den_in[sl] + gi * jnp.sum(qc * nst, axis=-1, keepdims=True)
            outs.append(num / jnp.maximum(jnp.abs(den), jnp.exp(-mt)))
            m_new = mt[CHUNK - 1:CHUNK, :]
            fl = fcol[sl][CHUNK - 1:CHUNK, :]
            wk = jnp.exp(fl - fcol[sl] + icol[sl] - m_new)
            dec = jnp.exp(m_prev[cc] + fl - m_new)
            kw = kh[sl] * wk
            cst = dec * cst + _dot_tn(kw, vb[sl])
            nst = dec * nst + jnp.sum(kw, axis=0, keepdims=True)
        c_s[h] = cst
        n_s[h:h + 1, :] = nst
        m_s[h:h + 1, :] = jnp.broadcast_to(mp, (1, LANES))
        hh = jnp.concatenate(outs, axis=0)
        y = _rms_rows(hh) * nw_ref[:, hs] * jax.nn.sigmoid(og_ref[:, hs])
        y_ref[:, y_off + h * HEAD_DIM:y_off + (h + 1) * HEAD_DIM] = y.astype(y_ref.dtype)

    @pl.when(t == pl.num_programs(1) - 1)
    def _():
        c_out[0] = c_s[...]
        n_out[0] = n_s[0:HEADS, :]
        m_out[0] = m_s[...]


def _unit_lower_inverse_minus_eye(a, r, c):
    def coupling(s):
        return jnp.logical_and((r // (2 * s)) == (c // (2 * s)), (r // s) != (c // s))

    x = jnp.where(r == c, 1.0, jnp.where(coupling(1), -a, 0.0))
    s = 2
    while s < CHUNK:
        x = x - _dot(x, _dot(jnp.where(coupling(s), a, 0.0), x))
        s *= 2
    return jnp.where(r == c, 0.0, x)


def _gdn_prompt_body(q_ref, k_ref, v_ref, z_ref, sm_ref, gb_ref, al_ref, wc_ref, cs_ref, nw_ref,
                     y_ref, s_out, conv_out, s_s, xb_s, vn_s, *, tb, y_off):
    t = pl.program_id(1)
    nc = tb // CHUNK
    gw = GROUP_WIDTH
    pad = SUBLANES
    hist = CONV_W - 1

    @pl.when(t == 0)
    def _():
        s_s[...] = jnp.zeros_like(s_s)
        for p in range(3):
            xb_s[p, 0:pad, :] = jnp.zeros((pad, gw), f32)
            xb_s[p, pad - hist:pad, :] = cs_ref[0, :, p * gw:(p + 1) * gw]

    @pl.when(t > 0)
    def _():
        for p in range(3):
            xb_s[p, 0:pad, :] = xb_s[p, tb:tb + pad, :]

    xb_s[0, pad:, :] = q_ref[...]
    xb_s[1, pad:, :] = k_ref[...]
    xb_s[2, pad:, :] = v_ref[...]

    conv = []
    for p in range(3):
        acc = xb_s[p, pad - hist:pad - hist + tb, :] * wc_ref[0:1, p * gw:(p + 1) * gw]
        for j in range(1, CONV_W):
            acc = acc + xb_s[p, pad - hist + j:pad - hist + j + tb, :] * wc_ref[j:j + 1, p * gw:(p + 1) * gw]
        conv.append(_silu(acc))

    r, c, _, tril, strict = _chunk_masks(tb)
    ltri = tril.astype(f32)
    pre = sm_ref[...] + gb_ref[...]
    beta_all = jax.nn.sigmoid(pre)
    g_all = -jnp.exp(al_ref[...]) * jax.nn.softplus(pre)
    gall = _dot_exact(ltri, g_all)
    gall_t = gall.T
    for h in range(HEADS):
        hs = slice(h * HEAD_DIM, (h + 1) * HEAD_DIM)
        gcol = gall[:, L_GDA + h:L_GDA + h + 1]
        grow = gall_t[L_GDA + h:L_GDA + h + 1, :]
        bcol = beta_all[:, L_GDB + h:L_GDB + h + 1]
        decay = jnp.where(tril, jnp.exp(jnp.where(tril, gcol - grow, 0.0)), 0.0)
        qh = _l2n_rows(conv[0][:, hs]) * HEAD_DIM ** -0.5
        kh = _l2n_rows(conv[1][:, hs])
        vh = conv[2][:, hs]
        kb = kh * bcol
        egc = jnp.exp(gcol)
        a = jnp.where(strict, _dot_nt(kb, kh) * decay, 0.0)
        rhs = jnp.concatenate([vh * bcol, kb * egc], axis=1)
        x = rhs + _dot(_unit_lower_inverse_minus_eye(a, r, c), rhs)
        u = x[:, :HEAD_DIM]
        w = x[:, HEAD_DIM:]
        attn = (_dot_nt(qh, kh) * decay).astype(bf16)
        qe = qh * egc
        st = s_s[h]
        vn_s[...] = jnp.zeros_like(vn_s)
        outs = []
        for cc in range(nc):
            sl = slice(cc * CHUNK, (cc + 1) * CHUNK)
            vnew = u[sl] - _dot(w[sl], st)
            vn_s[sl, :] = vnew.astype(bf16)
            outs.append(_dot(qe[sl], st) + jnp.dot(attn[sl, :], vn_s[...], preferred_element_type=f32))
            gl = gcol[sl][CHUNK - 1:CHUNK, :]
            kdec = kh[sl] * jnp.exp(gl - gcol[sl])
            st = st * jnp.exp(gl) + _dot_tn(kdec, vnew)
        s_s[h] = st
        o = jnp.concatenate(outs, axis=0)
        y = _rms_rows(o) * nw_ref[...] * _silu(z_ref[:, hs])
        y_ref[:, y_off + h * HEAD_DIM:y_off + (h + 1) * HEAD_DIM] = y.astype(y_ref.dtype)

    @pl.when(t == pl.num_programs(1) - 1)
    def _():
        s_out[0] = s_s[...]
        for p in range(3):
            conv_out[0, :, p * gw:(p + 1) * gw] = xb_s[p, tb + pad - hist:tb + pad, :]


def _gla_prompt_body(q_ref, k_ref, v_ref, z_ref, sm_ref, w2_ref, b_ref, nw_ref, seg_ref,
                     y_ref, s_out, s_s, *, tb, y_off):
    t = pl.program_id(1)
    nc = tb // CHUNK
    dk = GLA_KEY_DIM
    sub = GLA_SUB
    nsub = CHUNK // sub

    @pl.when(t == 0)
    def _():
        s_s[...] = jnp.zeros_like(s_s)

    r, c, same, tril, _ = _chunk_masks(tb)
    ltri = tril.astype(f32)
    pre = _dot_exact(sm_ref[...], w2_ref[...]) + b_ref[...]
    gk = jax.nn.log_sigmoid(pre) * (1.0 / GLA_GATE_DENOM)
    g = _dot_exact(ltri, gk)
    q = q_ref[...] * dk ** -0.5
    k = k_ref[...]
    rpos = lax.broadcasted_iota(jnp.int32, (tb, 1), 0) % CHUNK
    lane = lax.broadcasted_iota(jnp.int32, (1, LANES), 1)
    lane_lo = lane < dk

    a_off = [jnp.zeros((tb, tb), f32) for _ in range(HEADS)]
    g3 = g.reshape(nc, CHUNK, HEADS * dk)
    for i in range(1, nsub):
        ri = jnp.broadcast_to(g3[:, sub * i:sub * i + 1, :], (nc, CHUNK, HEADS * dk)).reshape(tb, HEADS * dk)
        qs = q * jnp.exp(jnp.where(rpos // sub == i, g - ri, NEG_INF))
        ks = k * jnp.exp(jnp.where(rpos < sub * i, ri - g, NEG_INF))
        for h in range(HEADS):
            ps = slice((h // 2) * LANES, (h // 2 + 1) * LANES)
            keep = lane_lo if h % 2 == 0 else jnp.logical_not(lane_lo)
            a_off[h] = a_off[h] + _dot_nt(jnp.where(keep, qs[:, ps], 0.0), ks[:, ps])

    band = jnp.zeros((tb, LANES), f32)
    for j in range(sub):
        ksh = k if j == 0 else pltpu.roll(k, j, 0)
        gsh = g if j == 0 else pltpu.roll(g, j, 0)
        valid = (rpos % sub) >= j
        prod = q * ksh * jnp.exp(jnp.where(valid, g - gsh, NEG_INF))
        band = band + jnp.dot(prod.astype(bf16), seg_ref[j], preferred_element_type=f32)
    if tb > LANES:
        band = jnp.concatenate([band, jnp.zeros((tb, tb - LANES), f32)], axis=1)
    in_sub = jnp.logical_and((r // sub) == (c // sub), r >= c)
    a_diag = []
    for h in range(HEADS):
        shift = (tb - (sub - 1) - sub * h) % tb
        a_diag.append(jnp.where(in_sub, pltpu.roll(band, shift, 1, stride=1, stride_axis=0), 0.0))

    eg = jnp.exp(g)
    qe = q * eg
    g_t = g.T
    for p in range(HEADS // 2):
        ps = slice(p * LANES, (p + 1) * LANES)
        st = s_s[p]
        o_in = []
        vbs = []
        for hh in range(2):
            h = 2 * p + hh
            hs = slice(h * HEAD_DIM, (h + 1) * HEAD_DIM)
            vbs.append(v_ref[:, hs].astype(bf16))
            a = a_diag[h] + jnp.where(same, a_off[h], 0.0)
            o_in.append(_dot(a, vbs[hh]))
        outs = [[], []]
        for cc in range(nc):
            sl = slice(cc * CHUNK, (cc + 1) * CHUNK)
            last = cc * CHUNK + CHUNK - 1
            gl_row = g[last:last + 1, ps]
            gl_col = g_t[ps, last:last + 1]
            kdec = k[sl, ps] * jnp.exp(gl_row - g[sl, ps])
            upd = jnp.zeros((LANES, HEAD_DIM), f32)
            for hh in range(2):
                keep = lane_lo if hh == 0 else jnp.logical_not(lane_lo)
                outs[hh].append(o_in[hh][sl] + _dot(jnp.where(keep, qe[sl, ps], 0.0), st))
                upd = upd + _dot_tn(jnp.where(keep, kdec, 0.0), vbs[hh][sl])
            st = st * jnp.exp(gl_col) + upd
        s_s[p] = st
        for hh in range(2):
            h = 2 * p + hh
            hs = slice(h * HEAD_DIM, (h + 1) * HEAD_DIM)
            o = jnp.concatenate(outs[hh], axis=0)
            y = _rms_rows(o) * nw_ref[...] * _silu(z_ref[:, hs])
            y_ref[:, y_off + h * HEAD_DIM:y_off + (h + 1) * HEAD_DIM] = y.astype(y_ref.dtype)

    @pl.when(t == pl.num_programs(1) - 1)
    def _():
        s_out[0] = s_s[...]


def _gla_segment_table():
    d = jnp.arange(HEADS * GLA_KEY_DIM)[None, :, None]
    p = jnp.arange(LANES)[None, None, :]
    j = jnp.arange(GLA_SUB)[:, None, None]
    return (p == (d // GLA_KEY_DIM) * GLA_SUB + (GLA_SUB - 1) - j).astype(bf16)


def _prompt_mixer_kernel(rq, rk, rv, rg, mq, mk, mv, mo, dq, dk, dv, dz, lq, lk, lv, lz, sm_ref,
                         cos_ref, sin_ref, gb_ref, al_ref, wc_ref, cs_ref, mlnw_ref, gdnw_ref, glnw_ref,
                         w2_ref, glb_ref, seg_ref,
                         y_ref, ret_o, mlc_o, mln_o, mlm_o, gdn_o, conv_o, gla_o,
                         ret_s, mlc_s, mln_s, mlm_s, gdn_s, xb_s, vn_s, gla_s, *, tb):
    gw = GROUP_WIDTH
    _ret_prompt_body(rq, rk, rv, rg, cos_ref, sin_ref, y_ref, ret_o, ret_s, tb=tb, y_off=0)
    _mlstm_prompt_body(mq, mk, mv, mo, sm_ref, gb_ref, mlnw_ref, y_ref, mlc_o, mln_o, mlm_o,
                       mlc_s, mln_s, mlm_s, tb=tb, y_off=gw)
    _gdn_prompt_body(dq, dk, dv, dz, sm_ref, gb_ref, al_ref, wc_ref, cs_ref, gdnw_ref, y_ref, gdn_o, conv_o,
                     gdn_s, xb_s, vn_s, tb=tb, y_off=2 * gw)
    _gla_prompt_body(lq, lk, lv, lz, sm_ref, w2_ref, glb_ref, glnw_ref, seg_ref, y_ref, gla_o, gla_s,
                     tb=tb, y_off=3 * gw)


def _prompt_mixers(c_all, lp, *, batch, seq, conv_zero):
    tb = MIX_TB
    nt = seq // tb
    gw = GROUP_WIDTH
    kw = HEADS * GLA_KEY_DIM

    def cspec(name, width):
        blk = C_OFF[name] // width
        assert C_OFF[name] % width == 0
        return pl.BlockSpec((tb, width), lambda b, t: (b * nt + t, blk))

    def const2(shape):
        return pl.BlockSpec(shape, lambda b, t: (0, 0))

    def per_batch(shape):
        return pl.BlockSpec((1,) + shape, lambda b, t: (b,) + (0,) * len(shape))

    names = [("ret_q", gw), ("ret_k", gw), ("ret_v", gw), ("ret_g", gw),
             ("ml_q", gw), ("ml_k", gw), ("ml_v", gw), ("ml_o", gw),
             ("gd_q", gw), ("gd_k", gw), ("gd_v", gw), ("gd_z", gw),
             ("gl_q", kw), ("gl_k", kw), ("gl_v", gw), ("gl_z", gw), ("small", LANES)]
    in_specs = [cspec(n, w) for n, w in names] + [
        pl.BlockSpec((tb, HEAD_DIM), lambda b, t: (t, 0)), pl.BlockSpec((tb, HEAD_DIM), lambda b, t: (t, 0)),
        const2((1, LANES)), const2((1, LANES)), const2((CONV_W, 3 * gw)), per_batch((CONV_W - 1, 3 * gw)),
        const2((1, gw)), const2((1, HEAD_DIM)), const2((1, HEAD_DIM)), const2((LANES, kw)), const2((1, kw)),
        pl.BlockSpec((GLA_SUB, kw, LANES), lambda b, t: (0, 0, 0))]
    assert tb % LANES == 0
    state_shape = (HEADS, HEAD_DIM, HEAD_DIM)
    out_shapes = [(c_all.shape[0], D_MODEL), (batch,) + state_shape, (batch,) + state_shape,
                  (batch, HEADS, HEAD_DIM), (batch, SUBLANES, LANES), (batch,) + state_shape,
                  (batch, CONV_W - 1, 3 * gw), (batch, HEADS // 2, LANES, HEAD_DIM)]
    out_specs = [pl.BlockSpec((tb, D_MODEL), lambda b, t: (b * nt + t, 0))] + [
        per_batch(s[1:]) for s in out_shapes[1:]]
    outs = pl.pallas_call(
        functools.partial(_prompt_mixer_kernel, tb=tb),
        grid=(batch, nt),
        in_specs=in_specs,
        out_specs=out_specs,
        out_shape=[jax.ShapeDtypeStruct(out_shapes[0], bf16)] + [jax.ShapeDtypeStruct(s, f32) for s in out_shapes[1:]],
        scratch_shapes=[pltpu.VMEM(state_shape, f32), pltpu.VMEM(state_shape, f32),
                        pltpu.VMEM((SUBLANES, HEAD_DIM), f32), pltpu.VMEM((SUBLANES, LANES), f32),
                        pltpu.VMEM(state_shape, f32), pltpu.VMEM((3, tb + SUBLANES, gw), f32),
                        pltpu.VMEM((tb, HEAD_DIM), bf16), pltpu.VMEM((HEADS // 2, LANES, HEAD_DIM), f32)],
        compiler_params=_cparams(("parallel", "arbitrary")),
        name="prompt_mixers",
    )(*([c_all] * len(names)), lp["cos_p"], lp["sin_p"], lp["gate_bias"], lp["alog_row"], lp["gd_conv"],
      conv_zero, lp["ml_norm"], lp["gd_norm"], lp["gl_norm"], lp["gl_w2p"], lp["gl_b"], _gla_segment_table())
    y, s_ret, s_c, s_n, s_m, s_gdn, s_conv, s_gla = outs
    states = {"ret": s_ret, "ml_C": s_c, "ml_n": s_n, "ml_m": s_m[:, :HEADS, 0],
              "gdn": s_gdn, "gdn_conv": s_conv,
              "gla": s_gla.reshape(batch, HEADS, GLA_KEY_DIM, HEAD_DIM)}
    return y, states


def _cols(x):
    g = x.shape[0]
    return jnp.concatenate([x, jnp.zeros((LANES - g, LANES), f32)], axis=0).T


def _sample_mixer_kernel(c_ref, cos_ref, sin_ref, gb_ref, al_ref, wc_ref, mlnw_ref, gdnw_ref, glnw_ref,
                         w2_ref, glb_ref,
                         ret_ref, mlc_ref, mln_ref, mlm_ref, gdn_ref, conv_ref, gla_ref, *rest):
    y_ref, ret_o, mlc_o, mln_o, mlm_o, gdn_o, conv_o, gla_o = rest[-8:]
    G = SAMPLE_G
    gw = GROUP_WIDTH
    dk = GLA_KEY_DIM

    def cblk(name, width):
        return c_ref[:, C_OFF[name]:C_OFF[name] + width]

    small = cblk("small", LANES)
    pre = small + gb_ref[...]
    cosf = cos_ref[...]
    sinf = sin_ref[...]

    for h in range(HEADS):
        hs = slice(h * HEAD_DIM, (h + 1) * HEAD_DIM)
        gamma = 1.0 - 2.0 ** (-5.0 - h)
        q = _rope(cblk("ret_q", gw)[:, hs], cosf, sinf)
        k = _rope(cblk("ret_k", gw)[:, hs], cosf, sinf) * HEAD_DIM ** -0.5
        v = cblk("ret_v", gw)[:, hs]
        qt, kt = _cols(q), _cols(k)
        rows = []
        for j in range(G):
            s_new = gamma * ret_ref[j, h] + kt[:, j:j + 1] * v[j:j + 1, :]
            ret_o[j, h] = s_new
            rows.append(jnp.sum(qt[:, j:j + 1] * s_new, axis=0, keepdims=True))
        o = jnp.concatenate(rows, axis=0)
        y = _rms_rows(o) * _silu(cblk("ret_g", gw)[:, hs])
        y_ref[:, 0 * gw + h * HEAD_DIM:0 * gw + (h + 1) * HEAD_DIM] = y.astype(y_ref.dtype)

    m_old = mlm_ref[...]
    f_al = pltpu.roll(jax.nn.log_sigmoid(pre), LANES - (L_MLF - L_MLI), 1)
    m_int = m_old + f_al
    m_t = jnp.maximum(m_int, pre)
    w_in = jnp.exp(pre - m_t)
    g_in = jnp.exp(m_int - m_t)
    e_neg = jnp.exp(-m_t)
    mlm_o[...] = m_t
    for h in range(HEADS):
        hs = slice(h * HEAD_DIM, (h + 1) * HEAD_DIM)
        q = cblk("ml_q", gw)[:, hs]
        k = cblk("ml_k", gw)[:, hs] * HEAD_DIM ** -0.5
        v = cblk("ml_v", gw)[:, hs]
        qt, kt = _cols(q), _cols(k)
        rows = []
        for j in range(G):
            wj = w_in[j:j + 1, h:h + 1]
            gj = g_in[j:j + 1, h:h + 1]
            c_new = gj * mlc_ref[j, h] + (kt[:, j:j + 1] * wj) * v[j:j + 1, :]
            mlc_o[j, h] = c_new
            n_new = gj * mln_ref[j, h:h + 1, :] + wj * k[j:j + 1, :]
            mln_o[j, h:h + 1, :] = n_new
            num = jnp.sum(qt[:, j:j + 1] * c_new, axis=0, keepdims=True)
            den = jnp.sum(q[j:j + 1, :] * n_new, axis=-1, keepdims=True)
            rows.append(num / jnp.maximum(jnp.abs(den), e_neg[j:j + 1, h:h + 1]))
        hh = jnp.concatenate(rows, axis=0)
        y = _rms_rows(hh) * mlnw_ref[:, hs] * jax.nn.sigmoid(cblk("ml_o", gw)[:, hs])
        y_ref[:, 1 * gw + h * HEAD_DIM:1 * gw + (h + 1) * HEAD_DIM] = y.astype(y_ref.dtype)

    beta_all = jax.nn.sigmoid(pre)
    eg_all = jnp.exp(-jnp.exp(al_ref[...]) * jax.nn.softplus(pre))
    conv = []
    for p, name in enumerate(("gd_q", "gd_k", "gd_v")):
        ps = slice(p * gw, (p + 1) * gw)
        x = cblk(name, gw)
        acc = x * wc_ref[CONV_W - 1:CONV_W, ps]
        for j in range(CONV_W - 1):
            acc = acc + conv_ref[j, :, ps] * wc_ref[j:j + 1, ps]
        conv.append(_silu(acc))
        for j in range(CONV_W - 2):
            conv_o[j, :, ps] = conv_ref[j + 1, :, ps]
        conv_o[CONV_W - 2, :, ps] = x
    for h in range(HEADS):
        hs = slice(h * HEAD_DIM, (h + 1) * HEAD_DIM)
        q = _l2n_rows(conv[0][:, hs]) * HEAD_DIM ** -0.5
        k = _l2n_rows(conv[1][:, hs])
        v = conv[2][:, hs]
        qt, kt = _cols(q), _cols(k)
        rows = []
        for j in range(G):
            bj = beta_all[j:j + 1, L_GDB + h:L_GDB + h + 1]
            ej = eg_all[j:j + 1, L_GDA + h:L_GDA + h + 1]
            s_old = gdn_ref[j, h]
            ks = jnp.sum(kt[:, j:j + 1] * s_old, axis=0, keepdims=True)
            v_new = bj * (v[j:j + 1, :] - ej * ks)
            s_new = ej * s_old + kt[:, j:j + 1] * v_new
            gdn_o[j, h] = s_new
            rows.append(jnp.sum(qt[:, j:j + 1] * s_new, axis=0, keepdims=True))
        o = jnp.concatenate(rows, axis=0)
        y = _rms_rows(o) * gdnw_ref[...] * _silu(cblk("gd_z", gw)[:, hs])
        y_ref[:, 2 * gw + h * HEAD_DIM:2 * gw + (h + 1) * HEAD_DIM] = y.astype(y_ref.dtype)

    gk = jax.nn.log_sigmoid(_dot_exact(small, w2_ref[...]) + glb_ref[...]) * (1.0 / GLA_GATE_DENOM)
    eg = jnp.exp(gk)
    q_all = cblk("gl_q", HEADS * dk) * dk ** -0.5
    k_all = cblk("gl_k", HEADS * dk)
    upper = lax.broadcasted_iota(jnp.int32, (LANES, 1), 0) < dk
    for p in range(HEADS // 2):
        ps = slice(p * LANES, (p + 1) * LANES)
        qt, kt, et = _cols(q_all[:, ps]), _cols(k_all[:, ps]), _cols(eg[:, ps])
        v0 = cblk("gl_v", gw)[:, (2 * p) * HEAD_DIM:(2 * p + 1) * HEAD_DIM]
        v1 = cblk("gl_v", gw)[:, (2 * p + 1) * HEAD_DIM:(2 * p + 2) * HEAD_DIM]
        rows0, rows1 = [], []
        for j in range(G):
            v2 = jnp.where(upper, v0[j:j + 1, :], v1[j:j + 1, :])
            s_new = et[:, j:j + 1] * gla_ref[j, p] + kt[:, j:j + 1] * v2
            gla_o[j, p] = s_new
            qs = qt[:, j:j + 1] * s_new
            rows0.append(jnp.sum(qs[:dk], axis=0, keepdims=True))
            rows1.append(jnp.sum(qs[dk:], axis=0, keepdims=True))
        for hh, rows in enumerate((rows0, rows1)):
            h = 2 * p + hh
            hs = slice(h * HEAD_DIM, (h + 1) * HEAD_DIM)
            o = jnp.concatenate(rows, axis=0)
            y = _rms_rows(o) * glnw_ref[...] * _silu(cblk("gl_z", gw)[:, hs])
            y_ref[:, 3 * gw + h * HEAD_DIM:3 * gw + (h + 1) * HEAD_DIM] = y.astype(y_ref.dtype)


def _sample_mixers(c_all, lp, st, y_prev, st_prev, *, layer, depth, row0, nb):
    G = SAMPLE_G
    gw = GROUP_WIDTH
    kw = HEADS * GLA_KEY_DIM
    assert row0 % G == 0 and nb % G == 0
    blk0 = row0 // G

    def const2(shape):
        return pl.BlockSpec(shape, lambda i: (0, 0))

    def slab(shape, batch_axis=0):
        def index(i):
            idx = [0] * len(shape)
            idx[batch_axis] = i
            return (layer,) + tuple(idx)
        return pl.BlockSpec((None,) + shape, index)

    state_specs = [slab((G, HEADS, HEAD_DIM, HEAD_DIM)), slab((G, HEADS, HEAD_DIM, HEAD_DIM)),
                   slab((G, HEADS, HEAD_DIM)), slab((G, LANES)), slab((G, HEADS, HEAD_DIM, HEAD_DIM)),
                   slab((CONV_W - 1, G, 3 * gw), batch_axis=1), slab((G, HEADS // 2, LANES, HEAD_DIM))]
    order = ("ret", "ml_C", "ml_n", "ml_m", "gdn", "gdn_conv", "gla")
    states = [st[n] for n in order]
    y_spec = pl.BlockSpec((G, D_MODEL), lambda i: (blk0 + i, 0))

    args = [c_all, lp["cos_s"], lp["sin_s"], lp["gate_bias"], lp["alog_row"], lp["gd_conv"], lp["ml_norm"],
            lp["gd_norm"], lp["gl_norm"], lp["gl_w2p"], lp["gl_b"]] + states
    in_specs = [pl.BlockSpec((G, C_WIDTH), lambda i: (blk0 + i, 0)),
                const2((1, HEAD_DIM)), const2((1, HEAD_DIM)), const2((1, LANES)), const2((1, LANES)),
                const2((CONV_W, 3 * gw)), const2((1, gw)), const2((1, HEAD_DIM)), const2((1, HEAD_DIM)),
                const2((LANES, kw)), const2((1, kw))] + state_specs
    inplace = [y_prev] + ([st_prev[n] for n in order] if st_prev is not None else [])
    aliases = {len(args) + k: k for k in range(len(inplace))}
    args += inplace
    in_specs += [pl.BlockSpec(memory_space=pl.ANY)] * len(inplace)

    outs = pl.pallas_call(
        _sample_mixer_kernel,
        grid=(nb // G,),
        in_specs=in_specs,
        out_specs=[y_spec] + state_specs,
        out_shape=[jax.ShapeDtypeStruct(y_prev.shape, y_prev.dtype)] + [
            jax.ShapeDtypeStruct(s.shape, s.dtype) for s in states],
        input_output_aliases=aliases,
        compiler_params=_cparams(("parallel",)),
        name="sample_mixers",
    )(*args)
    return outs[0], dict(zip(order, outs[1:]))


def _rope_tables(pos):
    half = HEAD_DIM // 2
    inv = ROPE_BASE ** (-jnp.arange(half, dtype=f32) / half)
    ang = pos[:, None] * inv[None, :]
    cos, sin = jnp.cos(ang), jnp.sin(ang)
    return jnp.concatenate([cos, cos], axis=-1), jnp.concatenate([-sin, sin], axis=-1)


def _permute_w_in(w):
    big = jnp.concatenate([w[..., 0:4096], w[..., 4104:6152], w[..., 6160:7696]], axis=-1)
    small = jnp.concatenate([w[..., 4096:4104], w[..., 6152:6160], w[..., 7696:7712]], axis=-1)
    pad = jnp.zeros(w.shape[:-1] + (C_WIDTH - big.shape[-1] - small.shape[-1],), w.dtype)
    return jnp.concatenate([big, small, pad], axis=-1)


def _lane_row(pieces):
    row = jnp.zeros((1, LANES), f32)
    for off, val in pieces:
        row = lax.dynamic_update_slice(row, val.reshape(1, -1).astype(f32), (0, off))
    return row


def kernel(x_prompt, x_sample, state_ret, state_mlstm_C, state_mlstm_n, state_mlstm_m, state_gdn, state_gdn_conv, state_gla, cache_mem_k, cache_mem_v, mem_prompt, norm_mix_pre, norm_mix_post, w_in, ml_ib, ml_fb, ml_norm, gd_conv, gd_A_log, gd_dt_bias, gd_norm, gl_w2, gl_b, gl_norm, w_out, norm_x_pre, norm_x_post, norm_mem, w_xq, w_xk, w_xv, w_xo, norm_mlp_pre, norm_mlp_post, w_up, w_down):
    bp, tp, d = x_prompt.shape
    bs, ts, _ = x_sample.shape
    depth = w_in.shape[0]
    assert ts == 1 and d == D_MODEL and tp % MIX_TB == 0
    rows_p = bp * tp
    rows = rows_p + bs
    assert rows % ROW_TILE == 0 and rows_p % SAMPLE_G == 0

    x, h = _embed(x_prompt.reshape(rows_p, d), x_sample.reshape(bs, d), norm_mix_pre[0], tm=LANES)
    w_in_b = _permute_w_in(w_in.astype(bf16))
    w_out_b, w_xo_b, w_down_b = w_out.astype(bf16), w_xo.astype(bf16), w_down.astype(bf16)
    cos_p, sin_p = _rope_tables(jnp.arange(tp, dtype=f32))
    cos_s, sin_s = _rope_tables(jnp.arange(ts, dtype=f32) + PAST_LEN)
    conv_zero = jnp.zeros((bp, CONV_W - 1, 3 * GROUP_WIDTH), f32)
    mem2d = mem_prompt.reshape(bp * MEM_LEN, d)
    st_in = {"ret": state_ret, "ml_C": state_mlstm_C, "ml_n": state_mlstm_n,
             "ml_m": jnp.pad(state_mlstm_m, ((0, 0), (0, 0), (0, LANES - HEADS))),
             "gdn": state_gdn, "gdn_conv": jnp.transpose(state_gdn_conv, (0, 2, 1, 3)),
             "gla": state_gla.reshape(depth, bs, HEADS // 2, LANES, HEAD_DIM)}

    new_p = {n: [] for n in ("ret", "ml_C", "ml_n", "ml_m", "gdn", "gdn_conv", "gla")}
    st_s = None
    mk_p = mv_p = None
    for l in range(depth):
        lp = {
            "cos_p": cos_p, "sin_p": sin_p, "cos_s": cos_s, "sin_s": sin_s,
            "gate_bias": _lane_row([(L_MLI, ml_ib[l]), (L_MLF, ml_fb[l]), (L_GDA, gd_dt_bias[l])]),
            "alog_row": _lane_row([(L_GDA, gd_A_log[l])]),
            "ml_norm": ml_norm[l].reshape(1, GROUP_WIDTH),
            "gd_norm": gd_norm[l].reshape(1, HEAD_DIM),
            "gl_norm": gl_norm[l].reshape(1, HEAD_DIM),
            "gd_conv": gd_conv[l],
            "gl_w2p": jnp.zeros((LANES, HEADS * GLA_KEY_DIM), f32).at[L_GLR:L_GLR + GLA_RANK].set(gl_w2[l]),
            "gl_b": gl_b[l].reshape(1, HEADS * GLA_KEY_DIM),
        }
        mk_p = _norm_matmul_slab(mem2d, norm_mem[l], w_xk, mk_p, layer=l, tm=512, tn=1024, name="mem_k")
        mv_p = _norm_matmul_slab(mem2d, norm_mem[l], w_xv, mv_p, layer=l, tm=512, tn=1024, name="mem_v")

        c_all = _matmul(h, w_in_b, layer=l, tm=ROW_TILE, tn=1024, out_dtype=f32, name="w_in")
        y_all, st_p = _prompt_mixers(c_all, lp, batch=bp, seq=tp, conv_zero=conv_zero)
        y_all, st_s = _sample_mixers(c_all, lp, st_in, y_all, st_s, layer=l, depth=depth, row0=rows_p, nb=bs)
        x, h = _matmul_norm_res(y_all, w_out_b, norm_mix_post[l], x, norm_x_pre[l], layer=l,
                                tm=ROW_TILE, tk=1024, name="w_out")

        q_all = _matmul(h, w_xq, layer=l, tm=ROW_TILE, tn=1024, out_dtype=bf16, name="w_xq")
        o_all = _xattn_prompt(q_all, mk_p, mv_p, layer=l, batch=bp, seq=tp, tq=512)
        o_all = _xattn_sample(q_all, cache_mem_k, cache_mem_v, o_all, layer=l, row0=rows_p, nb=bs)
        x, h = _matmul_norm_res(o_all, w_xo_b, norm_x_post[l], x, norm_mlp_pre[l], layer=l,
                                tm=ROW_TILE, tk=1024, name="w_xo")

        u = _matmul(h, w_up, layer=l, tm=ROW_TILE, tn=1024, out_dtype=bf16, act="relu2", name="w_up")
        g_next = norm_mix_pre[l + 1] if l + 1 < depth else None
        x, h = _matmul_norm_res(u, w_down_b, norm_mlp_post[l], x, g_next, layer=l,
                                tm=ROW_TILE, tk=1024, name="w_down")

        for n in new_p:
            new_p[n].append(st_p[n])

    def stk(lst):
        return jnp.stack(lst, axis=0)

    xp = x[:rows_p].reshape(bp, tp, d)
    xs = x[rows_p:].reshape(bs, ts, d)
    return (xp, xs,
            stk(new_p["ret"]), stk(new_p["ml_C"]), stk(new_p["ml_n"]), stk(new_p["ml_m"]),
            stk(new_p["gdn"]), stk(new_p["gdn_conv"]), stk(new_p["gla"]),
            mk_p.reshape(depth, bp, MEM_LEN, d), mv_p.reshape(depth, bp, MEM_LEN, d),
            st_s["ret"], st_s["ml_C"], st_s["ml_n"], st_s["ml_m"][:, :, :HEADS],
            st_s["gdn"], jnp.transpose(st_s["gdn_conv"], (0, 2, 1, 3)),
            st_s["gla"].reshape(depth, bs, HEADS, GLA_KEY_DIM, HEAD_DIM))
```

```python
import functools
import math

import jax
import jax.numpy as jnp
from jax import lax
from jax.experimental import pallas as pl
from jax.experimental.pallas import tpu as pltpu

f32 = jnp.float32
bf16 = jnp.bfloat16

D_MODEL = 2048
HEADS = 4
HEAD_DIM = 128
GROUP_WIDTH = 512
GLA_KEY_DIM = 64
GLA_RANK = 16
GLA_GATE_DENOM = 16.0
CONV_W = 4
CHUNK = 64
ROPE_BASE = 10000.0
X_HEADS = 4
X_HEAD_DIM = 512
MEM_LEN = 256
D_FF = 8192
EPS = 1e-6
PAST_LEN = 16384
NEG_INF = float("-inf")

LANES = 128
SUBLANES = 8

C_OFF = {
    "ret_q": 0, "ret_k": 512, "ret_v": 1024, "ret_g": 1536,
    "ml_q": 2048, "ml_k": 2560, "ml_v": 3072, "ml_o": 3584,
    "gd_q": 4096, "gd_k": 4608, "gd_v": 5120, "gd_z": 5632,
    "gl_q": 6144, "gl_k": 6400, "gl_v": 6656, "gl_z": 7168,
    "small": 7680,
}
C_WIDTH = 8192
L_MLI, L_MLF, L_GDB, L_GDA, L_GLR = 0, 4, 8, 12, 16

ROW_TILE = 640
MIX_TB = 256
SAMPLE_G = 8
GLA_SUB = 16
VMEM_LIMIT = 56 * 1024 * 1024


def _cparams(sem):
    return pltpu.CompilerParams(dimension_semantics=sem, vmem_limit_bytes=VMEM_LIMIT)


def _dot(a, b):
    return jnp.dot(a.astype(bf16), b.astype(bf16), preferred_element_type=f32)


def _dot_nt(a, b):
    return lax.dot_general(a.astype(bf16), b.astype(bf16), (((1,), (1,)), ((), ())),
                           preferred_element_type=f32)


def _dot_tn(a, b):
    return lax.dot_general(a.astype(bf16), b.astype(bf16), (((0,), (0,)), ((), ())),
                           preferred_element_type=f32)


def _dot_exact(a, b):
    return jnp.dot(a, b, preferred_element_type=f32, precision=lax.Precision.HIGHEST)


def _rms_rows(x):
    return x * lax.rsqrt(jnp.mean(x * x, axis=-1, keepdims=True) + EPS)


def _l2n_rows(x):
    return x * lax.rsqrt(jnp.sum(x * x, axis=-1, keepdims=True) + EPS)


def _silu(x):
    return x * jax.nn.sigmoid(x)


def _rope(x, cosf, sinf):
    return x * cosf + pltpu.roll(x, HEAD_DIM // 2, 1) * sinf


def _chunk_masks(tb):
    r = lax.broadcasted_iota(jnp.int32, (tb, tb), 0)
    c = lax.broadcasted_iota(jnp.int32, (tb, tb), 1)
    same = (r // CHUNK) == (c // CHUNK)
    tril = jnp.logical_and(same, r >= c)
    strict = jnp.logical_and(same, r > c)
    return r, c, same, tril, strict


def _rms_gain(x, g):
    return x * lax.rsqrt(jnp.mean(x * x, axis=-1, keepdims=True) + EPS) * g


def _embed_kernel(xp_ref, xs_ref, g_ref, x_ref, h_ref, *, n_prompt_blocks):
    i = pl.program_id(0)

    @pl.when(i < n_prompt_blocks)
    def _():
        x_ref[...] = xp_ref[...]

    @pl.when(i >= n_prompt_blocks)
    def _():
        x_ref[...] = xs_ref[...]

    h_ref[...] = _rms_gain(x_ref[...], g_ref[...]).astype(bf16)


def _embed(xp2d, xs2d, g, *, tm):
    mp, d = xp2d.shape
    ms = xs2d.shape[0]
    assert mp % tm == 0 and ms % tm == 0
    npb, nsb = mp // tm, ms // tm
    return pl.pallas_call(
        functools.partial(_embed_kernel, n_prompt_blocks=npb),
        grid=(npb + nsb,),
        in_specs=[pl.BlockSpec((tm, d), lambda i: (jnp.minimum(i, npb - 1), 0)),
                  pl.BlockSpec((tm, d), lambda i: (jnp.maximum(i - npb, 0), 0)),
                  pl.BlockSpec((1, d), lambda i: (0, 0))],
        out_specs=[pl.BlockSpec((tm, d), lambda i: (i, 0)), pl.BlockSpec((tm, d), lambda i: (i, 0))],
        out_shape=[jax.ShapeDtypeStruct((mp + ms, d), f32), jax.ShapeDtypeStruct((mp + ms, d), bf16)],
        compiler_params=_cparams(("parallel",)),
        name="embed",
    )(xp2d, xs2d, g.reshape(1, d))


def _matmul_kernel(h_ref, w_ref, *rest, act, cast_w):
    if cast_w:
        o_ref, wb_ref = rest

        @pl.when(pl.program_id(1) == 0)
        def _():
            wb_ref[...] = w_ref[...].astype(bf16)

        w = wb_ref[...]
    else:
        (o_ref,) = rest
        w = w_ref[...]
    y = jnp.dot(h_ref[...], w, preferred_element_type=f32)
    if act == "relu2":
        y = jnp.square(jnp.maximum(y, 0.0))
    o_ref[...] = y.astype(o_ref.dtype)


def _matmul(h, w, *, layer, tm, tn, out_dtype, name, act=None):
    m, k = h.shape
    n = w.shape[2]
    assert m % tm == 0 and n % tn == 0
    cast_w = w.dtype != bf16
    return pl.pallas_call(
        functools.partial(_matmul_kernel, act=act, cast_w=cast_w),
        grid=(n // tn, m // tm),
        in_specs=[pl.BlockSpec((tm, k), lambda j, i: (i, 0)),
                  pl.BlockSpec((None, k, tn), lambda j, i: (layer, 0, j))],
        out_specs=pl.BlockSpec((tm, tn), lambda j, i: (i, j)),
        out_shape=jax.ShapeDtypeStruct((m, n), out_dtype),
        scratch_shapes=[pltpu.VMEM((k, tn), bf16)] if cast_w else [],
        compiler_params=_cparams(("parallel", "arbitrary")),
        name=name,
    )(h, w)


def _norm_matmul_kernel(x_ref, g_ref, w_ref, *rest):
    o_ref, h_ref = rest[-2:]

    @pl.when(pl.program_id(1) == 0)
    def _():
        h_ref[...] = _rms_gain(x_ref[...], g_ref[...]).astype(bf16)

    o_ref[...] = jnp.dot(h_ref[...], w_ref[...].astype(bf16), preferred_element_type=f32)


def _norm_matmul_slab(x, g, w, prev, *, layer, tm, tn, name):
    m, k = x.shape
    depth, _, n = w.shape
    assert m % tm == 0 and n % tn == 0
    in_specs = [pl.BlockSpec((tm, k), lambda i, j: (i, 0)),
                pl.BlockSpec((1, k), lambda i, j: (0, 0)),
                pl.BlockSpec((None, k, tn), lambda i, j: (layer, 0, j))]
    args = [x, g.reshape(1, k), w]
    aliases = {}
    if prev is not None:
        in_specs.append(pl.BlockSpec(memory_space=pl.ANY))
        args.append(prev)
        aliases = {len(args) - 1: 0}
    return pl.pallas_call(
        _norm_matmul_kernel,
        grid=(m // tm, n // tn),
        in_specs=in_specs,
        out_specs=pl.BlockSpec((None, tm, tn), lambda i, j: (layer, i, j)),
        out_shape=jax.ShapeDtypeStruct((depth, m, n), f32),
        scratch_shapes=[pltpu.VMEM((tm, k), bf16)],
        input_output_aliases=aliases,
        compiler_params=_cparams(("parallel", "arbitrary")),
        name=name,
    )(*args)


def _matmul_norm_res_kernel(a_ref, w_ref, g_ref, r_ref, *rest, next_norm, single_k):
    rest = list(rest)
    gn_ref = rest.pop(0) if next_norm else None
    o_ref = rest.pop(0)
    hn_ref = rest.pop(0) if next_norm else None

    def finish(y):
        x_new = r_ref[...] + _rms_gain(y, g_ref[...])
        o_ref[...] = x_new
        if next_norm:
            hn_ref[...] = _rms_gain(x_new, gn_ref[...]).astype(bf16)

    if single_k:
        finish(jnp.dot(a_ref[...], w_ref[...], preferred_element_type=f32))
        return

    (acc_ref,) = rest
    kk = pl.program_id(1)

    @pl.when(kk == 0)
    def _():
        acc_ref[...] = jnp.zeros_like(acc_ref)

    acc_ref[...] += jnp.dot(a_ref[...], w_ref[...], preferred_element_type=f32)

    @pl.when(kk == pl.num_programs(1) - 1)
    def _():
        finish(acc_ref[...])


def _matmul_norm_res(a, w, g, res, g_next, *, layer, tm, tk, name):
    m, k = a.shape
    n = w.shape[2]
    assert m % tm == 0 and k % tk == 0
    next_norm = g_next is not None
    row = pl.BlockSpec((tm, n), lambda i, j: (i, 0))
    vec = pl.BlockSpec((1, n), lambda i, j: (0, 0))
    in_specs = [pl.BlockSpec((tm, tk), lambda i, j: (i, j)),
                pl.BlockSpec((None, tk, n), lambda i, j: (layer, j, 0)), vec, row]
    args = [a, w, g.reshape(1, n), res]
    out_specs = [row]
    out_shape = [jax.ShapeDtypeStruct((m, n), f32)]
    if next_norm:
        in_specs.append(vec)
        args.append(g_next.reshape(1, n))
        out_specs.append(row)
        out_shape.append(jax.ShapeDtypeStruct((m, n), bf16))
    single_k = k == tk
    outs = pl.pallas_call(
        functools.partial(_matmul_norm_res_kernel, next_norm=next_norm, single_k=single_k),
        grid=(m // tm, k // tk),
        in_specs=in_specs,
        out_specs=out_specs,
        out_shape=out_shape,
        scratch_shapes=[] if single_k else [pltpu.VMEM((tm, n), f32)],
        compiler_params=_cparams(("parallel", "arbitrary")),
        name=name,
    )(*args)
    return (outs[0], outs[1]) if next_norm else (outs[0], None)


def _xattn_prompt_kernel(q_ref, k_ref, v_ref, o_ref, kb_ref, vb_ref):
    @pl.when(pl.program_id(1) == 0)
    def _():
        kb_ref[...] = k_ref[...].astype(bf16)
        vb_ref[...] = v_ref[...].astype(bf16)

    scale = X_HEAD_DIM ** -0.5
    for h in range(X_HEADS):
        sl = slice(h * X_HEAD_DIM, (h + 1) * X_HEAD_DIM)
        s = lax.dot_general(q_ref[:, sl], kb_ref[:, sl], (((1,), (1,)), ((), ())),
                            preferred_element_type=f32) * scale
        m = jnp.max(s, axis=-1, keepdims=True)
        p = jnp.exp(s - m)
        l = jnp.sum(p, axis=-1, keepdims=True)
        o = jnp.dot(p.astype(bf16), vb_ref[:, sl], preferred_element_type=f32) / l
        o_ref[:, sl] = o.astype(o_ref.dtype)


def _xattn_prompt(q_all, mk, mv, *, layer, batch, seq, tq):
    nt = seq // tq
    return pl.pallas_call(
        _xattn_prompt_kernel,
        grid=(batch, nt),
        in_specs=[pl.BlockSpec((tq, D_MODEL), lambda b, t: (b * nt + t, 0)),
                  pl.BlockSpec((None, MEM_LEN, D_MODEL), lambda b, t: (layer, b, 0)),
                  pl.BlockSpec((None, MEM_LEN, D_MODEL), lambda b, t: (layer, b, 0))],
        out_specs=pl.BlockSpec((tq, D_MODEL), lambda b, t: (b * nt + t, 0)),
        out_shape=jax.ShapeDtypeStruct(q_all.shape, bf16),
        scratch_shapes=[pltpu.VMEM((MEM_LEN, D_MODEL), bf16), pltpu.VMEM((MEM_LEN, D_MODEL), bf16)],
        compiler_params=_cparams(("parallel", "arbitrary")),
        name="xattn_prompt",
    )(q_all, mk, mv)


def _xattn_sample_kernel(q_ref, k_ref, v_ref, prev_ref, o_ref, acc_ref):
    del prev_ref
    j = pl.program_id(1)
    scale = X_HEAD_DIM ** -0.5
    rows = lax.broadcasted_iota(jnp.int32, (SAMPLE_G, 1), 0)
    q = jnp.sum(jnp.where(rows == j, q_ref[...].astype(f32), 0.0), axis=0, keepdims=True)
    for h in range(X_HEADS):
        sl = slice(h * X_HEAD_DIM, (h + 1) * X_HEAD_DIM)
        s = jnp.sum(k_ref[:, sl] * q[:, sl], axis=-1, keepdims=True) * scale
        m = jnp.max(s, axis=0, keepdims=True)
        p = jnp.exp(s - m)
        l = jnp.sum(p, axis=0, keepdims=True)
        acc_ref[pl.ds(j, 1), sl] = jnp.sum(p * v_ref[:, sl], axis=0, keepdims=True) / l

    @pl.when(j == SAMPLE_G - 1)
    def _():
        o_ref[...] = acc_ref[...].astype(o_ref.dtype)


def _xattn_sample(q_all, ck, cv, o_prev, *, layer, row0, nb):
    G = SAMPLE_G
    blk0 = row0 // G
    return pl.pallas_call(
        _xattn_sample_kernel,
        grid=(nb // G, G),
        in_specs=[pl.BlockSpec((G, D_MODEL), lambda g, j: (blk0 + g, 0)),
                  pl.BlockSpec((None, None, MEM_LEN, D_MODEL), lambda g, j: (layer, g * G + j, 0, 0)),
                  pl.BlockSpec((None, None, MEM_LEN, D_MODEL), lambda g, j: (layer, g * G + j, 0, 0)),
                  pl.BlockSpec(memory_space=pl.ANY)],
        out_specs=pl.BlockSpec((G, D_MODEL), lambda g, j: (blk0 + g, 0)),
        out_shape=jax.ShapeDtypeStruct(o_prev.shape, o_prev.dtype),
        scratch_shapes=[pltpu.VMEM((G, D_MODEL), f32)],
        input_output_aliases={3: 0},
        compiler_params=_cparams(("parallel", "arbitrary")),
        name="xattn_sample",
    )(q_all, ck, cv, o_prev)


def _ret_prompt_body(q_ref, k_ref, v_ref, g_ref, cos_ref, sin_ref, y_ref, s_out, s_s, *, tb, y_off):
    t = pl.program_id(1)
    nc = tb // CHUNK

    @pl.when(t == 0)
    def _():
        s_s[...] = jnp.zeros_like(s_s)

    cosf = cos_ref[...]
    sinf = sin_ref[...]
    r, c, _, tril, _ = _chunk_masks(tb)
    diff = (r - c).astype(f32)
    ridx = (lax.broadcasted_iota(jnp.int32, (tb, 1), 0) % CHUNK).astype(f32)
    for h in range(HEADS):
        hs = slice(h * HEAD_DIM, (h + 1) * HEAD_DIM)
        lg = math.log(1.0 - 2.0 ** (-5.0 - h))
        decay = jnp.where(tril, jnp.exp(jnp.maximum(diff, 0.0) * lg), 0.0)
        qr = _rope(q_ref[:, hs], cosf, sinf)
        kr = _rope(k_ref[:, hs], cosf, sinf) * HEAD_DIM ** -0.5
        vb = v_ref[:, hs].astype(bf16)
        s = _dot_nt(qr, kr) * decay
        o_in = _dot(s, vb)
        qd = (qr * jnp.exp((ridx + 1.0) * lg)).astype(bf16)
        kd = (kr * jnp.exp((CHUNK - 1.0 - ridx) * lg)).astype(bf16)
        cdec = math.exp(CHUNK * lg)
        st = s_s[h]
        outs = []
        for cc in range(nc):
            sl = slice(cc * CHUNK, (cc + 1) * CHUNK)
            outs.append(o_in[sl] + _dot(qd[sl], st))
            st = st * cdec + _dot_tn(kd[sl], vb[sl])
        s_s[h] = st
        o = jnp.concatenate(outs, axis=0)
        y = _rms_rows(o) * _silu(g_ref[:, hs])
        y_ref[:, y_off + h * HEAD_DIM:y_off + (h + 1) * HEAD_DIM] = y.astype(y_ref.dtype)

    @pl.when(t == pl.num_programs(1) - 1)
    def _():
        s_out[0] = s_s[...]


def _mlstm_prompt_body(q_ref, k_ref, v_ref, og_ref, sm_ref, gb_ref, nw_ref,
                       y_ref, c_out, n_out, m_out, c_s, n_s, m_s, *, tb, y_off):
    t = pl.program_id(1)
    nc = tb // CHUNK

    @pl.when(t == 0)
    def _():
        c_s[...] = jnp.zeros_like(c_s)
        n_s[...] = jnp.zeros_like(n_s)
        m_s[...] = jnp.zeros_like(m_s)

    _, _, _, tril, _ = _chunk_masks(tb)
    ltri = tril.astype(f32)
    pre = sm_ref[...] + gb_ref[...]
    fall = _dot_exact(ltri, jax.nn.log_sigmoid(pre))
    imf_t = (pre - pltpu.roll(fall, LANES - (L_MLF - L_MLI), 1)).T
    for h in range(HEADS):
        hs = slice(h * HEAD_DIM, (h + 1) * HEAD_DIM)
        fcol = fall[:, L_MLF + h:L_MLF + h + 1]
        icol = pre[:, L_MLI + h:L_MLI + h + 1]
        dm = jnp.where(tril, fcol + imf_t[h:h + 1, :], NEG_INF)
        dmax = jnp.max(dm, axis=-1, keepdims=True)
        mp = m_s[h:h + 1, 0:1]
        m_prev, m_int, m_cur = [], [], []
        for cc in range(nc):
            sl = slice(cc * CHUNK, (cc + 1) * CHUNK)
            mi = mp + fcol[sl]
            mt = jnp.maximum(mi, dmax[sl])
            m_prev.append(mp)
            m_int.append(mi)
            m_cur.append(mt)
            mp = mt[CHUNK - 1:CHUNK, :]
        m_t = jnp.concatenate(m_cur, axis=0)
        w = jnp.exp(dm - m_t)
        qh = q_ref[:, hs]
        kh = k_ref[:, hs] * HEAD_DIM ** -0.5
        vb = v_ref[:, hs].astype(bf16)
        s = _dot_nt(qh, kh) * w
        num_in = _dot(s, vb)
        den_in = jnp.sum(s, axis=-1, keepdims=True)
        cst = c_s[h]
        nst = n_s[h:h + 1, :]
        outs = []
        for cc in range(nc):
            sl = slice(cc * CHUNK, (cc + 1) * CHUNK)
            mi, mt = m_int[cc], m_cur[cc]
            gi = jnp.exp(mi - mt)
            qc = qh[sl]
            num = num_in[sl] + gi * _dot(qc, cst)
            den = den_in[sl] + gi * jnp.sum(qc * nst, axis=-1, keepdims=True)
            outs.append(num / jnp.maximum(jnp.abs(den), jnp.exp(-mt)))
            m_new = mt[CHUNK - 1:CHUNK, :]
            fl = fcol[sl][CHUNK - 1:CHUNK, :]
            wk = jnp.exp(fl - fcol[sl] + icol[sl] - m_new)
            dec = jnp.exp(m_prev[cc] + fl - m_new)
            kw = kh[sl] * wk
            cst = dec * cst + _dot_tn(kw, vb[sl])
            nst = dec * nst + jnp.sum(kw, axis=0, keepdims=True)
        c_s[h] = cst
        n_s[h:h + 1, :] = nst
        m_s[h:h + 1, :] = jnp.broadcast_to(mp, (1, LANES))
        hh = jnp.concatenate(outs, axis=0)
        y = _rms_rows(hh) * nw_ref[:, hs] * jax.nn.sigmoid(og_ref[:, hs])
        y_ref[:, y_off + h * HEAD_DIM:y_off + (h + 1) * HEAD_DIM] = y.astype(y_ref.dtype)

    @pl.when(t == pl.num_programs(1) - 1)
    def _():
        c_out[0] = c_s[...]
        n_out[0] = n_s[0:HEADS, :]
        m_out[0] = m_s[...]


def _unit_lower_inverse_minus_eye(a, r, c):
    def coupling(s):
        return jnp.logical_and((r // (2 * s)) == (c // (2 * s)), (r // s) != (c // s))

    x = jnp.where(r == c, 1.0, jnp.where(coupling(1), -a, 0.0))
    s = 2
    while s < CHUNK:
        x = x - _dot(x, _dot(jnp.where(coupling(s), a, 0.0), x))
        s *= 2
    return jnp.where(r == c, 0.0, x)


def _gdn_prompt_body(q_ref, k_ref, v_ref, z_ref, sm_ref, gb_ref, al_ref, wc_ref, cs_ref, nw_ref,
                     y_ref, s_out, conv_out, s_s, xb_s, vn_s, *, tb, y_off):
    t = pl.program_id(1)
    nc = tb // CHUNK
    gw = GROUP_WIDTH
    pad = SUBLANES
    hist = CONV_W - 1

    @pl.when(t == 0)
    def _():
        s_s[...] = jnp.zeros_like(s_s)
        for p in range(3):
            xb_s[p, 0:pad, :] = jnp.zeros((pad, gw), f32)
            xb_s[p, pad - hist:pad, :] = cs_ref[0, :, p * gw:(p + 1) * gw]

    @pl.when(t > 0)
    def _():
        for p in range(3):
            xb_s[p, 0:pad, :] = xb_s[p, tb:tb + pad, :]

    xb_s[0, pad:, :] = q_ref[...]
    xb_s[1, pad:, :] = k_ref[...]
    xb_s[2, pad:, :] = v_ref[...]

    conv = []
    for p in range(3):
        acc = xb_s[p, pad - hist:pad - hist + tb, :] * wc_ref[0:1, p * gw:(p + 1) * gw]
        for j in range(1, CONV_W):
            acc = acc + xb_s[p, pad - hist + j:pad - hist + j + tb, :] * wc_ref[j:j + 1, p * gw:(p + 1) * gw]
        conv.append(_silu(acc))

    r, c, _, tril, strict = _chunk_masks(tb)
    ltri = tril.astype(f32)
    pre = sm_ref[...] + gb_ref[...]
    beta_all = jax.nn.sigmoid(pre)
    g_all = -jnp.exp(al_ref[...]) * jax.nn.softplus(pre)
    gall = _dot_exact(ltri, g_all)
    gall_t = gall.T
    for h in range(HEADS):
        hs = slice(h * HEAD_DIM, (h + 1) * HEAD_DIM)
        gcol = gall[:, L_GDA + h:L_GDA + h + 1]
        grow = gall_t[L_GDA + h:L_GDA + h + 1, :]
        bcol = beta_all[:, L_GDB + h:L_GDB + h + 1]
        decay = jnp.where(tril, jnp.exp(jnp.where(tril, gcol - grow, 0.0)), 0.0)
        qh = _l2n_rows(conv[0][:, hs]) * HEAD_DIM ** -0.5
        kh = _l2n_rows(conv[1][:, hs])
        vh = conv[2][:, hs]
        kb = kh * bcol
        egc = jnp.exp(gcol)
        a = jnp.where(strict, _dot_nt(kb, kh) * decay, 0.0)
        rhs = jnp.concatenate([vh * bcol, kb * egc], axis=1)
        x = rhs + _dot(_unit_lower_inverse_minus_eye(a, r, c), rhs)
        u = x[:, :HEAD_DIM]
        w = x[:, HEAD_DIM:]
        attn = (_dot_nt(qh, kh) * decay).astype(bf16)
        qe = qh * egc
        st = s_s[h]
        vn_s[...] = jnp.zeros_like(vn_s)
        outs = []
        for cc in range(nc):
            sl = slice(cc * CHUNK, (cc + 1) * CHUNK)
            vnew = u[sl] - _dot(w[sl], st)
            vn_s[sl, :] = vnew.astype(bf16)
            outs.append(_dot(qe[sl], st) + jnp.dot(attn[sl, :], vn_s[...], preferred_element_type=f32))
            gl = gcol[sl][CHUNK - 1:CHUNK, :]
            kdec = kh[sl] * jnp.exp(gl - gcol[sl])
            st = st * jnp.exp(gl) + _dot_tn(kdec, vnew)
        s_s[h] = st
        o = jnp.concatenate(outs, axis=0)
        y = _rms_rows(o) * nw_ref[...] * _silu(z_ref[:, hs])
        y_ref[:, y_off + h * HEAD_DIM:y_off + (h + 1) * HEAD_DIM] = y.astype(y_ref.dtype)

    @pl.when(t == pl.num_programs(1) - 1)
    def _():
        s_out[0] = s_s[...]
        for p in range(3):
            conv_out[0, :, p * gw:(p + 1) * gw] = xb_s[p, tb + pad - hist:tb + pad, :]


def _gla_prompt_body(q_ref, k_ref, v_ref, z_ref, sm_ref, w2_ref, b_ref, nw_ref, seg_ref,
                     y_ref, s_out, s_s, *, tb, y_off):
    t = pl.program_id(1)
    nc = tb // CHUNK
    dk = GLA_KEY_DIM
    sub = GLA_SUB
    nsub = CHUNK // sub

    @pl.when(t == 0)
    def _():
        s_s[...] = jnp.zeros_like(s_s)

    r, c, same, tril, _ = _chunk_masks(tb)
    ltri = tril.astype(f32)
    pre = _dot_exact(sm_ref[...], w2_ref[...]) + b_ref[...]
    gk = jax.nn.log_sigmoid(pre) * (1.0 / GLA_GATE_DENOM)
    g = _dot_exact(ltri, gk)
    q = q_ref[...] * dk ** -0.5
    k = k_ref[...]
    rpos = lax.broadcasted_iota(jnp.int32, (tb, 1), 0) % CHUNK
    lane = lax.broadcasted_iota(jnp.int32, (1, LANES), 1)
    lane_lo = lane < dk

    a_off = [jnp.zeros((tb, tb), f32) for _ in range(HEADS)]
    g3 = g.reshape(nc, CHUNK, HEADS * dk)
    for i in range(1, nsub):
        ri = jnp.broadcast_to(g3[:, sub * i:sub * i + 1, :], (nc, CHUNK, HEADS * dk)).reshape(tb, HEADS * dk)
        qs = q * jnp.exp(jnp.where(rpos // sub == i, g - ri, NEG_INF))
        ks = k * jnp.exp(jnp.where(rpos < sub * i, ri - g, NEG_INF))
        for h in range(HEADS):
            ps = slice((h // 2) * LANES, (h // 2 + 1) * LANES)
            keep = lane_lo if h % 2 == 0 else jnp.logical_not(lane_lo)
            a_off[h] = a_off[h] + _dot_nt(jnp.where(keep, qs[:, ps], 0.0), ks[:, ps])

    band = jnp.zeros((tb, LANES), f32)
    for j in range(sub):
        ksh = k if j == 0 else pltpu.roll(k, j, 0)
        gsh = g if j == 0 else pltpu.roll(g, j, 0)
        valid = (rpos % sub) >= j
        prod = q * ksh * jnp.exp(jnp.where(valid, g - gsh, NEG_INF))
        band = band + jnp.dot(prod.astype(bf16), seg_ref[j], preferred_element_type=f32)
    if tb > LANES:
        band = jnp.concatenate([band, jnp.zeros((tb, tb - LANES), f32)], axis=1)
    in_sub = jnp.logical_and((r // sub) == (c // sub), r >= c)
    a_diag = []
    for h in range(HEADS):
        shift = (tb - (sub - 1) - sub * h) % tb
        a_diag.append(jnp.where(in_sub, pltpu.roll(band, shift, 1, stride=1, stride_axis=0), 0.0))

    eg = jnp.exp(g)
    qe = q * eg
    g_t = g.T
    for p in range(HEADS // 2):
        ps = slice(p * LANES, (p + 1) * LANES)
        st = s_s[p]
        o_in = []
        vbs = []
        for hh in range(2):
            h = 2 * p + hh
            hs = slice(h * HEAD_DIM, (h + 1) * HEAD_DIM)
            vbs.append(v_ref[:, hs].astype(bf16))
            a = a_diag[h] + jnp.where(same, a_off[h], 0.0)
            o_in.append(_dot(a, vbs[hh]))
        outs = [[], []]
        for cc in range(nc):
            sl = slice(cc * CHUNK, (cc + 1) * CHUNK)
            last = cc * CHUNK + CHUNK - 1
            gl_row = g[last:last + 1, ps]
            gl_col = g_t[ps, last:last + 1]
            kdec = k[sl, ps] * jnp.exp(gl_row - g[sl, ps])
            upd = jnp.zeros((LANES, HEAD_DIM), f32)
            for hh in range(2):
                keep = lane_lo if hh == 0 else jnp.logical_not(lane_lo)
                outs[hh].append(o_in[hh][sl] + _dot(jnp.where(keep, qe[sl, ps], 0.0), st))
                upd = upd + _dot_tn(jnp.where(keep, kdec, 0.0), vbs[hh][sl])
            st = st * jnp.exp(gl_col) + upd
        s_s[p] = st
        for hh in range(2):
            h = 2 * p + hh
            hs = slice(h * HEAD_DIM, (h + 1) * HEAD_DIM)
            o = jnp.concatenate(outs[hh], axis=0)
            y = _rms_rows(o) * nw_ref[...] * _silu(z_ref[:, hs])
            y_ref[:, y_off + h * HEAD_DIM:y_off + (h + 1) * HEAD_DIM] = y.astype(y_ref.dtype)

    @pl.when(t == pl.num_programs(1) - 1)
    def _():
        s_out[0] = s_s[...]


def _gla_segment_table():
    d = jnp.arange(HEADS * GLA_KEY_DIM)[None, :, None]
    p = jnp.arange(LANES)[None, None, :]
    j = jnp.arange(GLA_SUB)[:, None, None]
    return (p == (d // GLA_KEY_DIM) * GLA_SUB + (GLA_SUB - 1) - j).astype(bf16)


def _prompt_mixer_kernel(rq, rk, rv, rg, mq, mk, mv, mo, dq, dk, dv, dz, lq, lk, lv, lz, sm_ref,
                         cos_ref, sin_ref, gb_ref, al_ref, wc_ref, cs_ref, mlnw_ref, gdnw_ref, glnw_ref,
                         w2_ref, glb_ref, seg_ref,
                         y_ref, ret_o, mlc_o, mln_o, mlm_o, gdn_o, conv_o, gla_o,
                         ret_s, mlc_s, mln_s, mlm_s, gdn_s, xb_s, vn_s, gla_s, *, tb):
    gw = GROUP_WIDTH
    _ret_prompt_body(rq, rk, rv, rg, cos_ref, sin_ref, y_ref, ret_o, ret_s, tb=tb, y_off=0)
    _mlstm_prompt_body(mq, mk, mv, mo, sm_ref, gb_ref, mlnw_ref, y_ref, mlc_o, mln_o, mlm_o,
                       mlc_s, mln_s, mlm_s, tb=tb, y_off=gw)
    _gdn_prompt_body(dq, dk, dv, dz, sm_ref, gb_ref, al_ref, wc_ref, cs_ref, gdnw_ref, y_ref, gdn_o, conv_o,
                     gdn_s, xb_s, vn_s, tb=tb, y_off=2 * gw)
    _gla_prompt_body(lq, lk, lv, lz, sm_ref, w2_ref, glb_ref, glnw_ref, seg_ref, y_ref, gla_o, gla_s,
                     tb=tb, y_off=3 * gw)


def _prompt_mixers(c_all, lp, *, batch, seq, conv_zero):
    tb = MIX_TB
    nt = seq // tb
    gw = GROUP_WIDTH
    kw = HEADS * GLA_KEY_DIM

    def cspec(name, width):
        blk = C_OFF[name] // width
        assert C_OFF[name] % width == 0
        return pl.BlockSpec((tb, width), lambda b, t: (b * nt + t, blk))

    def const2(shape):
        return pl.BlockSpec(shape, lambda b, t: (0, 0))

    def per_batch(shape):
        return pl.BlockSpec((1,) + shape, lambda b, t: (b,) + (0,) * len(shape))

    names = [("ret_q", gw), ("ret_k", gw), ("ret_v", gw), ("ret_g", gw),
             ("ml_q", gw), ("ml_k", gw), ("ml_v", gw), ("ml_o", gw),
             ("gd_q", gw), ("gd_k", gw), ("gd_v", gw), ("gd_z", gw),
             ("gl_q", kw), ("gl_k", kw), ("gl_v", gw), ("gl_z", gw), ("small", LANES)]
    in_specs = [cspec(n, w) for n, w in names] + [
        pl.BlockSpec((tb, HEAD_DIM), lambda b, t: (t, 0)), pl.BlockSpec((tb, HEAD_DIM), lambda b, t: (t, 0)),
        const2((1, LANES)), const2((1, LANES)), const2((CONV_W, 3 * gw)), per_batch((CONV_W - 1, 3 * gw)),
        const2((1, gw)), const2((1, HEAD_DIM)), const2((1, HEAD_DIM)), const2((LANES, kw)), const2((1, kw)),
        pl.BlockSpec((GLA_SUB, kw, LANES), lambda b, t: (0, 0, 0))]
    assert tb % LANES == 0
    state_shape = (HEADS, HEAD_DIM, HEAD_DIM)
    out_shapes = [(c_all.shape[0], D_MODEL), (batch,) + state_shape, (batch,) + state_shape,
                  (batch, HEADS, HEAD_DIM), (batch, SUBLANES, LANES), (batch,) + state_shape,
                  (batch, CONV_W - 1, 3 * gw), (batch, HEADS // 2, LANES, HEAD_DIM)]
    out_specs = [pl.BlockSpec((tb, D_MODEL), lambda b, t: (b * nt + t, 0))] + [
        per_batch(s[1:]) for s in out_shapes[1:]]
    outs = pl.pallas_call(
        functools.partial(_prompt_mixer_kernel, tb=tb),
        grid=(batch, nt),
        in_specs=in_specs,
        out_specs=out_specs,
        out_shape=[jax.ShapeDtypeStruct(out_shapes[0], bf16)] + [jax.ShapeDtypeStruct(s, f32) for s in out_shapes[1:]],
        scratch_shapes=[pltpu.VMEM(state_shape, f32), pltpu.VMEM(state_shape, f32),
                        pltpu.VMEM((SUBLANES, HEAD_DIM), f32), pltpu.VMEM((SUBLANES, LANES), f32),
                        pltpu.VMEM(state_shape, f32), pltpu.VMEM((3, tb + SUBLANES, gw), f32),
                        pltpu.VMEM((tb, HEAD_DIM), bf16), pltpu.VMEM((HEADS // 2, LANES, HEAD_DIM), f32)],
        compiler_params=_cparams(("parallel", "arbitrary")),
        name="prompt_mixers",
    )(*([c_all] * len(names)), lp["cos_p"], lp["sin_p"], lp["gate_bias"], lp["alog_row"], lp["gd_conv"],
      conv_zero, lp["ml_norm"], lp["gd_norm"], lp["gl_norm"], lp["gl_w2p"], lp["gl_b"], _gla_segment_table())
    y, s_ret, s_c, s_n, s_m, s_gdn, s_conv, s_gla = outs
    states = {"ret": s_ret, "ml_C": s_c, "ml_n": s_n, "ml_m": s_m[:, :HEADS, 0],
              "gdn": s_gdn, "gdn_conv": s_conv,
              "gla": s_gla.reshape(batch, HEADS, GLA_KEY_DIM, HEAD_DIM)}
    return y, states


def _rows_pad(x):
    return jnp.concatenate([x, jnp.zeros((LANES - x.shape[0], LANES), f32)], axis=0)


def _cols(x):
    return _rows_pad(x).T


def _row(x, j):
    r = lax.broadcasted_iota(jnp.int32, (x.shape[0], 1), 0)
    return jnp.sum(jnp.where(r == j, x, 0.0), axis=0, keepdims=True)


def _outer(x_cols, j, y_rows):
    lane = lax.broadcasted_iota(jnp.int32, (1, LANES), 1)
    return jnp.dot(jnp.where(lane == j, x_cols, 0.0).astype(bf16), y_rows, preferred_element_type=f32)


def _sample_mixer_kernel(c_ref, cos_ref, sin_ref, gb_ref, al_ref, wc_ref, mlnw_ref, gdnw_ref, glnw_ref,
                         w2_ref, glb_ref,
                         ret_ref, mlc_ref, mln_ref, mlm_ref, gdn_ref, conv_ref, gla_ref, *rest):
    y_ref, ret_o, mlc_o, mln_o, mlm_o, gdn_o, conv_o, gla_o = rest[-8:]
    G = SAMPLE_G
    gw = GROUP_WIDTH
    dk = GLA_KEY_DIM

    def cblk(name, width):
        return c_ref[:, C_OFF[name]:C_OFF[name] + width]

    small = cblk("small", LANES)
    pre = small + gb_ref[...]
    cosf = cos_ref[...]
    sinf = sin_ref[...]

    for h in range(HEADS):
        hs = slice(h * HEAD_DIM, (h + 1) * HEAD_DIM)
        gamma = 1.0 - 2.0 ** (-5.0 - h)
        q = _rope(cblk("ret_q", gw)[:, hs], cosf, sinf)
        k = _rope(cblk("ret_k", gw)[:, hs], cosf, sinf) * HEAD_DIM ** -0.5
        v = cblk("ret_v", gw)[:, hs]
        kt, vpad, qb = _cols(k), _rows_pad(v).astype(bf16), q.astype(bf16)
        rows = []
        for j in range(G):
            s_new = gamma * ret_ref[j, h] + _outer(kt, j, vpad)
            ret_o[j, h] = s_new
            rows.append(_row(jnp.dot(qb, s_new.astype(bf16), preferred_element_type=f32), j))
        o = jnp.concatenate(rows, axis=0)
        y = _rms_rows(o) * _silu(cblk("ret_g", gw)[:, hs])
        y_ref[:, 0 * gw + h * HEAD_DIM:0 * gw + (h + 1) * HEAD_DIM] = y.astype(y_ref.dtype)

    m_old = mlm_ref[...]
    f_al = pltpu.roll(jax.nn.log_sigmoid(pre), LANES - (L_MLF - L_MLI), 1)
    m_int = m_old + f_al
    m_t = jnp.maximum(m_int, pre)
    w_in = jnp.exp(pre - m_t)
    g_in = jnp.exp(m_int - m_t)
    e_neg = jnp.exp(-m_t)
    mlm_o[...] = m_t
    for h in range(HEADS):
        hs = slice(h * HEAD_DIM, (h + 1) * HEAD_DIM)
        q = cblk("ml_q", gw)[:, hs]
        k = cblk("ml_k", gw)[:, hs] * HEAD_DIM ** -0.5
        v = cblk("ml_v", gw)[:, hs]
        kw = k * w_in[:, h:h + 1]
        kt, vpad, qb = _cols(kw), _rows_pad(v).astype(bf16), q.astype(bf16)
        rows = []
        for j in range(G):
            gj = g_in[j:j + 1, h:h + 1]
            c_new = gj * mlc_ref[j, h] + _outer(kt, j, vpad)
            mlc_o[j, h] = c_new
            n_new = gj * mln_ref[j, h:h + 1, :] + kw[j:j + 1, :]
            mln_o[j, h:h + 1, :] = n_new
            num = _row(jnp.dot(qb, c_new.astype(bf16), preferred_element_type=f32), j)
            den = jnp.sum(q[j:j + 1, :] * n_new, axis=-1, keepdims=True)
            rows.append(num / jnp.maximum(jnp.abs(den), e_neg[j:j + 1, h:h + 1]))
        hh = jnp.concatenate(rows, axis=0)
        y = _rms_rows(hh) * mlnw_ref[:, hs] * jax.nn.sigmoid(cblk("ml_o", gw)[:, hs])
        y_ref[:, 1 * gw + h * HEAD_DIM:1 * gw + (h + 1) * HEAD_DIM] = y.astype(y_ref.dtype)

    beta_all = jax.nn.sigmoid(pre)
    eg_all = jnp.exp(-jnp.exp(al_ref[...]) * jax.nn.softplus(pre))
    conv = []
    for p, name in enumerate(("gd_q", "gd_k", "gd_v")):
        ps = slice(p * gw, (p + 1) * gw)
        x = cblk(name, gw)
        acc = x * wc_ref[CONV_W - 1:CONV_W, ps]
        for j in range(CONV_W - 1):
            acc = acc + conv_ref[j, :, ps] * wc_ref[j:j + 1, ps]
        conv.append(_silu(acc))
        for j in range(CONV_W - 2):
            conv_o[j, :, ps] = conv_ref[j + 1, :, ps]
        conv_o[CONV_W - 2, :, ps] = x
    for h in range(HEADS):
        hs = slice(h * HEAD_DIM, (h + 1) * HEAD_DIM)
        q = _l2n_rows(conv[0][:, hs]) * HEAD_DIM ** -0.5
        k = _l2n_rows(conv[1][:, hs])
        v = conv[2][:, hs]
        kt, kb16, qb = _cols(k), k.astype(bf16), q.astype(bf16)
        bcol = beta_all[:, L_GDB + h:L_GDB + h + 1]
        ecol = eg_all[:, L_GDA + h:L_GDA + h + 1]
        ks = jnp.concatenate(
            [_row(jnp.dot(kb16, gdn_ref[j, h].astype(bf16), preferred_element_type=f32), j) for j in range(G)],
            axis=0)
        vpad = _rows_pad(bcol * (v - ecol * ks)).astype(bf16)
        rows = []
        for j in range(G):
            s_new = ecol[j:j + 1, :] * gdn_ref[j, h] + _outer(kt, j, vpad)
            gdn_o[j, h] = s_new
            rows.append(_row(jnp.dot(qb, s_new.astype(bf16), preferred_element_type=f32), j))
        o = jnp.concatenate(rows, axis=0)
        y = _rms_rows(o) * gdnw_ref[...] * _silu(cblk("gd_z", gw)[:, hs])
        y_ref[:, 2 * gw + h * HEAD_DIM:2 * gw + (h + 1) * HEAD_DIM] = y.astype(y_ref.dtype)

    gk = jax.nn.log_sigmoid(_dot_exact(small, w2_ref[...]) + glb_ref[...]) * (1.0 / GLA_GATE_DENOM)
    eg = jnp.exp(gk)
    q_all = cblk("gl_q", HEADS * dk) * dk ** -0.5
    k_all = cblk("gl_k", HEADS * dk)
    upper = lax.broadcasted_iota(jnp.int32, (LANES, 1), 0) < dk
    lane = lax.broadcasted_iota(jnp.int32, (1, LANES), 1)
    for p in range(HEADS // 2):
        ps = slice(p * LANES, (p + 1) * LANES)
        kp, qp = k_all[:, ps], q_all[:, ps]
        kt2 = _cols(jnp.concatenate([kp, kp], axis=0))
        et = _cols(eg[:, ps])
        v0 = cblk("gl_v", gw)[:, (2 * p) * HEAD_DIM:(2 * p + 1) * HEAD_DIM]
        v1 = cblk("gl_v", gw)[:, (2 * p + 1) * HEAD_DIM:(2 * p + 2) * HEAD_DIM]
        vpad = _rows_pad(jnp.concatenate([v0, v1], axis=0)).astype(bf16)
        q2 = jnp.concatenate([jnp.where(lane < dk, qp, 0.0), jnp.where(lane < dk, 0.0, qp)], axis=0).astype(bf16)
        rows0, rows1 = [], []
        for j in range(G):
            keep = lane == jnp.where(upper, j, G + j)
            u = jnp.dot(jnp.where(keep, kt2, 0.0).astype(bf16), vpad, preferred_element_type=f32)
            s_new = et[:, j:j + 1] * gla_ref[j, p] + u
            gla_o[j, p] = s_new
            res = jnp.dot(q2, s_new.astype(bf16), preferred_element_type=f32)
            rows0.append(_row(res, j))
            rows1.append(_row(res, G + j))
        for hh, rows in enumerate((rows0, rows1)):
            h = 2 * p + hh
            hs = slice(h * HEAD_DIM, (h + 1) * HEAD_DIM)
            o = jnp.concatenate(rows, axis=0)
            y = _rms_rows(o) * glnw_ref[...] * _silu(cblk("gl_z", gw)[:, hs])
            y_ref[:, 3 * gw + h * HEAD_DIM:3 * gw + (h + 1) * HEAD_DIM] = y.astype(y_ref.dtype)


def _sample_mixers(c_all, lp, st, y_prev, st_prev, *, layer, depth, row0, nb):
    G = SAMPLE_G
    gw = GROUP_WIDTH
    kw = HEADS * GLA_KEY_DIM
    assert row0 % G == 0 and nb % G == 0
    blk0 = row0 // G

    def const2(shape):
        return pl.BlockSpec(shape, lambda i: (0, 0))

    def slab(shape, batch_axis=0):
        def index(i):
            idx = [0] * len(shape)
            idx[batch_axis] = i
            return (layer,) + tuple(idx)
        return pl.BlockSpec((None,) + shape, index)

    state_specs = [slab((G, HEADS, HEAD_DIM, HEAD_DIM)), slab((G, HEADS, HEAD_DIM, HEAD_DIM)),
                   slab((G, HEADS, HEAD_DIM)), slab((G, LANES)), slab((G, HEADS, HEAD_DIM, HEAD_DIM)),
                   slab((CONV_W - 1, G, 3 * gw), batch_axis=1), slab((G, HEADS // 2, LANES, HEAD_DIM))]
    order = ("ret", "ml_C", "ml_n", "ml_m", "gdn", "gdn_conv", "gla")
    states = [st[n] for n in order]
    y_spec = pl.BlockSpec((G, D_MODEL), lambda i: (blk0 + i, 0))

    args = [c_all, lp["cos_s"], lp["sin_s"], lp["gate_bias"], lp["alog_row"], lp["gd_conv"], lp["ml_norm"],
            lp["gd_norm"], lp["gl_norm"], lp["gl_w2p"], lp["gl_b"]] + states
    in_specs = [pl.BlockSpec((G, C_WIDTH), lambda i: (blk0 + i, 0)),
                const2((1, HEAD_DIM)), const2((1, HEAD_DIM)), const2((1, LANES)), const2((1, LANES)),
                const2((CONV_W, 3 * gw)), const2((1, gw)), const2((1, HEAD_DIM)), const2((1, HEAD_DIM)),
                const2((LANES, kw)), const2((1, kw))] + state_specs
    inplace = [y_prev] + ([st_prev[n] for n in order] if st_prev is not None else [])
    aliases = {len(args) + k: k for k in range(len(inplace))}
    args += inplace
    in_specs += [pl.BlockSpec(memory_space=pl.ANY)] * len(inplace)

    outs = pl.pallas_call(
        _sample_mixer_kernel,
        grid=(nb // G,),
        in_specs=in_specs,
        out_specs=[y_spec] + state_specs,
        out_shape=[jax.ShapeDtypeStruct(y_prev.shape, y_prev.dtype)] + [
            jax.ShapeDtypeStruct(s.shape, s.dtype) for s in states],
        input_output_aliases=aliases,
        compiler_params=_cparams(("parallel",)),
        name="sample_mixers",
    )(*args)
    return outs[0], dict(zip(order, outs[1:]))


def _rope_tables(pos):
    half = HEAD_DIM // 2
    inv = ROPE_BASE ** (-jnp.arange(half, dtype=f32) / half)
    ang = pos[:, None] * inv[None, :]
    cos, sin = jnp.cos(ang), jnp.sin(ang)
    return jnp.concatenate([cos, cos], axis=-1), jnp.concatenate([-sin, sin], axis=-1)


def _permute_w_in(w):
    big = jnp.concatenate([w[..., 0:4096], w[..., 4104:6152], w[..., 6160:7696]], axis=-1)
    small = jnp.concatenate([w[..., 4096:4104], w[..., 6152:6160], w[..., 7696:7712]], axis=-1)
    pad = jnp.zeros(w.shape[:-1] + (C_WIDTH - big.shape[-1] - small.shape[-1],), w.dtype)
    return jnp.concatenate([big, small, pad], axis=-1)


W_IN_RAW = 7712
W_IN_TILE = 512
W_IN_SHIFTS = ((8, 8, 12), (16, 12, 15))
W_IN_TAIL = 7680
W_IN_GATE_TILES = (4096 // LANES, 6144 // LANES)


def _w_in_prep_kernel(a_ref, b_ref, t_ref, c0_ref, c1_ref, o_ref):
    j = pl.program_id(1)
    rows = a_ref.shape[0]
    rb = 256
    last = W_IN_SHIFTS[-1][2] - 1

    def shifted(shift, nxt_ref):
        for r0 in range(0, rows, rb):
            x = jnp.concatenate([a_ref[r0:r0 + rb, :], nxt_ref[r0:r0 + rb, :]], axis=1)
            o_ref[r0:r0 + rb, :] = pltpu.roll(x, x.shape[1] - shift, 1)[:, :W_IN_TILE].astype(bf16)

    @pl.when(j < W_IN_SHIFTS[0][1])
    def _():
        o_ref[...] = a_ref[...].astype(bf16)

    for shift, lo, hi in W_IN_SHIFTS:
        @pl.when(jnp.logical_and(j >= lo, j < min(hi, last)))
        def _():
            shifted(shift, b_ref)

    @pl.when(j == last)
    def _():
        shifted(W_IN_SHIFTS[-1][0], t_ref)

    @pl.when(j > last)
    def _():
        lane = lax.broadcasted_iota(jnp.int32, (1, LANES), 1)
        gates = jnp.where(lane < L_GDB, c0_ref[...],
                          jnp.where(lane < L_GLR, c1_ref[...],
                                    jnp.where(lane < L_GLR + GLA_RANK, t_ref[...], 0.0)))
        o_ref[:, :LANES] = gates.astype(bf16)
        o_ref[:, LANES:] = jnp.zeros((rows, W_IN_TILE - LANES), bf16)


def _w_in_prep(w_in):
    depth, d, n = w_in.shape
    assert n == W_IN_RAW and C_OFF["small"] == W_IN_SHIFTS[-1][2] * W_IN_TILE
    tail = jnp.pad(w_in[:, :, W_IN_TAIL:], ((0, 0), (0, 0), (0, LANES - (n - W_IN_TAIL))))
    ratio = W_IN_TILE // LANES
    last_wide = W_IN_TAIL // W_IN_TILE - 1
    last_lane_tile = W_IN_TAIL // LANES - 1
    g0, g1 = W_IN_GATE_TILES

    def lane_tile(index):
        return pl.BlockSpec((None, d, LANES), index)

    return pl.pallas_call(
        _w_in_prep_kernel,
        grid=(depth, C_WIDTH // W_IN_TILE),
        in_specs=[pl.BlockSpec((None, d, W_IN_TILE), lambda l, j: (l, 0, jnp.minimum(j, last_wide))),
                  lane_tile(lambda l, j: (l, 0, jnp.minimum(ratio * (j + 1), last_lane_tile))),
                  lane_tile(lambda l, j: (l, 0, 0)),
                  lane_tile(lambda l, j: (l, 0, g0)),
                  lane_tile(lambda l, j: (l, 0, g1))],
        out_specs=pl.BlockSpec((None, d, W_IN_TILE), lambda l, j: (l, 0, j)),
        out_shape=jax.ShapeDtypeStruct((depth, d, C_WIDTH), bf16),
        compiler_params=_cparams(("parallel", "arbitrary")),
        name="w_in_prep",
    )(w_in, w_in, tail, w_in, w_in)


def _lane_row(pieces):
    row = jnp.zeros((1, LANES), f32)
    for off, val in pieces:
        row = lax.dynamic_update_slice(row, val.reshape(1, -1).astype(f32), (0, off))
    return row


def kernel(x_prompt, x_sample, state_ret, state_mlstm_C, state_mlstm_n, state_mlstm_m, state_gdn, state_gdn_conv, state_gla, cache_mem_k, cache_mem_v, mem_prompt, norm_mix_pre, norm_mix_post, w_in, ml_ib, ml_fb, ml_norm, gd_conv, gd_A_log, gd_dt_bias, gd_norm, gl_w2, gl_b, gl_norm, w_out, norm_x_pre, norm_x_post, norm_mem, w_xq, w_xk, w_xv, w_xo, norm_mlp_pre, norm_mlp_post, w_up, w_down):
    bp, tp, d = x_prompt.shape
    bs, ts, _ = x_sample.shape
    depth = w_in.shape[0]
    assert ts == 1 and d == D_MODEL and tp % MIX_TB == 0
    rows_p = bp * tp
    rows = rows_p + bs
    assert rows % ROW_TILE == 0 and rows_p % SAMPLE_G == 0

    x, h = _embed(x_prompt.reshape(rows_p, d), x_sample.reshape(bs, d), norm_mix_pre[0], tm=LANES)
    w_in_b = _w_in_prep(w_in)
    w_out_b, w_xo_b, w_down_b = w_out.astype(bf16), w_xo.astype(bf16), w_down.astype(bf16)
    cos_p, sin_p = _rope_tables(jnp.arange(tp, dtype=f32))
    cos_s, sin_s = _rope_tables(jnp.arange(ts, dtype=f32) + PAST_LEN)
    conv_zero = jnp.zeros((bp, CONV_W - 1, 3 * GROUP_WIDTH), f32)
    mem2d = mem_prompt.reshape(bp * MEM_LEN, d)
    st_in = {"ret": state_ret, "ml_C": state_mlstm_C, "ml_n": state_mlstm_n,
             "ml_m": jnp.pad(state_mlstm_m, ((0, 0), (0, 0), (0, LANES - HEADS))),
             "gdn": state_gdn, "gdn_conv": jnp.transpose(state_gdn_conv, (0, 2, 1, 3)),
             "gla": state_gla.reshape(depth, bs, HEADS // 2, LANES, HEAD_DIM)}

    new_p = {n: [] for n in ("ret", "ml_C", "ml_n", "ml_m", "gdn", "gdn_conv", "gla")}
    st_s = None
    mk_p = mv_p = None
    for l in range(depth):
        lp = {
            "cos_p": cos_p, "sin_p": sin_p, "cos_s": cos_s, "sin_s": sin_s,
            "gate_bias": _lane_row([(L_MLI, ml_ib[l]), (L_MLF, ml_fb[l]), (L_GDA, gd_dt_bias[l])]),
            "alog_row": _lane_row([(L_GDA, gd_A_log[l])]),
            "ml_norm": ml_norm[l].reshape(1, GROUP_WIDTH),
            "gd_norm": gd_norm[l].reshape(1, HEAD_DIM),
            "gl_norm": gl_norm[l].reshape(1, HEAD_DIM),
            "gd_conv": gd_conv[l],
            "gl_w2p": jnp.zeros((LANES, HEADS * GLA_KEY_DIM), f32).at[L_GLR:L_GLR + GLA_RANK].set(gl_w2[l]),
            "gl_b": gl_b[l].reshape(1, HEADS * GLA_KEY_DIM),
        }
        mk_p = _norm_matmul_slab(mem2d, norm_mem[l], w_xk, mk_p, layer=l, tm=512, tn=1024, name="mem_k")
        mv_p = _norm_matmul_slab(mem2d, norm_mem[l], w_xv, mv_p, layer=l, tm=512, tn=1024, name="mem_v")

        c_all = _matmul(h, w_in_b, layer=l, tm=ROW_TILE, tn=1024, out_dtype=f32, name="w_in")
        y_all, st_p = _prompt_mixers(c_all, lp, batch=bp, seq=tp, conv_zero=conv_zero)
        y_all, st_s = _sample_mixers(c_all, lp, st_in, y_all, st_s, layer=l, depth=depth, row0=rows_p, nb=bs)
        x, h = _matmul_norm_res(y_all, w_out_b, norm_mix_post[l], x, norm_x_pre[l], layer=l,
                                tm=ROW_TILE, tk=D_MODEL, name="w_out")

        q_all = _matmul(h, w_xq, layer=l, tm=ROW_TILE, tn=1024, out_dtype=bf16, name="w_xq")
        o_all = _xattn_prompt(q_all, mk_p, mv_p, layer=l, batch=bp, seq=tp, tq=512)
        o_all = _xattn_sample(q_all, cache_mem_k, cache_mem_v, o_all, layer=l, row0=rows_p, nb=bs)
        x, h = _matmul_norm_res(o_all, w_xo_b, norm_x_post[l], x, norm_mlp_pre[l], layer=l,
                                tm=ROW_TILE, tk=D_MODEL, name="w_xo")

        u = _matmul(h, w_up, layer=l, tm=ROW_TILE, tn=1024, out_dtype=bf16, act="relu2", name="w_up")
        g_next = norm_mix_pre[l + 1] if l + 1 < depth else None
        x, h = _matmul_norm_res(u, w_down_b, norm_mlp_post[l], x, g_next, layer=l,
                                tm=ROW_TILE, tk=1024, name="w_down")

        for n in new_p:
            new_p[n].append(st_p[n])

    def stk(lst):
        return jnp.stack(lst, axis=0)

    xp = x[:rows_p].reshape(bp, tp, d)
    xs = x[rows_p:].reshape(bs, ts, d)
    return (xp, xs,
            stk(new_p["ret"]), stk(new_p["ml_C"]), stk(new_p["ml_n"]), stk(new_p["ml_m"]),
            stk(new_p["gdn"]), stk(new_p["gdn_conv"]), stk(new_p["gla"]),
            mk_p.reshape(depth, bp, MEM_LEN, d), mv_p.reshape(depth, bp, MEM_LEN, d),
            st_s["ret"], st_s["ml_C"], st_s["ml_n"], st_s["ml_m"][:, :, :HEADS],
            st_s["gdn"], jnp.transpose(st_s["gdn_conv"], (0, 2, 1, 3)),
            st_s["gla"].reshape(depth, bs, HEADS, GLA_KEY_DIM, HEAD_DIM))
```

```python
import functools
import math

import jax
import jax.numpy as jnp
from jax import lax
from jax.experimental import pallas as pl
from jax.experimental.pallas import tpu as pltpu

f32 = jnp.float32
bf16 = jnp.bfloat16

D_MODEL = 2048
HEADS = 4
HEAD_DIM = 128
GROUP_WIDTH = 512
GLA_KEY_DIM = 64
GLA_RANK = 16
GLA_GATE_DENOM = 16.0
CONV_W = 4
CHUNK = 64
ROPE_BASE = 10000.0
X_HEADS = 4
X_HEAD_DIM = 512
MEM_LEN = 256
D_FF = 8192
EPS = 1e-6
PAST_LEN = 16384
NEG_INF = float("-inf")

LANES = 128
SUBLANES = 8

C_OFF = {
    "ret_q": 0, "ret_k": 512, "ret_v": 1024, "ret_g": 1536,
    "ml_q": 2048, "ml_k": 2560, "ml_v": 3072, "ml_o": 3584,
    "gd_q": 4096, "gd_k": 4608, "gd_v": 5120, "gd_z": 5632,
    "gl_q": 6144, "gl_k": 6400, "gl_v": 6656, "gl_z": 7168,
    "small": 7680,
}
C_WIDTH = 8192
L_MLI, L_MLF, L_GDB, L_GDA, L_GLR = 0, 4, 8, 12, 16

ROW_TILE = 640
MIX_TB = 256
SAMPLE_G = 8
GLA_SUB = 16
XATTN_S = 2
VMEM_LIMIT = 56 * 1024 * 1024


def _cparams(sem):
    return pltpu.CompilerParams(dimension_semantics=sem, vmem_limit_bytes=VMEM_LIMIT)


def _dot(a, b):
    return jnp.dot(a.astype(bf16), b.astype(bf16), preferred_element_type=f32)


def _dot_nt(a, b):
    return lax.dot_general(a.astype(bf16), b.astype(bf16), (((1,), (1,)), ((), ())),
                           preferred_element_type=f32)


def _dot_tn(a, b):
    return lax.dot_general(a.astype(bf16), b.astype(bf16), (((0,), (0,)), ((), ())),
                           preferred_element_type=f32)


def _dot_exact(a, b):
    return jnp.dot(a, b, preferred_element_type=f32, precision=lax.Precision.HIGHEST)


def _mask_dot_f32(mask, x):
    m = jnp.where(mask, 1.0, 0.0).astype(bf16)
    hi = x.astype(bf16)
    r1 = x - hi.astype(f32)
    mid = r1.astype(bf16)
    lo = (r1 - mid.astype(f32)).astype(bf16)
    return (jnp.dot(m, hi, preferred_element_type=f32) + jnp.dot(m, mid, preferred_element_type=f32)
            + jnp.dot(m, lo, preferred_element_type=f32))


def _rms_rows(x):
    return x * lax.rsqrt(jnp.mean(x * x, axis=-1, keepdims=True) + EPS)


def _l2n_rows(x):
    return x * lax.rsqrt(jnp.sum(x * x, axis=-1, keepdims=True) + EPS)


def _silu(x):
    return x * jax.nn.sigmoid(x)


def _rope(x, cosf, sinf):
    return x * cosf + pltpu.roll(x, HEAD_DIM // 2, 1) * sinf


def _chunk_masks(tb):
    r = lax.broadcasted_iota(jnp.int32, (tb, tb), 0)
    c = lax.broadcasted_iota(jnp.int32, (tb, tb), 1)
    same = (r // CHUNK) == (c // CHUNK)
    tril = jnp.logical_and(same, r >= c)
    strict = jnp.logical_and(same, r > c)
    return r, c, same, tril, strict


def _rms_gain(x, g):
    return x * lax.rsqrt(jnp.mean(x * x, axis=-1, keepdims=True) + EPS) * g


def _embed_kernel(x_in_ref, g_ref, *rest):
    x_ref, h_ref = rest[-2:]
    x = x_in_ref[...]
    x_ref[...] = x
    h_ref[...] = _rms_gain(x, g_ref[...]).astype(bf16)


def _embed(x2d, g, prev, *, row0, rows, tm):
    m, d = x2d.shape
    assert m % tm == 0 and row0 % tm == 0
    blk0 = row0 // tm
    args = [x2d, g.reshape(1, d)] + (list(prev) if prev is not None else [])
    out_spec = pl.BlockSpec((tm, d), lambda i: (blk0 + i, 0))
    return pl.pallas_call(
        _embed_kernel,
        grid=(m // tm,),
        in_specs=[pl.BlockSpec((tm, d), lambda i: (i, 0)), pl.BlockSpec((1, d), lambda i: (0, 0))] + (
            [pl.BlockSpec(memory_space=pl.ANY)] * 2 if prev is not None else []),
        out_specs=[out_spec, out_spec],
        out_shape=[jax.ShapeDtypeStruct((rows, d), f32), jax.ShapeDtypeStruct((rows, d), bf16)],
        input_output_aliases={2: 0, 3: 1} if prev is not None else {},
        compiler_params=_cparams(("parallel",)),
        name="embed",
    )(*args)


def _matmul_kernel(h_ref, w_ref, *rest, act, cast_w):
    if cast_w:
        o_ref, wb_ref = rest

        @pl.when(pl.program_id(1) == 0)
        def _():
            wb_ref[...] = w_ref[...].astype(bf16)

        w = wb_ref[...]
    else:
        (o_ref,) = rest
        w = w_ref[...]
    y = jnp.dot(h_ref[...], w, preferred_element_type=f32)
    if act == "relu2":
        y = jnp.square(jnp.maximum(y, 0.0))
    o_ref[...] = y.astype(o_ref.dtype)


def _matmul(h, w, *, layer, tm, tn, out_dtype, name, act=None):
    m, k = h.shape
    n = w.shape[2]
    assert m % tm == 0 and n % tn == 0
    cast_w = w.dtype != bf16
    return pl.pallas_call(
        functools.partial(_matmul_kernel, act=act, cast_w=cast_w),
        grid=(n // tn, m // tm),
        in_specs=[pl.BlockSpec((tm, k), lambda j, i: (i, 0)),
                  pl.BlockSpec((None, k, tn), lambda j, i: (layer, 0, j))],
        out_specs=pl.BlockSpec((tm, tn), lambda j, i: (i, j)),
        out_shape=jax.ShapeDtypeStruct((m, n), out_dtype),
        scratch_shapes=[pltpu.VMEM((k, tn), bf16)] if cast_w else [],
        compiler_params=_cparams(("parallel", "arbitrary")),
        name=name,
    )(h, w)


def _norm_matmul_kernel(x_ref, g_ref, w_ref, *rest):
    o_ref, h_ref = rest[-2:]

    @pl.when(pl.program_id(1) == 0)
    def _():
        h_ref[...] = _rms_gain(x_ref[...], g_ref[...]).astype(bf16)

    o_ref[...] = jnp.dot(h_ref[...], w_ref[...].astype(bf16), preferred_element_type=f32)


def _norm_matmul_slab(x, g, w, prev, *, layer, tm, tn, name):
    m, k = x.shape
    depth, _, n = w.shape
    assert m % tm == 0 and n % tn == 0
    in_specs = [pl.BlockSpec((tm, k), lambda i, j: (i, 0)),
                pl.BlockSpec((1, k), lambda i, j: (0, 0)),
                pl.BlockSpec((None, k, tn), lambda i, j: (layer, 0, j))]
    args = [x, g.reshape(1, k), w]
    aliases = {}
    if prev is not None:
        in_specs.append(pl.BlockSpec(memory_space=pl.ANY))
        args.append(prev)
        aliases = {len(args) - 1: 0}
    return pl.pallas_call(
        _norm_matmul_kernel,
        grid=(m // tm, n // tn),
        in_specs=in_specs,
        out_specs=pl.BlockSpec((None, tm, tn), lambda i, j: (layer, i, j)),
        out_shape=jax.ShapeDtypeStruct((depth, m, n), f32),
        scratch_shapes=[pltpu.VMEM((tm, k), bf16)],
        input_output_aliases=aliases,
        compiler_params=_cparams(("parallel", "arbitrary")),
        name=name,
    )(*args)


def _matmul_norm_res_kernel(a_ref, w_ref, g_ref, r_ref, *rest, next_norm, single_k):
    rest = list(rest)
    gn_ref = rest.pop(0) if next_norm else None
    o_ref = rest.pop(0)
    hn_ref = rest.pop(0) if next_norm else None

    def finish(y):
        x_new = r_ref[...] + _rms_gain(y, g_ref[...])
        o_ref[...] = x_new
        if next_norm:
            hn_ref[...] = _rms_gain(x_new, gn_ref[...]).astype(bf16)

    if single_k:
        finish(jnp.dot(a_ref[...], w_ref[...], preferred_element_type=f32))
        return

    (acc_ref,) = rest
    kk = pl.program_id(1)

    @pl.when(kk == 0)
    def _():
        acc_ref[...] = jnp.zeros_like(acc_ref)

    acc_ref[...] += jnp.dot(a_ref[...], w_ref[...], preferred_element_type=f32)

    @pl.when(kk == pl.num_programs(1) - 1)
    def _():
        finish(acc_ref[...])


def _matmul_norm_res(a, w, g, res, g_next, *, layer, tm, tk, name):
    m, k = a.shape
    n = w.shape[2]
    assert m % tm == 0 and k % tk == 0
    next_norm = g_next is not None
    row = pl.BlockSpec((tm, n), lambda i, j: (i, 0))
    vec = pl.BlockSpec((1, n), lambda i, j: (0, 0))
    in_specs = [pl.BlockSpec((tm, tk), lambda i, j: (i, j)),
                pl.BlockSpec((None, tk, n), lambda i, j: (layer, j, 0)), vec, row]
    args = [a, w, g.reshape(1, n), res]
    out_specs = [row]
    out_shape = [jax.ShapeDtypeStruct((m, n), f32)]
    if next_norm:
        in_specs.append(vec)
        args.append(g_next.reshape(1, n))
        out_specs.append(row)
        out_shape.append(jax.ShapeDtypeStruct((m, n), bf16))
    single_k = k == tk
    outs = pl.pallas_call(
        functools.partial(_matmul_norm_res_kernel, next_norm=next_norm, single_k=single_k),
        grid=(m // tm, k // tk),
        in_specs=in_specs,
        out_specs=out_specs,
        out_shape=out_shape,
        scratch_shapes=[] if single_k else [pltpu.VMEM((tm, n), f32)],
        compiler_params=_cparams(("parallel", "arbitrary")),
        name=name,
    )(*args)
    return (outs[0], outs[1]) if next_norm else (outs[0], None)


def _xattn_prompt_kernel(q_ref, k_ref, v_ref, o_ref, kb_ref, vb_ref):
    @pl.when(pl.program_id(1) == 0)
    def _():
        kb_ref[...] = k_ref[...].astype(bf16)
        vb_ref[...] = v_ref[...].astype(bf16)

    scale = X_HEAD_DIM ** -0.5
    for h in range(X_HEADS):
        sl = slice(h * X_HEAD_DIM, (h + 1) * X_HEAD_DIM)
        s = lax.dot_general(q_ref[:, sl], kb_ref[:, sl], (((1,), (1,)), ((), ())),
                            preferred_element_type=f32) * scale
        m = jnp.max(s, axis=-1, keepdims=True)
        p = jnp.exp(s - m)
        l = jnp.sum(p, axis=-1, keepdims=True)
        o = jnp.dot(p.astype(bf16), vb_ref[:, sl], preferred_element_type=f32) / l
        o_ref[:, sl] = o.astype(o_ref.dtype)


def _xattn_prompt(q_all, mk, mv, *, layer, batch, seq, tq):
    nt = seq // tq
    return pl.pallas_call(
        _xattn_prompt_kernel,
        grid=(batch, nt),
        in_specs=[pl.BlockSpec((tq, D_MODEL), lambda b, t: (b * nt + t, 0)),
                  pl.BlockSpec((None, MEM_LEN, D_MODEL), lambda b, t: (layer, b, 0)),
                  pl.BlockSpec((None, MEM_LEN, D_MODEL), lambda b, t: (layer, b, 0))],
        out_specs=pl.BlockSpec((tq, D_MODEL), lambda b, t: (b * nt + t, 0)),
        out_shape=jax.ShapeDtypeStruct(q_all.shape, bf16),
        scratch_shapes=[pltpu.VMEM((MEM_LEN, D_MODEL), bf16), pltpu.VMEM((MEM_LEN, D_MODEL), bf16)],
        compiler_params=_cparams(("parallel", "arbitrary")),
        name="xattn_prompt",
    )(q_all, mk, mv)


def _xattn_sample_kernel(q_ref, k_ref, v_ref, prev_ref, o_ref, acc_ref):
    del prev_ref
    j = pl.program_id(1)
    scale = X_HEAD_DIM ** -0.5
    rows = lax.broadcasted_iota(jnp.int32, (SAMPLE_G, 1), 0)
    q_all = q_ref[...].astype(f32)
    for s_i in range(XATTN_S):
        row = j * XATTN_S + s_i
        q = jnp.sum(jnp.where(rows == row, q_all, 0.0), axis=0, keepdims=True)
        for h in range(X_HEADS):
            sl = slice(h * X_HEAD_DIM, (h + 1) * X_HEAD_DIM)
            s = jnp.sum(k_ref[s_i, :, sl] * q[:, sl], axis=-1, keepdims=True) * scale
            m = jnp.max(s, axis=0, keepdims=True)
            p = jnp.exp(s - m)
            l = jnp.sum(p, axis=0, keepdims=True)
            acc_ref[pl.ds(row, 1), sl] = jnp.sum(p * v_ref[s_i, :, sl], axis=0, keepdims=True) / l

    @pl.when(j == pl.num_programs(1) - 1)
    def _():
        o_ref[...] = acc_ref[...].astype(o_ref.dtype)


def _xattn_sample(q_all, ck, cv, o_prev, *, layer, row0, nb):
    G = SAMPLE_G
    blk0 = row0 // G
    per_g = G // XATTN_S
    kv_spec = pl.BlockSpec((None, XATTN_S, MEM_LEN, D_MODEL), lambda g, j: (layer, g * per_g + j, 0, 0))
    return pl.pallas_call(
        _xattn_sample_kernel,
        grid=(nb // G, per_g),
        in_specs=[pl.BlockSpec((G, D_MODEL), lambda g, j: (blk0 + g, 0)), kv_spec, kv_spec,
                  pl.BlockSpec(memory_space=pl.ANY)],
        out_specs=pl.BlockSpec((G, D_MODEL), lambda g, j: (blk0 + g, 0)),
        out_shape=jax.ShapeDtypeStruct(o_prev.shape, o_prev.dtype),
        scratch_shapes=[pltpu.VMEM((G, D_MODEL), f32)],
        input_output_aliases={3: 0},
        compiler_params=_cparams(("parallel", "arbitrary")),
        name="xattn_sample",
    )(q_all, ck, cv, o_prev)


def _ret_prompt_body(q_ref, k_ref, v_ref, g_ref, cos_ref, sin_ref, y_ref, s_out, s_s, *, tb, y_off):
    t = pl.program_id(1)
    nc = tb // CHUNK

    @pl.when(t == 0)
    def _():
        s_s[...] = jnp.zeros_like(s_s)

    cosf = cos_ref[...]
    sinf = sin_ref[...]
    r, c, _, tril, _ = _chunk_masks(tb)
    diff = (r - c).astype(f32)
    ridx = (lax.broadcasted_iota(jnp.int32, (tb, 1), 0) % CHUNK).astype(f32)
    for h in range(HEADS):
        hs = slice(h * HEAD_DIM, (h + 1) * HEAD_DIM)
        lg = math.log(1.0 - 2.0 ** (-5.0 - h))
        decay = jnp.where(tril, jnp.exp(jnp.maximum(diff, 0.0) * lg), 0.0)
        qr = _rope(q_ref[:, hs], cosf, sinf)
        kr = _rope(k_ref[:, hs], cosf, sinf) * HEAD_DIM ** -0.5
        vb = v_ref[:, hs].astype(bf16)
        s = _dot_nt(qr, kr) * decay
        o_in = _dot(s, vb)
        qd = (qr * jnp.exp((ridx + 1.0) * lg)).astype(bf16)
        kd = (kr * jnp.exp((CHUNK - 1.0 - ridx) * lg)).astype(bf16)
        cdec = math.exp(CHUNK * lg)
        st = s_s[h]
        outs = []
        for cc in range(nc):
            sl = slice(cc * CHUNK, (cc + 1) * CHUNK)
            outs.append(o_in[sl] + _dot(qd[sl], st))
            st = st * cdec + _dot_tn(kd[sl], vb[sl])
        s_s[h] = st
        o = jnp.concatenate(outs, axis=0)
        y = _rms_rows(o) * _silu(g_ref[:, hs])
        y_ref[:, y_off + h * HEAD_DIM:y_off + (h + 1) * HEAD_DIM] = y.astype(y_ref.dtype)

    @pl.when(t == pl.num_programs(1) - 1)
    def _():
        s_out[0] = s_s[...]


def _mlstm_prompt_body(q_ref, k_ref, v_ref, og_ref, sm_ref, gb_ref, nw_ref,
                       y_ref, c_out, n_out, m_out, c_s, n_s, m_s, *, tb, y_off):
    t = pl.program_id(1)
    nc = tb // CHUNK

    @pl.when(t == 0)
    def _():
        c_s[...] = jnp.zeros_like(c_s)
        n_s[...] = jnp.zeros_like(n_s)
        m_s[...] = jnp.zeros_like(m_s)

    _, _, _, tril, _ = _chunk_masks(tb)
    pre = sm_ref[...] + gb_ref[...]
    fall = _mask_dot_f32(tril, jax.nn.log_sigmoid(pre))
    imf_t = (pre - pltpu.roll(fall, LANES - (L_MLF - L_MLI), 1)).T
    for h in range(HEADS):
        hs = slice(h * HEAD_DIM, (h + 1) * HEAD_DIM)
        fcol = fall[:, L_MLF + h:L_MLF + h + 1]
        icol = pre[:, L_MLI + h:L_MLI + h + 1]
        dm = jnp.where(tril, fcol + imf_t[h:h + 1, :], NEG_INF)
        dmax = jnp.max(dm, axis=-1, keepdims=True)
        mp = m_s[h:h + 1, 0:1]
        m_prev, m_int, m_cur = [], [], []
        for cc in range(nc):
            sl = slice(cc * CHUNK, (cc + 1) * CHUNK)
            mi = mp + fcol[sl]
            mt = jnp.maximum(mi, dmax[sl])
            m_prev.append(mp)
            m_int.append(mi)
            m_cur.append(mt)
            mp = mt[CHUNK - 1:CHUNK, :]
        m_t = jnp.concatenate(m_cur, axis=0)
        w = jnp.exp(dm - m_t)
        qh = q_ref[:, hs]
        kh = k_ref[:, hs] * HEAD_DIM ** -0.5
        vb = v_ref[:, hs].astype(bf16)
        s = _dot_nt(qh, kh) * w
        num_in = _dot(s, vb)
        den_in = jnp.sum(s, axis=-1, keepdims=True)
        cst = c_s[h]
        nst = n_s[h:h + 1, :]
        outs = []
        for cc in range(nc):
            sl = slice(cc * CHUNK, (cc + 1) * CHUNK)
            mi, mt = m_int[cc], m_cur[cc]
            gi = jnp.exp(mi - mt)
            qc = qh[sl]
            num = num_in[sl] + gi * _dot(qc, cst)
            den = den_in[sl] + gi * jnp.sum(qc * nst, axis=-1, keepdims=True)
            outs.append(num / jnp.maximum(jnp.abs(den), jnp.exp(-mt)))
            m_new = mt[CHUNK - 1:CHUNK, :]
            fl = fcol[sl][CHUNK - 1:CHUNK, :]
            wk = jnp.exp(fl - fcol[sl] + icol[sl] - m_new)
            dec = jnp.exp(m_prev[cc] + fl - m_new)
            kw = kh[sl] * wk
            cst = dec * cst + _dot_tn(kw, vb[sl])
            nst = dec * nst + jnp.sum(kw, axis=0, keepdims=True)
        c_s[h] = cst
        n_s[h:h + 1, :] = nst
        m_s[h:h + 1, :] = jnp.broadcast_to(mp, (1, LANES))
        hh = jnp.concatenate(outs, axis=0)
        y = _rms_rows(hh) * nw_ref[:, hs] * jax.nn.sigmoid(og_ref[:, hs])
        y_ref[:, y_off + h * HEAD_DIM:y_off + (h + 1) * HEAD_DIM] = y.astype(y_ref.dtype)

    @pl.when(t == pl.num_programs(1) - 1)
    def _():
        c_out[0] = c_s[...]
        n_out[0] = n_s[0:HEADS, :]
        m_out[0] = m_s[...]


def _unit_lower_inverse_minus_eye(a, r, c):
    def coupling(s):
        return jnp.logical_and((r // (2 * s)) == (c // (2 * s)), (r // s) != (c // s))

    x = jnp.where(r == c, 1.0, jnp.where(coupling(1), -a, 0.0))
    s = 2
    while s < CHUNK:
        x = x - _dot(x, _dot(jnp.where(coupling(s), a, 0.0), x))
        s *= 2
    return jnp.where(r == c, 0.0, x)


def _gdn_prompt_body(q_ref, k_ref, v_ref, z_ref, sm_ref, gb_ref, al_ref, wc_ref, cs_ref, nw_ref,
                     y_ref, s_out, conv_out, s_s, xb_s, vn_s, *, tb, y_off):
    t = pl.program_id(1)
    nc = tb // CHUNK
    gw = GROUP_WIDTH
    pad = SUBLANES
    hist = CONV_W - 1

    @pl.when(t == 0)
    def _():
        s_s[...] = jnp.zeros_like(s_s)
        for p in range(3):
            xb_s[p, 0:pad, :] = jnp.zeros((pad, gw), f32)
            xb_s[p, pad - hist:pad, :] = cs_ref[0, :, p * gw:(p + 1) * gw]

    @pl.when(t > 0)
    def _():
        for p in range(3):
            xb_s[p, 0:pad, :] = xb_s[p, tb:tb + pad, :]

    xb_s[0, pad:, :] = q_ref[...]
    xb_s[1, pad:, :] = k_ref[...]
    xb_s[2, pad:, :] = v_ref[...]

    conv = []
    for p in range(3):
        acc = xb_s[p, pad - hist:pad - hist + tb, :] * wc_ref[0:1, p * gw:(p + 1) * gw]
        for j in range(1, CONV_W):
            acc = acc + xb_s[p, pad - hist + j:pad - hist + j + tb, :] * wc_ref[j:j + 1, p * gw:(p + 1) * gw]
        conv.append(_silu(acc))

    r, c, _, tril, strict = _chunk_masks(tb)
    pre = sm_ref[...] + gb_ref[...]
    beta_all = jax.nn.sigmoid(pre)
    g_all = -jnp.exp(al_ref[...]) * jax.nn.softplus(pre)
    gall = _mask_dot_f32(tril, g_all)
    gall_t = gall.T
    for h in range(HEADS):
        hs = slice(h * HEAD_DIM, (h + 1) * HEAD_DIM)
        gcol = gall[:, L_GDA + h:L_GDA + h + 1]
        grow = gall_t[L_GDA + h:L_GDA + h + 1, :]
        bcol = beta_all[:, L_GDB + h:L_GDB + h + 1]
        decay = jnp.where(tril, jnp.exp(jnp.where(tril, gcol - grow, 0.0)), 0.0)
        qh = _l2n_rows(conv[0][:, hs]) * HEAD_DIM ** -0.5
        kh = _l2n_rows(conv[1][:, hs])
        vh = conv[2][:, hs]
        kb = kh * bcol
        egc = jnp.exp(gcol)
        a = jnp.where(strict, _dot_nt(kb, kh) * decay, 0.0)
        rhs = jnp.concatenate([vh * bcol, kb * egc], axis=1)
        x = rhs + _dot(_unit_lower_inverse_minus_eye(a, r, c), rhs)
        u = x[:, :HEAD_DIM]
        w = x[:, HEAD_DIM:]
        attn = (_dot_nt(qh, kh) * decay).astype(bf16)
        qe = qh * egc
        st = s_s[h]
        vn_s[...] = jnp.zeros_like(vn_s)
        outs = []
        for cc in range(nc):
            sl = slice(cc * CHUNK, (cc + 1) * CHUNK)
            vnew = u[sl] - _dot(w[sl], st)
            vn_s[sl, :] = vnew.astype(bf16)
            outs.append(_dot(qe[sl], st) + jnp.dot(attn[sl, :], vn_s[...], preferred_element_type=f32))
            gl = gcol[sl][CHUNK - 1:CHUNK, :]
            kdec = kh[sl] * jnp.exp(gl - gcol[sl])
            st = st * jnp.exp(gl) + _dot_tn(kdec, vnew)
        s_s[h] = st
        o = jnp.concatenate(outs, axis=0)
        y = _rms_rows(o) * nw_ref[...] * _silu(z_ref[:, hs])
        y_ref[:, y_off + h * HEAD_DIM:y_off + (h + 1) * HEAD_DIM] = y.astype(y_ref.dtype)

    @pl.when(t == pl.num_programs(1) - 1)
    def _():
        s_out[0] = s_s[...]
        for p in range(3):
            conv_out[0, :, p * gw:(p + 1) * gw] = xb_s[p, tb + pad - hist:tb + pad, :]


def _gla_prompt_body(q_ref, k_ref, v_ref, z_ref, sm_ref, w2_ref, b_ref, nw_ref, seg_ref,
                     y_ref, s_out, s_s, *, tb, y_off):
    t = pl.program_id(1)
    nc = tb // CHUNK
    dk = GLA_KEY_DIM
    sub = GLA_SUB
    nsub = CHUNK // sub

    @pl.when(t == 0)
    def _():
        s_s[...] = jnp.zeros_like(s_s)

    r, c, same, tril, _ = _chunk_masks(tb)
    pre = _dot_exact(sm_ref[...], w2_ref[...]) + b_ref[...]
    gk = jax.nn.log_sigmoid(pre) * (1.0 / GLA_GATE_DENOM)
    g = _mask_dot_f32(tril, gk)
    q = q_ref[...] * dk ** -0.5
    k = k_ref[...]
    rpos = lax.broadcasted_iota(jnp.int32, (tb, 1), 0) % CHUNK
    lane = lax.broadcasted_iota(jnp.int32, (1, LANES), 1)
    lane_lo = lane < dk

    a_off = [jnp.zeros((tb, tb), f32) for _ in range(HEADS)]
    g3 = g.reshape(nc, CHUNK, HEADS * dk)
    for i in range(1, nsub):
        ri = jnp.broadcast_to(g3[:, sub * i:sub * i + 1, :], (nc, CHUNK, HEADS * dk)).reshape(tb, HEADS * dk)
        qs = q * jnp.exp(jnp.where(rpos // sub == i, g - ri, NEG_INF))
        ks = k * jnp.exp(jnp.where(rpos < sub * i, ri - g, NEG_INF))
        for h in range(HEADS):
            ps = slice((h // 2) * LANES, (h // 2 + 1) * LANES)
            keep = lane_lo if h % 2 == 0 else jnp.logical_not(lane_lo)
            a_off[h] = a_off[h] + _dot_nt(jnp.where(keep, qs[:, ps], 0.0), ks[:, ps])

    band = jnp.zeros((tb, LANES), f32)
    for j in range(sub):
        ksh = k if j == 0 else pltpu.roll(k, j, 0)
        gsh = g if j == 0 else pltpu.roll(g, j, 0)
        valid = (rpos % sub) >= j
        prod = q * ksh * jnp.exp(jnp.where(valid, g - gsh, NEG_INF))
        band = band + jnp.dot(prod.astype(bf16), seg_ref[j], preferred_element_type=f32)
    if tb > LANES:
        band = jnp.concatenate([band, jnp.zeros((tb, tb - LANES), f32)], axis=1)
    in_sub = jnp.logical_and((r // sub) == (c // sub), r >= c)
    a_diag = []
    for h in range(HEADS):
        shift = (tb - (sub - 1) - sub * h) % tb
        a_diag.append(jnp.where(in_sub, pltpu.roll(band, shift, 1, stride=1, stride_axis=0), 0.0))

    eg = jnp.exp(g)
    qe = q * eg
    g_t = g.T
    for p in range(HEADS // 2):
        ps = slice(p * LANES, (p + 1) * LANES)
        st = s_s[p]
        o_in = []
        vbs = []
        for hh in range(2):
            h = 2 * p + hh
            hs = slice(h * HEAD_DIM, (h + 1) * HEAD_DIM)
            vbs.append(v_ref[:, hs].astype(bf16))
            a = a_diag[h] + jnp.where(same, a_off[h], 0.0)
            o_in.append(_dot(a, vbs[hh]))
        outs = [[], []]
        for cc in range(nc):
            sl = slice(cc * CHUNK, (cc + 1) * CHUNK)
            last = cc * CHUNK + CHUNK - 1
            gl_row = g[last:last + 1, ps]
            gl_col = g_t[ps, last:last + 1]
            kdec = k[sl, ps] * jnp.exp(gl_row - g[sl, ps])
            upd = jnp.zeros((LANES, HEAD_DIM), f32)
            for hh in range(2):
                keep = lane_lo if hh == 0 else jnp.logical_not(lane_lo)
                outs[hh].append(o_in[hh][sl] + _dot(jnp.where(keep, qe[sl, ps], 0.0), st))
                upd = upd + _dot_tn(jnp.where(keep, kdec, 0.0), vbs[hh][sl])
            st = st * jnp.exp(gl_col) + upd
        s_s[p] = st
        for hh in range(2):
            h = 2 * p + hh
            hs = slice(h * HEAD_DIM, (h + 1) * HEAD_DIM)
            o = jnp.concatenate(outs[hh], axis=0)
            y = _rms_rows(o) * nw_ref[...] * _silu(z_ref[:, hs])
            y_ref[:, y_off + h * HEAD_DIM:y_off + (h + 1) * HEAD_DIM] = y.astype(y_ref.dtype)

    @pl.when(t == pl.num_programs(1) - 1)
    def _():
        s_out[0] = s_s[...]


def _gla_segment_table():
    d = jnp.arange(HEADS * GLA_KEY_DIM)[None, :, None]
    p = jnp.arange(LANES)[None, None, :]
    j = jnp.arange(GLA_SUB)[:, None, None]
    return (p == (d // GLA_KEY_DIM) * GLA_SUB + (GLA_SUB - 1) - j).astype(bf16)


def _prompt_mixer_kernel(rq, rk, rv, rg, mq, mk, mv, mo, dq, dk, dv, dz, lq, lk, lv, lz, sm_ref,
                         cos_ref, sin_ref, gb_ref, al_ref, wc_ref, cs_ref, mlnw_ref, gdnw_ref, glnw_ref,
                         w2_ref, glb_ref, seg_ref,
                         y_ref, ret_o, mlc_o, mln_o, mlm_o, gdn_o, conv_o, gla_o,
                         ret_s, mlc_s, mln_s, mlm_s, gdn_s, xb_s, vn_s, gla_s, *, tb):
    gw = GROUP_WIDTH
    _ret_prompt_body(rq, rk, rv, rg, cos_ref, sin_ref, y_ref, ret_o, ret_s, tb=tb, y_off=0)
    _mlstm_prompt_body(mq, mk, mv, mo, sm_ref, gb_ref, mlnw_ref, y_ref, mlc_o, mln_o, mlm_o,
                       mlc_s, mln_s, mlm_s, tb=tb, y_off=gw)
    _gdn_prompt_body(dq, dk, dv, dz, sm_ref, gb_ref, al_ref, wc_ref, cs_ref, gdnw_ref, y_ref, gdn_o, conv_o,
                     gdn_s, xb_s, vn_s, tb=tb, y_off=2 * gw)
    _gla_prompt_body(lq, lk, lv, lz, sm_ref, w2_ref, glb_ref, glnw_ref, seg_ref, y_ref, gla_o, gla_s,
                     tb=tb, y_off=3 * gw)


def _prompt_mixers(c_all, lp, *, batch, seq, conv_zero):
    tb = MIX_TB
    nt = seq // tb
    gw = GROUP_WIDTH
    kw = HEADS * GLA_KEY_DIM

    def cspec(name, width):
        blk = C_OFF[name] // width
        assert C_OFF[name] % width == 0
        return pl.BlockSpec((tb, width), lambda b, t: (b * nt + t, blk))

    def const2(shape):
        return pl.BlockSpec(shape, lambda b, t: (0, 0))

    def per_batch(shape):
        return pl.BlockSpec((1,) + shape, lambda b, t: (b,) + (0,) * len(shape))

    names = [("ret_q", gw), ("ret_k", gw), ("ret_v", gw), ("ret_g", gw),
             ("ml_q", gw), ("ml_k", gw), ("ml_v", gw), ("ml_o", gw),
             ("gd_q", gw), ("gd_k", gw), ("gd_v", gw), ("gd_z", gw),
             ("gl_q", kw), ("gl_k", kw), ("gl_v", gw), ("gl_z", gw), ("small", LANES)]
    in_specs = [cspec(n, w) for n, w in names] + [
        pl.BlockSpec((tb, HEAD_DIM), lambda b, t: (t, 0)), pl.BlockSpec((tb, HEAD_DIM), lambda b, t: (t, 0)),
        const2((1, LANES)), const2((1, LANES)), const2((CONV_W, 3 * gw)), per_batch((CONV_W - 1, 3 * gw)),
        const2((1, gw)), const2((1, HEAD_DIM)), const2((1, HEAD_DIM)), const2((LANES, kw)), const2((1, kw)),
        pl.BlockSpec((GLA_SUB, kw, LANES), lambda b, t: (0, 0, 0))]
    assert tb % LANES == 0
    state_shape = (HEADS, HEAD_DIM, HEAD_DIM)
    out_shapes = [(c_all.shape[0], D_MODEL), (batch,) + state_shape, (batch,) + state_shape,
                  (batch, HEADS, HEAD_DIM), (batch, SUBLANES, LANES), (batch,) + state_shape,
                  (batch, CONV_W - 1, 3 * gw), (batch, HEADS // 2, LANES, HEAD_DIM)]
    out_specs = [pl.BlockSpec((tb, D_MODEL), lambda b, t: (b * nt + t, 0))] + [
        per_batch(s[1:]) for s in out_shapes[1:]]
    outs = pl.pallas_call(
        functools.partial(_prompt_mixer_kernel, tb=tb),
        grid=(batch, nt),
        in_specs=in_specs,
        out_specs=out_specs,
        out_shape=[jax.ShapeDtypeStruct(out_shapes[0], bf16)] + [jax.ShapeDtypeStruct(s, f32) for s in out_shapes[1:]],
        scratch_shapes=[pltpu.VMEM(state_shape, f32), pltpu.VMEM(state_shape, f32),
                        pltpu.VMEM((SUBLANES, HEAD_DIM), f32), pltpu.VMEM((SUBLANES, LANES), f32),
                        pltpu.VMEM(state_shape, f32), pltpu.VMEM((3, tb + SUBLANES, gw), f32),
                        pltpu.VMEM((tb, HEAD_DIM), bf16), pltpu.VMEM((HEADS // 2, LANES, HEAD_DIM), f32)],
        compiler_params=_cparams(("parallel", "arbitrary")),
        name="prompt_mixers",
    )(*([c_all] * len(names)), lp["cos_p"], lp["sin_p"], lp["gate_bias"], lp["alog_row"], lp["gd_conv"],
      conv_zero, lp["ml_norm"], lp["gd_norm"], lp["gl_norm"], lp["gl_w2p"], lp["gl_b"], _gla_segment_table())
    y, s_ret, s_c, s_n, s_m, s_gdn, s_conv, s_gla = outs
    states = {"ret": s_ret, "ml_C": s_c, "ml_n": s_n, "ml_m": s_m[:, :HEADS, 0],
              "gdn": s_gdn, "gdn_conv": s_conv,
              "gla": s_gla.reshape(batch, HEADS, GLA_KEY_DIM, HEAD_DIM)}
    return y, states


def _rows_pad(x):
    return jnp.concatenate([x, jnp.zeros((LANES - x.shape[0], LANES), f32)], axis=0)


def _cols(x):
    return _rows_pad(x).T


def _row(x, j):
    r = lax.broadcasted_iota(jnp.int32, (x.shape[0], 1), 0)
    return jnp.sum(jnp.where(r == j, x, 0.0), axis=0, keepdims=True)


def _outer(x_cols, j, y_rows):
    lane = lax.broadcasted_iota(jnp.int32, (1, LANES), 1)
    return jnp.dot(jnp.where(lane == j, x_cols, 0.0).astype(bf16), y_rows, preferred_element_type=f32)


def _sample_mixer_kernel(c_ref, cos_ref, sin_ref, gb_ref, al_ref, wc_ref, mlnw_ref, gdnw_ref, glnw_ref,
                         w2_ref, glb_ref,
                         ret_ref, mlc_ref, mln_ref, mlm_ref, gdn_ref, conv_ref, gla_ref, *rest):
    y_ref, ret_o, mlc_o, mln_o, mlm_o, gdn_o, conv_o, gla_o = rest[-8:]
    G = SAMPLE_G
    gw = GROUP_WIDTH
    dk = GLA_KEY_DIM

    def cblk(name, width):
        return c_ref[:, C_OFF[name]:C_OFF[name] + width]

    small = cblk("small", LANES)
    pre = small + gb_ref[...]
    cosf = cos_ref[...]
    sinf = sin_ref[...]

    for h in range(HEADS):
        hs = slice(h * HEAD_DIM, (h + 1) * HEAD_DIM)
        gamma = 1.0 - 2.0 ** (-5.0 - h)
        q = _rope(cblk("ret_q", gw)[:, hs], cosf, sinf)
        k = _rope(cblk("ret_k", gw)[:, hs], cosf, sinf) * HEAD_DIM ** -0.5
        v = cblk("ret_v", gw)[:, hs]
        kt, vpad, qb = _cols(k), _rows_pad(v).astype(bf16), q.astype(bf16)
        rows = []
        for j in range(G):
            s_new = gamma * ret_ref[j, h] + _outer(kt, j, vpad)
            ret_o[j, h] = s_new
            rows.append(_row(jnp.dot(qb, s_new.astype(bf16), preferred_element_type=f32), j))
        o = jnp.concatenate(rows, axis=0)
        y = _rms_rows(o) * _silu(cblk("ret_g", gw)[:, hs])
        y_ref[:, 0 * gw + h * HEAD_DIM:0 * gw + (h + 1) * HEAD_DIM] = y.astype(y_ref.dtype)

    m_old = mlm_ref[...]
    f_al = pltpu.roll(jax.nn.log_sigmoid(pre), LANES - (L_MLF - L_MLI), 1)
    m_int = m_old + f_al
    m_t = jnp.maximum(m_int, pre)
    w_in = jnp.exp(pre - m_t)
    g_in = jnp.exp(m_int - m_t)
    e_neg = jnp.exp(-m_t)
    mlm_o[...] = m_t
    for h in range(HEADS):
        hs = slice(h * HEAD_DIM, (h + 1) * HEAD_DIM)
        q = cblk("ml_q", gw)[:, hs]
        k = cblk("ml_k", gw)[:, hs] * HEAD_DIM ** -0.5
        v = cblk("ml_v", gw)[:, hs]
        kw = k * w_in[:, h:h + 1]
        kt, vpad, qb = _cols(kw), _rows_pad(v).astype(bf16), q.astype(bf16)
        rows = []
        for j in range(G):
            gj = g_in[j:j + 1, h:h + 1]
            c_new = gj * mlc_ref[j, h] + _outer(kt, j, vpad)
            mlc_o[j, h] = c_new
            n_new = gj * mln_ref[j, h:h + 1, :] + kw[j:j + 1, :]
            mln_o[j, h:h + 1, :] = n_new
            num = _row(jnp.dot(qb, c_new.astype(bf16), preferred_element_type=f32), j)
            den = jnp.sum(q[j:j + 1, :] * n_new, axis=-1, keepdims=True)
            rows.append(num / jnp.maximum(jnp.abs(den), e_neg[j:j + 1, h:h + 1]))
        hh = jnp.concatenate(rows, axis=0)
        y = _rms_rows(hh) * mlnw_ref[:, hs] * jax.nn.sigmoid(cblk("ml_o", gw)[:, hs])
        y_ref[:, 1 * gw + h * HEAD_DIM:1 * gw + (h + 1) * HEAD_DIM] = y.astype(y_ref.dtype)

    beta_all = jax.nn.sigmoid(pre)
    eg_all = jnp.exp(-jnp.exp(al_ref[...]) * jax.nn.softplus(pre))
    conv = []
    for p, name in enumerate(("gd_q", "gd_k", "gd_v")):
        ps = slice(p * gw, (p + 1) * gw)
        x = cblk(name, gw)
        acc = x * wc_ref[CONV_W - 1:CONV_W, ps]
        for j in range(CONV_W - 1):
            acc = acc + conv_ref[j, :, ps] * wc_ref[j:j + 1, ps]
        conv.append(_silu(acc))
        for j in range(CONV_W - 2):
            conv_o[j, :, ps] = conv_ref[j + 1, :, ps]
        conv_o[CONV_W - 2, :, ps] = x
    for h in range(HEADS):
        hs = slice(h * HEAD_DIM, (h + 1) * HEAD_DIM)
        q = _l2n_rows(conv[0][:, hs]) * HEAD_DIM ** -0.5
        k = _l2n_rows(conv[1][:, hs])
        v = conv[2][:, hs]
        kt, kb16, qb = _cols(k), k.astype(bf16), q.astype(bf16)
        bcol = beta_all[:, L_GDB + h:L_GDB + h + 1]
        ecol = eg_all[:, L_GDA + h:L_GDA + h + 1]
        ks = jnp.concatenate(
            [_row(jnp.dot(kb16, gdn_ref[j, h].astype(bf16), preferred_element_type=f32), j) for j in range(G)],
            axis=0)
        vpad = _rows_pad(bcol * (v - ecol * ks)).astype(bf16)
        rows = []
        for j in range(G):
            s_new = ecol[j:j + 1, :] * gdn_ref[j, h] + _outer(kt, j, vpad)
            gdn_o[j, h] = s_new
            rows.append(_row(jnp.dot(qb, s_new.astype(bf16), preferred_element_type=f32), j))
        o = jnp.concatenate(rows, axis=0)
        y = _rms_rows(o) * gdnw_ref[...] * _silu(cblk("gd_z", gw)[:, hs])
        y_ref[:, 2 * gw + h * HEAD_DIM:2 * gw + (h + 1) * HEAD_DIM] = y.astype(y_ref.dtype)

    gk = jax.nn.log_sigmoid(_dot_exact(small, w2_ref[...]) + glb_ref[...]) * (1.0 / GLA_GATE_DENOM)
    eg = jnp.exp(gk)
    q_all = cblk("gl_q", HEADS * dk) * dk ** -0.5
    k_all = cblk("gl_k", HEADS * dk)
    upper = lax.broadcasted_iota(jnp.int32, (LANES, 1), 0) < dk
    lane = lax.broadcasted_iota(jnp.int32, (1, LANES), 1)
    for p in range(HEADS // 2):
        ps = slice(p * LANES, (p + 1) * LANES)
        kp, qp = k_all[:, ps], q_all[:, ps]
        kt2 = _cols(jnp.concatenate([kp, kp], axis=0))
        et = _cols(eg[:, ps])
        v0 = cblk("gl_v", gw)[:, (2 * p) * HEAD_DIM:(2 * p + 1) * HEAD_DIM]
        v1 = cblk("gl_v", gw)[:, (2 * p + 1) * HEAD_DIM:(2 * p + 2) * HEAD_DIM]
        vpad = _rows_pad(jnp.concatenate([v0, v1], axis=0)).astype(bf16)
        q2 = jnp.concatenate([jnp.where(lane < dk, qp, 0.0), jnp.where(lane < dk, 0.0, qp)], axis=0).astype(bf16)
        rows0, rows1 = [], []
        for j in range(G):
            keep = lane == jnp.where(upper, j, G + j)
            u = jnp.dot(jnp.where(keep, kt2, 0.0).astype(bf16), vpad, preferred_element_type=f32)
            s_new = et[:, j:j + 1] * gla_ref[j, p] + u
            gla_o[j, p] = s_new
            res = jnp.dot(q2, s_new.astype(bf16), preferred_element_type=f32)
            rows0.append(_row(res, j))
            rows1.append(_row(res, G + j))
        for hh, rows in enumerate((rows0, rows1)):
            h = 2 * p + hh
            hs = slice(h * HEAD_DIM, (h + 1) * HEAD_DIM)
            o = jnp.concatenate(rows, axis=0)
            y = _rms_rows(o) * glnw_ref[...] * _silu(cblk("gl_z", gw)[:, hs])
            y_ref[:, 3 * gw + h * HEAD_DIM:3 * gw + (h + 1) * HEAD_DIM] = y.astype(y_ref.dtype)


def _sample_mixers(c_all, lp, st, y_prev, st_prev, *, layer, depth, row0, nb):
    G = SAMPLE_G
    gw = GROUP_WIDTH
    kw = HEADS * GLA_KEY_DIM
    assert row0 % G == 0 and nb % G == 0
    blk0 = row0 // G

    def const2(shape):
        return pl.BlockSpec(shape, lambda i: (0, 0))

    def slab(shape, batch_axis=0):
        def index(i):
            idx = [0] * len(shape)
            idx[batch_axis] = i
            return (layer,) + tuple(idx)
        return pl.BlockSpec((None,) + shape, index)

    state_specs = [slab((G, HEADS, HEAD_DIM, HEAD_DIM)), slab((G, HEADS, HEAD_DIM, HEAD_DIM)),
                   slab((G, HEADS, HEAD_DIM)), slab((G, LANES)), slab((G, HEADS, HEAD_DIM, HEAD_DIM)),
                   slab((CONV_W - 1, G, 3 * gw), batch_axis=1), slab((G, HEADS // 2, LANES, HEAD_DIM))]
    order = ("ret", "ml_C", "ml_n", "ml_m", "gdn", "gdn_conv", "gla")
    states = [st[n] for n in order]
    y_spec = pl.BlockSpec((G, D_MODEL), lambda i: (blk0 + i, 0))

    args = [c_all, lp["cos_s"], lp["sin_s"], lp["gate_bias"], lp["alog_row"], lp["gd_conv"], lp["ml_norm"],
            lp["gd_norm"], lp["gl_norm"], lp["gl_w2p"], lp["gl_b"]] + states
    in_specs = [pl.BlockSpec((G, C_WIDTH), lambda i: (blk0 + i, 0)),
                const2((1, HEAD_DIM)), const2((1, HEAD_DIM)), const2((1, LANES)), const2((1, LANES)),
                const2((CONV_W, 3 * gw)), const2((1, gw)), const2((1, HEAD_DIM)), const2((1, HEAD_DIM)),
                const2((LANES, kw)), const2((1, kw))] + state_specs
    inplace = [y_prev] + ([st_prev[n] for n in order] if st_prev is not None else [])
    aliases = {len(args) + k: k for k in range(len(inplace))}
    args += inplace
    in_specs += [pl.BlockSpec(memory_space=pl.ANY)] * len(inplace)

    outs = pl.pallas_call(
        _sample_mixer_kernel,
        grid=(nb // G,),
        in_specs=in_specs,
        out_specs=[y_spec] + state_specs,
        out_shape=[jax.ShapeDtypeStruct(y_prev.shape, y_prev.dtype)] + [
            jax.ShapeDtypeStruct(s.shape, s.dtype) for s in states],
        input_output_aliases=aliases,
        compiler_params=_cparams(("parallel",)),
        name="sample_mixers",
    )(*args)
    return outs[0], dict(zip(order, outs[1:]))


def _rope_tables(pos):
    half = HEAD_DIM // 2
    inv = ROPE_BASE ** (-jnp.arange(half, dtype=f32) / half)
    ang = pos[:, None] * inv[None, :]
    cos, sin = jnp.cos(ang), jnp.sin(ang)
    return jnp.concatenate([cos, cos], axis=-1), jnp.concatenate([-sin, sin], axis=-1)


def _permute_w_in(w):
    big = jnp.concatenate([w[..., 0:4096], w[..., 4104:6152], w[..., 6160:7696]], axis=-1)
    small = jnp.concatenate([w[..., 4096:4104], w[..., 6152:6160], w[..., 7696:7712]], axis=-1)
    pad = jnp.zeros(w.shape[:-1] + (C_WIDTH - big.shape[-1] - small.shape[-1],), w.dtype)
    return jnp.concatenate([big, small, pad], axis=-1)


W_IN_RAW = 7712
W_IN_TILE = 512
W_IN_SHIFTS = ((8, 8, 12), (16, 12, 15))
W_IN_EDGE = 16
W_IN_GATE_ROWS = (4096, 6144, 7696)


def _w_in_prep_kernel(a_ref, b_ref, c0_ref, c1_ref, c2_ref, o_ref):
    j = pl.program_id(1)
    k = a_ref.shape[1]

    def emit(x):
        o_ref[...] = x.T.astype(bf16)

    @pl.when(j < W_IN_SHIFTS[0][1])
    def _():
        emit(a_ref[...])

    for shift, lo, hi in W_IN_SHIFTS:
        @pl.when(jnp.logical_and(j >= lo, j < hi))
        def _():
            emit(jnp.concatenate([a_ref[shift:, :], b_ref[:shift, :]], axis=0))

    @pl.when(j >= W_IN_SHIFTS[-1][2])
    def _():
        gates = jnp.concatenate([c0_ref[:L_GDB, :], c1_ref[L_GDB:L_GLR, :], c2_ref[...]], axis=0)
        emit(jnp.concatenate([gates, jnp.zeros((W_IN_TILE - gates.shape[0], k), f32)], axis=0))


def _w_in_prep(w_in):
    depth, d, n = w_in.shape
    assert n == W_IN_RAW and C_OFF["small"] == W_IN_SHIFTS[-1][2] * W_IN_TILE
    assert L_GLR + GLA_RANK == 2 * W_IN_EDGE and n % W_IN_EDGE == 0
    w_t = jnp.swapaxes(w_in, 1, 2)
    last_wide = W_IN_SHIFTS[-1][2] - 1
    per_tile = W_IN_TILE // W_IN_EDGE
    last_edge = n // W_IN_EDGE - 1

    def edge(index):
        return pl.BlockSpec((None, W_IN_EDGE, d), index)

    g0, g1, g2 = (r // W_IN_EDGE for r in W_IN_GATE_ROWS)
    return pl.pallas_call(
        _w_in_prep_kernel,
        grid=(depth, C_WIDTH // W_IN_TILE),
        in_specs=[pl.BlockSpec((None, W_IN_TILE, d), lambda l, j: (l, jnp.minimum(j, last_wide), 0)),
                  edge(lambda l, j: (l, jnp.minimum(per_tile * (j + 1), last_edge), 0)),
                  edge(lambda l, j: (l, g0, 0)), edge(lambda l, j: (l, g1, 0)), edge(lambda l, j: (l, g2, 0))],
        out_specs=pl.BlockSpec((None, d, W_IN_TILE), lambda l, j: (l, 0, j)),
        out_shape=jax.ShapeDtypeStruct((depth, d, C_WIDTH), bf16),
        compiler_params=_cparams(("parallel", "arbitrary")),
        name="w_in_prep",
    )(w_t, w_t, w_t, w_t, w_t)


def _lane_row(pieces):
    row = jnp.zeros((1, LANES), f32)
    for off, val in pieces:
        row = lax.dynamic_update_slice(row, val.reshape(1, -1).astype(f32), (0, off))
    return row


def kernel(x_prompt, x_sample, state_ret, state_mlstm_C, state_mlstm_n, state_mlstm_m, state_gdn, state_gdn_conv, state_gla, cache_mem_k, cache_mem_v, mem_prompt, norm_mix_pre, norm_mix_post, w_in, ml_ib, ml_fb, ml_norm, gd_conv, gd_A_log, gd_dt_bias, gd_norm, gl_w2, gl_b, gl_norm, w_out, norm_x_pre, norm_x_post, norm_mem, w_xq, w_xk, w_xv, w_xo, norm_mlp_pre, norm_mlp_post, w_up, w_down):
    bp, tp, d = x_prompt.shape
    bs, ts, _ = x_sample.shape
    depth = w_in.shape[0]
    assert ts == 1 and d == D_MODEL and tp % MIX_TB == 0
    rows_p = bp * tp
    rows = rows_p + bs
    assert rows % ROW_TILE == 0 and rows_p % SAMPLE_G == 0

    xh = _embed(x_prompt.reshape(rows_p, d), norm_mix_pre[0], None, row0=0, rows=rows, tm=512)
    x, h = _embed(x_sample.reshape(bs, d), norm_mix_pre[0], xh, row0=rows_p, rows=rows, tm=bs)
    w_in_b = _w_in_prep(w_in)
    w_out_b, w_xo_b, w_down_b = w_out.astype(bf16), w_xo.astype(bf16), w_down.astype(bf16)
    cos_p, sin_p = _rope_tables(jnp.arange(tp, dtype=f32))
    cos_s, sin_s = _rope_tables(jnp.arange(ts, dtype=f32) + PAST_LEN)
    conv_zero = jnp.zeros((bp, CONV_W - 1, 3 * GROUP_WIDTH), f32)
    mem2d = mem_prompt.reshape(bp * MEM_LEN, d)
    st_in = {"ret": state_ret, "ml_C": state_mlstm_C, "ml_n": state_mlstm_n,
             "ml_m": jnp.pad(state_mlstm_m, ((0, 0), (0, 0), (0, LANES - HEADS))),
             "gdn": state_gdn, "gdn_conv": jnp.transpose(state_gdn_conv, (0, 2, 1, 3)),
             "gla": state_gla.reshape(depth, bs, HEADS // 2, LANES, HEAD_DIM)}

    new_p = {n: [] for n in ("ret", "ml_C", "ml_n", "ml_m", "gdn", "gdn_conv", "gla")}
    st_s = None
    mk_p = mv_p = None
    for l in range(depth):
        lp = {
            "cos_p": cos_p, "sin_p": sin_p, "cos_s": cos_s, "sin_s": sin_s,
            "gate_bias": _lane_row([(L_MLI, ml_ib[l]), (L_MLF, ml_fb[l]), (L_GDA, gd_dt_bias[l])]),
            "alog_row": _lane_row([(L_GDA, gd_A_log[l])]),
            "ml_norm": ml_norm[l].reshape(1, GROUP_WIDTH),
            "gd_norm": gd_norm[l].reshape(1, HEAD_DIM),
            "gl_norm": gl_norm[l].reshape(1, HEAD_DIM),
            "gd_conv": gd_conv[l],
            "gl_w2p": jnp.zeros((LANES, HEADS * GLA_KEY_DIM), f32).at[L_GLR:L_GLR + GLA_RANK].set(gl_w2[l]),
            "gl_b": gl_b[l].reshape(1, HEADS * GLA_KEY_DIM),
        }
        mk_p = _norm_matmul_slab(mem2d, norm_mem[l], w_xk, mk_p, layer=l, tm=512, tn=1024, name="mem_k")
        mv_p = _norm_matmul_slab(mem2d, norm_mem[l], w_xv, mv_p, layer=l, tm=512, tn=1024, name="mem_v")

        c_all = _matmul(h, w_in_b, layer=l, tm=ROW_TILE, tn=1024, out_dtype=f32, name="w_in")
        y_all, st_p = _prompt_mixers(c_all, lp, batch=bp, seq=tp, conv_zero=conv_zero)
        y_all, st_s = _sample_mixers(c_all, lp, st_in, y_all, st_s, layer=l, depth=depth, row0=rows_p, nb=bs)
        x, h = _matmul_norm_res(y_all, w_out_b, norm_mix_post[l], x, norm_x_pre[l], layer=l,
                                tm=ROW_TILE, tk=D_MODEL, name="w_out")

        q_all = _matmul(h, w_xq, layer=l, tm=ROW_TILE, tn=1024, out_dtype=bf16, name="w_xq")
        o_all = _xattn_prompt(q_all, mk_p, mv_p, layer=l, batch=bp, seq=tp, tq=512)
        o_all = _xattn_sample(q_all, cache_mem_k, cache_mem_v, o_all, layer=l, row0=rows_p, nb=bs)
        x, h = _matmul_norm_res(o_all, w_xo_b, norm_x_post[l], x, norm_mlp_pre[l], layer=l,
                                tm=ROW_TILE, tk=D_MODEL, name="w_xo")

        u = _matmul(h, w_up, layer=l, tm=ROW_TILE, tn=1024, out_dtype=bf16, act="relu2", name="w_up")
        g_next = norm_mix_pre[l + 1] if l + 1 < depth else None
        x, h = _matmul_norm_res(u, w_down_b, norm_mlp_post[l], x, g_next, layer=l,
                                tm=ROW_TILE, tk=1024, name="w_down")

        for n in new_p:
            new_p[n].append(st_p[n])

    def stk(lst):
        return jnp.stack(lst, axis=0)

    xp = x[:rows_p].reshape(bp, tp, d)
    xs = x[rows_p:].reshape(bs, ts, d)
    return (xp, xs,
            stk(new_p["ret"]), stk(new_p["ml_C"]), stk(new_p["ml_n"]), stk(new_p["ml_m"]),
            stk(new_p["gdn"]), stk(new_p["gdn_conv"]), stk(new_p["gla"]),
            mk_p.reshape(depth, bp, MEM_LEN, d), mv_p.reshape(depth, bp, MEM_LEN, d),
            st_s["ret"], st_s["ml_C"], st_s["ml_n"], st_s["ml_m"][:, :, :HEADS],
            st_s["gdn"], jnp.transpose(st_s["gdn_conv"], (0, 2, 1, 3)),
            st_s["gla"].reshape(depth, bs, HEADS, GLA_KEY_DIM, HEAD_DIM))
```

```python
import functools
import math

import jax
import jax.numpy as jnp
from jax import lax
from jax.experimental import pallas as pl
from jax.experimental.pallas import tpu as pltpu

f32 = jnp.float32
bf16 = jnp.bfloat16

D_MODEL = 2048
HEADS = 4
HEAD_DIM = 128
GROUP_WIDTH = 512
GLA_KEY_DIM = 64
GLA_RANK = 16
GLA_GATE_DENOM = 16.0
CONV_W = 4
CHUNK = 64
ROPE_BASE = 10000.0
X_HEADS = 4
X_HEAD_DIM = 512
MEM_LEN = 256
D_FF = 8192
EPS = 1e-6
PAST_LEN = 16384
NEG_INF = float("-inf")

LANES = 128
SUBLANES = 8

C_OFF = {
    "ret_q": 0, "ret_k": 512, "ret_v": 1024, "ret_g": 1536,
    "ml_q": 2048, "ml_k": 2560, "ml_v": 3072, "ml_o": 3584,
    "gd_q": 4096, "gd_k": 4608, "gd_v": 5120, "gd_z": 5632,
    "gl_q": 6144, "gl_k": 6400, "gl_v": 6656, "gl_z": 7168,
    "small": 7680,
}
C_WIDTH = 8192
L_MLI, L_MLF, L_GDB, L_GDA, L_GLR = 0, 4, 8, 12, 16

ROW_TILE = 640
MIX_TB = 256
SAMPLE_G = 8
GLA_SUB = 16
XATTN_S = 2
VMEM_LIMIT = 56 * 1024 * 1024


def _cparams(sem):
    return pltpu.CompilerParams(dimension_semantics=sem, vmem_limit_bytes=VMEM_LIMIT)


def _dot(a, b):
    return jnp.dot(a.astype(bf16), b.astype(bf16), preferred_element_type=f32)


def _dot_nt(a, b):
    return lax.dot_general(a.astype(bf16), b.astype(bf16), (((1,), (1,)), ((), ())),
                           preferred_element_type=f32)


def _dot_tn(a, b):
    return lax.dot_general(a.astype(bf16), b.astype(bf16), (((0,), (0,)), ((), ())),
                           preferred_element_type=f32)


def _dot_exact(a, b):
    return jnp.dot(a, b, preferred_element_type=f32, precision=lax.Precision.HIGHEST)


def _mask_dot_f32(mask, x):
    return _dot_exact(jnp.where(mask, 1.0, 0.0), x)


def _rms_rows(x):
    return x * lax.rsqrt(jnp.mean(x * x, axis=-1, keepdims=True) + EPS)


def _l2n_rows(x):
    return x * lax.rsqrt(jnp.sum(x * x, axis=-1, keepdims=True) + EPS)


def _silu(x):
    return x * jax.nn.sigmoid(x)


def _rope(x, cosf, sinf):
    return x * cosf + pltpu.roll(x, HEAD_DIM // 2, 1) * sinf


def _chunk_masks(tb):
    r = lax.broadcasted_iota(jnp.int32, (tb, tb), 0)
    c = lax.broadcasted_iota(jnp.int32, (tb, tb), 1)
    same = (r // CHUNK) == (c // CHUNK)
    tril = jnp.logical_and(same, r >= c)
    strict = jnp.logical_and(same, r > c)
    return r, c, same, tril, strict


def _rms_gain(x, g):
    return x * lax.rsqrt(jnp.mean(x * x, axis=-1, keepdims=True) + EPS) * g


def _embed_kernel(x_in_ref, g_ref, *rest):
    x_ref, h_ref = rest[-2:]
    x = x_in_ref[...]
    x_ref[...] = x
    h_ref[...] = _rms_gain(x, g_ref[...]).astype(bf16)


def _embed(x2d, g, prev, *, row0, rows, tm):
    m, d = x2d.shape
    assert m % tm == 0 and row0 % tm == 0
    blk0 = row0 // tm
    args = [x2d, g.reshape(1, d)] + (list(prev) if prev is not None else [])
    out_spec = pl.BlockSpec((tm, d), lambda i: (blk0 + i, 0))
    return pl.pallas_call(
        _embed_kernel,
        grid=(m // tm,),
        in_specs=[pl.BlockSpec((tm, d), lambda i: (i, 0)), pl.BlockSpec((1, d), lambda i: (0, 0))] + (
            [pl.BlockSpec(memory_space=pl.ANY)] * 2 if prev is not None else []),
        out_specs=[out_spec, out_spec],
        out_shape=[jax.ShapeDtypeStruct((rows, d), f32), jax.ShapeDtypeStruct((rows, d), bf16)],
        input_output_aliases={2: 0, 3: 1} if prev is not None else {},
        compiler_params=_cparams(("parallel",)),
        name="embed",
    )(*args)


def _matmul_kernel(h_ref, w_ref, *rest, act, cast_w):
    if cast_w:
        o_ref, wb_ref = rest

        @pl.when(pl.program_id(1) == 0)
        def _():
            wb_ref[...] = w_ref[...].astype(bf16)

        w = wb_ref[...]
    else:
        (o_ref,) = rest
        w = w_ref[...]
    y = jnp.dot(h_ref[...], w, preferred_element_type=f32)
    if act == "relu2":
        y = jnp.square(jnp.maximum(y, 0.0))
    o_ref[...] = y.astype(o_ref.dtype)


def _matmul(h, w, *, layer, tm, tn, out_dtype, name, act=None):
    m, k = h.shape
    n = w.shape[2]
    assert m % tm == 0 and n % tn == 0
    cast_w = w.dtype != bf16
    return pl.pallas_call(
        functools.partial(_matmul_kernel, act=act, cast_w=cast_w),
        grid=(n // tn, m // tm),
        in_specs=[pl.BlockSpec((tm, k), lambda j, i: (i, 0)),
                  pl.BlockSpec((None, k, tn), lambda j, i: (layer, 0, j))],
        out_specs=pl.BlockSpec((tm, tn), lambda j, i: (i, j)),
        out_shape=jax.ShapeDtypeStruct((m, n), out_dtype),
        scratch_shapes=[pltpu.VMEM((k, tn), bf16)] if cast_w else [],
        compiler_params=_cparams(("parallel", "arbitrary")),
        name=name,
    )(h, w)


def _norm_matmul_kernel(x_ref, g_ref, w_ref, *rest):
    o_ref, h_ref = rest[-2:]

    @pl.when(pl.program_id(1) == 0)
    def _():
        h_ref[...] = _rms_gain(x_ref[...], g_ref[...]).astype(bf16)

    o_ref[...] = jnp.dot(h_ref[...], w_ref[...].astype(bf16), preferred_element_type=f32)


def _norm_matmul_slab(x, g, w, prev, *, layer, tm, tn, name):
    m, k = x.shape
    depth, _, n = w.shape
    assert m % tm == 0 and n % tn == 0
    in_specs = [pl.BlockSpec((tm, k), lambda i, j: (i, 0)),
                pl.BlockSpec((1, k), lambda i, j: (0, 0)),
                pl.BlockSpec((None, k, tn), lambda i, j: (layer, 0, j))]
    args = [x, g.reshape(1, k), w]
    aliases = {}
    if prev is not None:
        in_specs.append(pl.BlockSpec(memory_space=pl.ANY))
        args.append(prev)
        aliases = {len(args) - 1: 0}
    return pl.pallas_call(
        _norm_matmul_kernel,
        grid=(m // tm, n // tn),
        in_specs=in_specs,
        out_specs=pl.BlockSpec((None, tm, tn), lambda i, j: (layer, i, j)),
        out_shape=jax.ShapeDtypeStruct((depth, m, n), f32),
        scratch_shapes=[pltpu.VMEM((tm, k), bf16)],
        input_output_aliases=aliases,
        compiler_params=_cparams(("parallel", "arbitrary")),
        name=name,
    )(*args)


def _matmul_norm_res_kernel(a_ref, w_ref, g_ref, r_ref, *rest, next_norm, single_k):
    rest = list(rest)
    gn_ref = rest.pop(0) if next_norm else None
    o_ref = rest.pop(0)
    hn_ref = rest.pop(0) if next_norm else None

    def finish(y):
        x_new = r_ref[...] + _rms_gain(y, g_ref[...])
        o_ref[...] = x_new
        if next_norm:
            hn_ref[...] = _rms_gain(x_new, gn_ref[...]).astype(bf16)

    if single_k:
        finish(jnp.dot(a_ref[...], w_ref[...], preferred_element_type=f32))
        return

    (acc_ref,) = rest
    kk = pl.program_id(1)

    @pl.when(kk == 0)
    def _():
        acc_ref[...] = jnp.zeros_like(acc_ref)

    acc_ref[...] += jnp.dot(a_ref[...], w_ref[...], preferred_element_type=f32)

    @pl.when(kk == pl.num_programs(1) - 1)
    def _():
        finish(acc_ref[...])


def _matmul_norm_res(a, w, g, res, g_next, *, layer, tm, tk, name):
    m, k = a.shape
    n = w.shape[2]
    assert m % tm == 0 and k % tk == 0
    next_norm = g_next is not None
    row = pl.BlockSpec((tm, n), lambda i, j: (i, 0))
    vec = pl.BlockSpec((1, n), lambda i, j: (0, 0))
    in_specs = [pl.BlockSpec((tm, tk), lambda i, j: (i, j)),
                pl.BlockSpec((None, tk, n), lambda i, j: (layer, j, 0)), vec, row]
    args = [a, w, g.reshape(1, n), res]
    out_specs = [row]
    out_shape = [jax.ShapeDtypeStruct((m, n), f32)]
    if next_norm:
        in_specs.append(vec)
        args.append(g_next.reshape(1, n))
        out_specs.append(row)
        out_shape.append(jax.ShapeDtypeStruct((m, n), bf16))
    single_k = k == tk
    outs = pl.pallas_call(
        functools.partial(_matmul_norm_res_kernel, next_norm=next_norm, single_k=single_k),
        grid=(m // tm, k // tk),
        in_specs=in_specs,
        out_specs=out_specs,
        out_shape=out_shape,
        scratch_shapes=[] if single_k else [pltpu.VMEM((tm, n), f32)],
        compiler_params=_cparams(("parallel", "arbitrary")),
        name=name,
    )(*args)
    return (outs[0], outs[1]) if next_norm else (outs[0], None)


def _xattn_prompt_kernel(q_ref, k_ref, v_ref, o_ref, kb_ref, vb_ref):
    @pl.when(pl.program_id(1) == 0)
    def _():
        kb_ref[...] = k_ref[...].astype(bf16)
        vb_ref[...] = v_ref[...].astype(bf16)

    scale = X_HEAD_DIM ** -0.5
    for h in range(X_HEADS):
        sl = slice(h * X_HEAD_DIM, (h + 1) * X_HEAD_DIM)
        s = lax.dot_general(q_ref[:, sl], kb_ref[:, sl], (((1,), (1,)), ((), ())),
                            preferred_element_type=f32) * scale
        m = jnp.max(s, axis=-1, keepdims=True)
        p = jnp.exp(s - m)
        l = jnp.sum(p, axis=-1, keepdims=True)
        o = jnp.dot(p.astype(bf16), vb_ref[:, sl], preferred_element_type=f32) / l
        o_ref[:, sl] = o.astype(o_ref.dtype)


def _xattn_prompt(q_all, mk, mv, *, layer, batch, seq, tq):
    nt = seq // tq
    return pl.pallas_call(
        _xattn_prompt_kernel,
        grid=(batch, nt),
        in_specs=[pl.BlockSpec((tq, D_MODEL), lambda b, t: (b * nt + t, 0)),
                  pl.BlockSpec((None, MEM_LEN, D_MODEL), lambda b, t: (layer, b, 0)),
                  pl.BlockSpec((None, MEM_LEN, D_MODEL), lambda b, t: (layer, b, 0))],
        out_specs=pl.BlockSpec((tq, D_MODEL), lambda b, t: (b * nt + t, 0)),
        out_shape=jax.ShapeDtypeStruct(q_all.shape, bf16),
        scratch_shapes=[pltpu.VMEM((MEM_LEN, D_MODEL), bf16), pltpu.VMEM((MEM_LEN, D_MODEL), bf16)],
        compiler_params=_cparams(("parallel", "arbitrary")),
        name="xattn_prompt",
    )(q_all, mk, mv)


def _xattn_sample_kernel(q_ref, k_ref, v_ref, prev_ref, o_ref, acc_ref):
    del prev_ref
    j = pl.program_id(1)
    scale = X_HEAD_DIM ** -0.5
    rows = lax.broadcasted_iota(jnp.int32, (SAMPLE_G, 1), 0)
    q_all = q_ref[...].astype(f32)
    for s_i in range(XATTN_S):
        row = j * XATTN_S + s_i
        q = jnp.sum(jnp.where(rows == row, q_all, 0.0), axis=0, keepdims=True)
        for h in range(X_HEADS):
            sl = slice(h * X_HEAD_DIM, (h + 1) * X_HEAD_DIM)
            s = jnp.sum(k_ref[s_i, :, sl] * q[:, sl], axis=-1, keepdims=True) * scale
            m = jnp.max(s, axis=0, keepdims=True)
            p = jnp.exp(s - m)
            l = jnp.sum(p, axis=0, keepdims=True)
            acc_ref[pl.ds(row, 1), sl] = jnp.sum(p * v_ref[s_i, :, sl], axis=0, keepdims=True) / l

    @pl.when(j == pl.num_programs(1) - 1)
    def _():
        o_ref[...] = acc_ref[...].astype(o_ref.dtype)


def _xattn_sample(q_all, ck, cv, o_prev, *, layer, row0, nb):
    G = SAMPLE_G
    blk0 = row0 // G
    per_g = G // XATTN_S
    kv_spec = pl.BlockSpec((None, XATTN_S, MEM_LEN, D_MODEL), lambda g, j: (layer, g * per_g + j, 0, 0))
    return pl.pallas_call(
        _xattn_sample_kernel,
        grid=(nb // G, per_g),
        in_specs=[pl.BlockSpec((G, D_MODEL), lambda g, j: (blk0 + g, 0)), kv_spec, kv_spec,
                  pl.BlockSpec(memory_space=pl.ANY)],
        out_specs=pl.BlockSpec((G, D_MODEL), lambda g, j: (blk0 + g, 0)),
        out_shape=jax.ShapeDtypeStruct(o_prev.shape, o_prev.dtype),
        scratch_shapes=[pltpu.VMEM((G, D_MODEL), f32)],
        input_output_aliases={3: 0},
        compiler_params=_cparams(("parallel", "arbitrary")),
        name="xattn_sample",
    )(q_all, ck, cv, o_prev)


def _ret_prompt_body(q_ref, k_ref, v_ref, g_ref, cos_ref, sin_ref, y_ref, s_out, s_s, *, tb, y_off):
    t = pl.program_id(1)
    nc = tb // CHUNK

    @pl.when(t == 0)
    def _():
        s_s[...] = jnp.zeros_like(s_s)

    cosf = cos_ref[...]
    sinf = sin_ref[...]
    r, c, _, tril, _ = _chunk_masks(tb)
    diff = (r - c).astype(f32)
    ridx = (lax.broadcasted_iota(jnp.int32, (tb, 1), 0) % CHUNK).astype(f32)
    heads = []
    for h in range(HEADS):
        hs = slice(h * HEAD_DIM, (h + 1) * HEAD_DIM)
        lg = math.log(1.0 - 2.0 ** (-5.0 - h))
        decay = jnp.where(tril, jnp.exp(jnp.maximum(diff, 0.0) * lg), 0.0)
        qr = _rope(q_ref[:, hs], cosf, sinf)
        kr = _rope(k_ref[:, hs], cosf, sinf) * HEAD_DIM ** -0.5
        vb = v_ref[:, hs].astype(bf16)
        heads.append(dict(
            vb=vb, o_in=_dot(_dot_nt(qr, kr) * decay, vb),
            qd=(qr * jnp.exp((ridx + 1.0) * lg)).astype(bf16),
            kd=(kr * jnp.exp((CHUNK - 1.0 - ridx) * lg)).astype(bf16),
            cdec=math.exp(CHUNK * lg), st=s_s[h], outs=[]))
    for cc in range(nc):
        sl = slice(cc * CHUNK, (cc + 1) * CHUNK)
        for hd in heads:
            hd["outs"].append(hd["o_in"][sl] + _dot(hd["qd"][sl], hd["st"]))
            hd["st"] = hd["st"] * hd["cdec"] + _dot_tn(hd["kd"][sl], hd["vb"][sl])
    for h, hd in enumerate(heads):
        hs = slice(h * HEAD_DIM, (h + 1) * HEAD_DIM)
        s_s[h] = hd["st"]
        o = jnp.concatenate(hd["outs"], axis=0)
        y = _rms_rows(o) * _silu(g_ref[:, hs])
        y_ref[:, y_off + h * HEAD_DIM:y_off + (h + 1) * HEAD_DIM] = y.astype(y_ref.dtype)

    @pl.when(t == pl.num_programs(1) - 1)
    def _():
        s_out[0] = s_s[...]


def _mlstm_prompt_body(q_ref, k_ref, v_ref, og_ref, sm_ref, gb_ref, nw_ref,
                       y_ref, c_out, n_out, m_out, c_s, n_s, m_s, *, tb, y_off):
    t = pl.program_id(1)
    nc = tb // CHUNK

    @pl.when(t == 0)
    def _():
        c_s[...] = jnp.zeros_like(c_s)
        n_s[...] = jnp.zeros_like(n_s)
        m_s[...] = jnp.zeros_like(m_s)

    _, _, _, tril, _ = _chunk_masks(tb)
    pre = sm_ref[...] + gb_ref[...]
    fall = _mask_dot_f32(tril, jax.nn.log_sigmoid(pre))
    imf_t = (pre - pltpu.roll(fall, LANES - (L_MLF - L_MLI), 1)).T
    heads = []
    for h in range(HEADS):
        hs = slice(h * HEAD_DIM, (h + 1) * HEAD_DIM)
        fcol = fall[:, L_MLF + h:L_MLF + h + 1]
        icol = pre[:, L_MLI + h:L_MLI + h + 1]
        dm = jnp.where(tril, fcol + imf_t[h:h + 1, :], NEG_INF)
        dmax = jnp.max(dm, axis=-1, keepdims=True)
        mp = m_s[h:h + 1, 0:1]
        m_prev, m_int, m_cur = [], [], []
        for cc in range(nc):
            sl = slice(cc * CHUNK, (cc + 1) * CHUNK)
            mi = mp + fcol[sl]
            mt = jnp.maximum(mi, dmax[sl])
            m_prev.append(mp)
            m_int.append(mi)
            m_cur.append(mt)
            mp = mt[CHUNK - 1:CHUNK, :]
        heads.append(dict(fcol=fcol, icol=icol, dm=dm, m_last=mp, m_prev=m_prev, m_int=m_int, m_cur=m_cur))
    for h, hd in enumerate(heads):
        hs = slice(h * HEAD_DIM, (h + 1) * HEAD_DIM)
        w = jnp.exp(hd["dm"] - jnp.concatenate(hd["m_cur"], axis=0))
        qh = q_ref[:, hs]
        kh = k_ref[:, hs] * HEAD_DIM ** -0.5
        vb = v_ref[:, hs].astype(bf16)
        s = _dot_nt(qh, kh) * w
        hd.update(qh=qh, kh=kh, vb=vb, num_in=_dot(s, vb), den_in=jnp.sum(s, axis=-1, keepdims=True),
                  cst=c_s[h], nst=n_s[h:h + 1, :], outs=[])
    for cc in range(nc):
        sl = slice(cc * CHUNK, (cc + 1) * CHUNK)
        for hd in heads:
            mi, mt = hd["m_int"][cc], hd["m_cur"][cc]
            gi = jnp.exp(mi - mt)
            qc = hd["qh"][sl]
            num = hd["num_in"][sl] + gi * _dot(qc, hd["cst"])
            den = hd["den_in"][sl] + gi * jnp.sum(qc * hd["nst"], axis=-1, keepdims=True)
            hd["outs"].append(num / jnp.maximum(jnp.abs(den), jnp.exp(-mt)))
            m_new = mt[CHUNK - 1:CHUNK, :]
            fl = hd["fcol"][sl][CHUNK - 1:CHUNK, :]
            wk = jnp.exp(fl - hd["fcol"][sl] + hd["icol"][sl] - m_new)
            dec = jnp.exp(hd["m_prev"][cc] + fl - m_new)
            kw = hd["kh"][sl] * wk
            hd["cst"] = dec * hd["cst"] + _dot_tn(kw, hd["vb"][sl])
            hd["nst"] = dec * hd["nst"] + jnp.sum(kw, axis=0, keepdims=True)
    for h, hd in enumerate(heads):
        hs = slice(h * HEAD_DIM, (h + 1) * HEAD_DIM)
        c_s[h] = hd["cst"]
        n_s[h:h + 1, :] = hd["nst"]
        m_s[h:h + 1, :] = jnp.broadcast_to(hd["m_last"], (1, LANES))
        hh = jnp.concatenate(hd["outs"], axis=0)
        y = _rms_rows(hh) * nw_ref[:, hs] * jax.nn.sigmoid(og_ref[:, hs])
        y_ref[:, y_off + h * HEAD_DIM:y_off + (h + 1) * HEAD_DIM] = y.astype(y_ref.dtype)

    @pl.when(t == pl.num_programs(1) - 1)
    def _():
        c_out[0] = c_s[...]
        n_out[0] = n_s[0:HEADS, :]
        m_out[0] = m_s[...]


def _unit_lower_inverse_minus_eye(a_list, r, c):
    def coupling(s):
        return jnp.logical_and((r // (2 * s)) == (c // (2 * s)), (r // s) != (c // s))

    eye = r == c
    first = coupling(1)
    xs = [jnp.where(eye, 1.0, jnp.where(first, -a, 0.0)) for a in a_list]
    s = 2
    while s < CHUNK:
        couple = coupling(s)
        ys = [_dot(jnp.where(couple, a, 0.0), x) for a, x in zip(a_list, xs)]
        xs = [x - _dot(x, y) for x, y in zip(xs, ys)]
        s *= 2
    return [jnp.where(eye, 0.0, x) for x in xs]


def _gdn_prompt_body(q_ref, k_ref, v_ref, z_ref, sm_ref, gb_ref, al_ref, wc_ref, cs_ref, nw_ref,
                     y_ref, s_out, conv_out, s_s, xb_s, vn_s, *, tb, y_off):
    t = pl.program_id(1)
    nc = tb // CHUNK
    gw = GROUP_WIDTH
    pad = SUBLANES
    hist = CONV_W - 1

    @pl.when(t == 0)
    def _():
        s_s[...] = jnp.zeros_like(s_s)
        for p in range(3):
            xb_s[p, 0:pad, :] = jnp.zeros((pad, gw), f32)
            xb_s[p, pad - hist:pad, :] = cs_ref[0, :, p * gw:(p + 1) * gw]

    @pl.when(t > 0)
    def _():
        for p in range(3):
            xb_s[p, 0:pad, :] = xb_s[p, tb:tb + pad, :]

    xb_s[0, pad:, :] = q_ref[...]
    xb_s[1, pad:, :] = k_ref[...]
    xb_s[2, pad:, :] = v_ref[...]

    conv = []
    for p in range(3):
        acc = xb_s[p, pad - hist:pad - hist + tb, :] * wc_ref[0:1, p * gw:(p + 1) * gw]
        for j in range(1, CONV_W):
            acc = acc + xb_s[p, pad - hist + j:pad - hist + j + tb, :] * wc_ref[j:j + 1, p * gw:(p + 1) * gw]
        conv.append(_silu(acc))

    r, c, _, tril, strict = _chunk_masks(tb)
    pre = sm_ref[...] + gb_ref[...]
    beta_all = jax.nn.sigmoid(pre)
    g_all = -jnp.exp(al_ref[...]) * jax.nn.softplus(pre)
    gall = _mask_dot_f32(tril, g_all)
    gall_t = gall.T
    heads = []
    for h in range(HEADS):
        hs = slice(h * HEAD_DIM, (h + 1) * HEAD_DIM)
        gcol = gall[:, L_GDA + h:L_GDA + h + 1]
        grow = gall_t[L_GDA + h:L_GDA + h + 1, :]
        bcol = beta_all[:, L_GDB + h:L_GDB + h + 1]
        decay = jnp.where(tril, jnp.exp(jnp.where(tril, gcol - grow, 0.0)), 0.0)
        qh = _l2n_rows(conv[0][:, hs]) * HEAD_DIM ** -0.5
        kh = _l2n_rows(conv[1][:, hs])
        kb = kh * bcol
        egc = jnp.exp(gcol)
        heads.append(dict(
            gcol=gcol, kh=kh, qe=qh * egc,
            a=jnp.where(strict, _dot_nt(kb, kh) * decay, 0.0),
            rhs=jnp.concatenate([conv[2][:, hs] * bcol, kb * egc], axis=1),
            attn=(_dot_nt(qh, kh) * decay).astype(bf16)))
    inv = _unit_lower_inverse_minus_eye([hd["a"] for hd in heads], r, c)
    for h, hd in enumerate(heads):
        x = hd["rhs"] + _dot(inv[h], hd["rhs"])
        hd.update(u=x[:, :HEAD_DIM], w=x[:, HEAD_DIM:], st=s_s[h], outs=[])
        vn_s[h] = jnp.zeros((tb, HEAD_DIM), bf16)
    for cc in range(nc):
        sl = slice(cc * CHUNK, (cc + 1) * CHUNK)
        for h, hd in enumerate(heads):
            st, gcol = hd["st"], hd["gcol"]
            vnew = hd["u"][sl] - _dot(hd["w"][sl], st)
            vn_s[h, sl, :] = vnew.astype(bf16)
            hd["outs"].append(_dot(hd["qe"][sl], st)
                              + jnp.dot(hd["attn"][sl, :], vn_s[h], preferred_element_type=f32))
            gl = gcol[sl][CHUNK - 1:CHUNK, :]
            kdec = hd["kh"][sl] * jnp.exp(gl - gcol[sl])
            hd["st"] = st * jnp.exp(gl) + _dot_tn(kdec, vnew)
    for h, hd in enumerate(heads):
        hs = slice(h * HEAD_DIM, (h + 1) * HEAD_DIM)
        s_s[h] = hd["st"]
        o = jnp.concatenate(hd["outs"], axis=0)
        y = _rms_rows(o) * nw_ref[...] * _silu(z_ref[:, hs])
        y_ref[:, y_off + h * HEAD_DIM:y_off + (h + 1) * HEAD_DIM] = y.astype(y_ref.dtype)

    @pl.when(t == pl.num_programs(1) - 1)
    def _():
        s_out[0] = s_s[...]
        for p in range(3):
            conv_out[0, :, p * gw:(p + 1) * gw] = xb_s[p, tb + pad - hist:tb + pad, :]


def _gla_prompt_body(q_ref, k_ref, v_ref, z_ref, sm_ref, w2_ref, b_ref, nw_ref, seg_ref,
                     y_ref, s_out, s_s, *, tb, y_off):
    t = pl.program_id(1)
    nc = tb // CHUNK
    dk = GLA_KEY_DIM
    sub = GLA_SUB
    nsub = CHUNK // sub

    @pl.when(t == 0)
    def _():
        s_s[...] = jnp.zeros_like(s_s)

    r, c, same, tril, _ = _chunk_masks(tb)
    pre = _dot_exact(sm_ref[...], w2_ref[...]) + b_ref[...]
    gk = jax.nn.log_sigmoid(pre) * (1.0 / GLA_GATE_DENOM)
    g = _mask_dot_f32(tril, gk)
    q = q_ref[...] * dk ** -0.5
    k = k_ref[...]
    rpos = lax.broadcasted_iota(jnp.int32, (tb, 1), 0) % CHUNK
    lane = lax.broadcasted_iota(jnp.int32, (1, LANES), 1)
    lane_lo = lane < dk

    a_off = [jnp.zeros((tb, tb), f32) for _ in range(HEADS)]
    g3 = g.reshape(nc, CHUNK, HEADS * dk)
    for i in range(1, nsub):
        ri = jnp.broadcast_to(g3[:, sub * i:sub * i + 1, :], (nc, CHUNK, HEADS * dk)).reshape(tb, HEADS * dk)
        qs = q * jnp.exp(jnp.where(rpos // sub == i, g - ri, NEG_INF))
        ks = k * jnp.exp(jnp.where(rpos < sub * i, ri - g, NEG_INF))
        for h in range(HEADS):
            ps = slice((h // 2) * LANES, (h // 2 + 1) * LANES)
            keep = lane_lo if h % 2 == 0 else jnp.logical_not(lane_lo)
            a_off[h] = a_off[h] + _dot_nt(jnp.where(keep, qs[:, ps], 0.0), ks[:, ps])

    band = jnp.zeros((tb, LANES), f32)
    for j in range(sub):
        ksh = k if j == 0 else pltpu.roll(k, j, 0)
        gsh = g if j == 0 else pltpu.roll(g, j, 0)
        valid = (rpos % sub) >= j
        prod = q * ksh * jnp.exp(jnp.where(valid, g - gsh, NEG_INF))
        band = band + jnp.dot(prod.astype(bf16), seg_ref[j], preferred_element_type=f32)
    if tb > LANES:
        band = jnp.concatenate([band, jnp.zeros((tb, tb - LANES), f32)], axis=1)
    in_sub = jnp.logical_and((r // sub) == (c // sub), r >= c)
    a_diag = []
    for h in range(HEADS):
        shift = (tb - (sub - 1) - sub * h) % tb
        a_diag.append(jnp.where(in_sub, pltpu.roll(band, shift, 1, stride=1, stride_axis=0), 0.0))

    eg = jnp.exp(g)
    qe = q * eg
    g_t = g.T
    vbs = [v_ref[:, h * HEAD_DIM:(h + 1) * HEAD_DIM].astype(bf16) for h in range(HEADS)]
    o_in = [_dot(a_diag[h] + jnp.where(same, a_off[h], 0.0), vbs[h]) for h in range(HEADS)]
    sts = [s_s[p] for p in range(HEADS // 2)]
    outs = [[] for _ in range(HEADS)]
    for cc in range(nc):
        sl = slice(cc * CHUNK, (cc + 1) * CHUNK)
        last = cc * CHUNK + CHUNK - 1
        for p in range(HEADS // 2):
            ps = slice(p * LANES, (p + 1) * LANES)
            gl_row = g[last:last + 1, ps]
            gl_col = g_t[ps, last:last + 1]
            kdec = k[sl, ps] * jnp.exp(gl_row - g[sl, ps])
            upd = jnp.zeros((LANES, HEAD_DIM), f32)
            for hh in range(2):
                h = 2 * p + hh
                keep = lane_lo if hh == 0 else jnp.logical_not(lane_lo)
                outs[h].append(o_in[h][sl] + _dot(jnp.where(keep, qe[sl, ps], 0.0), sts[p]))
                upd = upd + _dot_tn(jnp.where(keep, kdec, 0.0), vbs[h][sl])
            sts[p] = sts[p] * jnp.exp(gl_col) + upd
    for p in range(HEADS // 2):
        s_s[p] = sts[p]
    for h in range(HEADS):
        hs = slice(h * HEAD_DIM, (h + 1) * HEAD_DIM)
        o = jnp.concatenate(outs[h], axis=0)
        y = _rms_rows(o) * nw_ref[...] * _silu(z_ref[:, hs])
        y_ref[:, y_off + h * HEAD_DIM:y_off + (h + 1) * HEAD_DIM] = y.astype(y_ref.dtype)

    @pl.when(t == pl.num_programs(1) - 1)
    def _():
        s_out[0] = s_s[...]


def _gla_segment_table():
    d = jnp.arange(HEADS * GLA_KEY_DIM)[None, :, None]
    p = jnp.arange(LANES)[None, None, :]
    j = jnp.arange(GLA_SUB)[:, None, None]
    return (p == (d // GLA_KEY_DIM) * GLA_SUB + (GLA_SUB - 1) - j).astype(bf16)


def _prompt_mixer_kernel(rq, rk, rv, rg, mq, mk, mv, mo, dq, dk, dv, dz, lq, lk, lv, lz, sm_ref,
                         cos_ref, sin_ref, gb_ref, al_ref, wc_ref, cs_ref, mlnw_ref, gdnw_ref, glnw_ref,
                         w2_ref, glb_ref, seg_ref,
                         y_ref, ret_o, mlc_o, mln_o, mlm_o, gdn_o, conv_o, gla_o,
                         ret_s, mlc_s, mln_s, mlm_s, gdn_s, xb_s, vn_s, gla_s, *, tb):
    gw = GROUP_WIDTH
    _ret_prompt_body(rq, rk, rv, rg, cos_ref, sin_ref, y_ref, ret_o, ret_s, tb=tb, y_off=0)
    _mlstm_prompt_body(mq, mk, mv, mo, sm_ref, gb_ref, mlnw_ref, y_ref, mlc_o, mln_o, mlm_o,
                       mlc_s, mln_s, mlm_s, tb=tb, y_off=gw)
    _gdn_prompt_body(dq, dk, dv, dz, sm_ref, gb_ref, al_ref, wc_ref, cs_ref, gdnw_ref, y_ref, gdn_o, conv_o,
                     gdn_s, xb_s, vn_s, tb=tb, y_off=2 * gw)
    _gla_prompt_body(lq, lk, lv, lz, sm_ref, w2_ref, glb_ref, glnw_ref, seg_ref, y_ref, gla_o, gla_s,
                     tb=tb, y_off=3 * gw)


def _prompt_mixers(c_all, lp, *, batch, seq, conv_zero):
    tb = MIX_TB
    nt = seq // tb
    gw = GROUP_WIDTH
    kw = HEADS * GLA_KEY_DIM

    def cspec(name, width):
        blk = C_OFF[name] // width
        assert C_OFF[name] % width == 0
        return pl.BlockSpec((tb, width), lambda b, t: (b * nt + t, blk))

    def const2(shape):
        return pl.BlockSpec(shape, lambda b, t: (0, 0))

    def per_batch(shape):
        return pl.BlockSpec((1,) + shape, lambda b, t: (b,) + (0,) * len(shape))

    names = [("ret_q", gw), ("ret_k", gw), ("ret_v", gw), ("ret_g", gw),
             ("ml_q", gw), ("ml_k", gw), ("ml_v", gw), ("ml_o", gw),
             ("gd_q", gw), ("gd_k", gw), ("gd_v", gw), ("gd_z", gw),
             ("gl_q", kw), ("gl_k", kw), ("gl_v", gw), ("gl_z", gw), ("small", LANES)]
    in_specs = [cspec(n, w) for n, w in names] + [
        pl.BlockSpec((tb, HEAD_DIM), lambda b, t: (t, 0)), pl.BlockSpec((tb, HEAD_DIM), lambda b, t: (t, 0)),
        const2((1, LANES)), const2((1, LANES)), const2((CONV_W, 3 * gw)), per_batch((CONV_W - 1, 3 * gw)),
        const2((1, gw)), const2((1, HEAD_DIM)), const2((1, HEAD_DIM)), const2((LANES, kw)), const2((1, kw)),
        pl.BlockSpec((GLA_SUB, kw, LANES), lambda b, t: (0, 0, 0))]
    assert tb % LANES == 0
    state_shape = (HEADS, HEAD_DIM, HEAD_DIM)
    out_shapes = [(c_all.shape[0], D_MODEL), (batch,) + state_shape, (batch,) + state_shape,
                  (batch, HEADS, HEAD_DIM), (batch, SUBLANES, LANES), (batch,) + state_shape,
                  (batch, CONV_W - 1, 3 * gw), (batch, HEADS // 2, LANES, HEAD_DIM)]
    out_specs = [pl.BlockSpec((tb, D_MODEL), lambda b, t: (b * nt + t, 0))] + [
        per_batch(s[1:]) for s in out_shapes[1:]]
    outs = pl.pallas_call(
        functools.partial(_prompt_mixer_kernel, tb=tb),
        grid=(batch, nt),
        in_specs=in_specs,
        out_specs=out_specs,
        out_shape=[jax.ShapeDtypeStruct(out_shapes[0], bf16)] + [jax.ShapeDtypeStruct(s, f32) for s in out_shapes[1:]],
        scratch_shapes=[pltpu.VMEM(state_shape, f32), pltpu.VMEM(state_shape, f32),
                        pltpu.VMEM((SUBLANES, HEAD_DIM), f32), pltpu.VMEM((SUBLANES, LANES), f32),
                        pltpu.VMEM(state_shape, f32), pltpu.VMEM((3, tb + SUBLANES, gw), f32),
                        pltpu.VMEM((HEADS, tb, HEAD_DIM), bf16), pltpu.VMEM((HEADS // 2, LANES, HEAD_DIM), f32)],
        compiler_params=_cparams(("parallel", "arbitrary")),
        name="prompt_mixers",
    )(*([c_all] * len(names)), lp["cos_p"], lp["sin_p"], lp["gate_bias"], lp["alog_row"], lp["gd_conv"],
      conv_zero, lp["ml_norm"], lp["gd_norm"], lp["gl_norm"], lp["gl_w2p"], lp["gl_b"], _gla_segment_table())
    y, s_ret, s_c, s_n, s_m, s_gdn, s_conv, s_gla = outs
    states = {"ret": s_ret, "ml_C": s_c, "ml_n": s_n, "ml_m": s_m[:, :HEADS, 0],
              "gdn": s_gdn, "gdn_conv": s_conv,
              "gla": s_gla.reshape(batch, HEADS, GLA_KEY_DIM, HEAD_DIM)}
    return y, states


def _rows_pad(x):
    return jnp.concatenate([x, jnp.zeros((LANES - x.shape[0], LANES), f32)], axis=0)


def _cols(x):
    return _rows_pad(x).T


def _row(x, j):
    r = lax.broadcasted_iota(jnp.int32, (x.shape[0], 1), 0)
    return jnp.sum(jnp.where(r == j, x, 0.0), axis=0, keepdims=True)


def _outer(x_cols, j, y_rows):
    lane = lax.broadcasted_iota(jnp.int32, (1, LANES), 1)
    return jnp.dot(jnp.where(lane == j, x_cols, 0.0).astype(bf16), y_rows, preferred_element_type=f32)


def _sample_mixer_kernel(c_ref, cos_ref, sin_ref, gb_ref, al_ref, wc_ref, mlnw_ref, gdnw_ref, glnw_ref,
                         w2_ref, glb_ref,
                         ret_ref, mlc_ref, mln_ref, mlm_ref, gdn_ref, conv_ref, gla_ref, *rest):
    y_ref, ret_o, mlc_o, mln_o, mlm_o, gdn_o, conv_o, gla_o = rest[-8:]
    G = SAMPLE_G
    gw = GROUP_WIDTH
    dk = GLA_KEY_DIM

    def cblk(name, width):
        return c_ref[:, C_OFF[name]:C_OFF[name] + width]

    small = cblk("small", LANES)
    pre = small + gb_ref[...]
    cosf = cos_ref[...]
    sinf = sin_ref[...]

    heads = []
    for h in range(HEADS):
        hs = slice(h * HEAD_DIM, (h + 1) * HEAD_DIM)
        q = _rope(cblk("ret_q", gw)[:, hs], cosf, sinf)
        k = _rope(cblk("ret_k", gw)[:, hs], cosf, sinf) * HEAD_DIM ** -0.5
        heads.append(dict(kt=_cols(k), vpad=_rows_pad(cblk("ret_v", gw)[:, hs]).astype(bf16), qb=q.astype(bf16),
                          gamma=1.0 - 2.0 ** (-5.0 - h), rows=[]))
    for j in range(G):
        for h, hd in enumerate(heads):
            s_new = hd["gamma"] * ret_ref[j, h] + _outer(hd["kt"], j, hd["vpad"])
            ret_o[j, h] = s_new
            hd["rows"].append(_row(jnp.dot(hd["qb"], s_new.astype(bf16), preferred_element_type=f32), j))
    for h, hd in enumerate(heads):
        hs = slice(h * HEAD_DIM, (h + 1) * HEAD_DIM)
        o = jnp.concatenate(hd["rows"], axis=0)
        y = _rms_rows(o) * _silu(cblk("ret_g", gw)[:, hs])
        y_ref[:, 0 * gw + h * HEAD_DIM:0 * gw + (h + 1) * HEAD_DIM] = y.astype(y_ref.dtype)

    m_old = mlm_ref[...]
    f_al = pltpu.roll(jax.nn.log_sigmoid(pre), LANES - (L_MLF - L_MLI), 1)
    m_int = m_old + f_al
    m_t = jnp.maximum(m_int, pre)
    w_in = jnp.exp(pre - m_t)
    g_in = jnp.exp(m_int - m_t)
    e_neg = jnp.exp(-m_t)
    mlm_o[...] = m_t
    heads = []
    for h in range(HEADS):
        hs = slice(h * HEAD_DIM, (h + 1) * HEAD_DIM)
        q = cblk("ml_q", gw)[:, hs]
        kw = cblk("ml_k", gw)[:, hs] * HEAD_DIM ** -0.5 * w_in[:, h:h + 1]
        heads.append(dict(q=q, kw=kw, kt=_cols(kw), vpad=_rows_pad(cblk("ml_v", gw)[:, hs]).astype(bf16),
                          qb=q.astype(bf16), rows=[]))
    for j in range(G):
        for h, hd in enumerate(heads):
            gj = g_in[j:j + 1, h:h + 1]
            c_new = gj * mlc_ref[j, h] + _outer(hd["kt"], j, hd["vpad"])
            mlc_o[j, h] = c_new
            n_new = gj * mln_ref[j, h:h + 1, :] + hd["kw"][j:j + 1, :]
            mln_o[j, h:h + 1, :] = n_new
            num = _row(jnp.dot(hd["qb"], c_new.astype(bf16), preferred_element_type=f32), j)
            den = jnp.sum(hd["q"][j:j + 1, :] * n_new, axis=-1, keepdims=True)
            hd["rows"].append(num / jnp.maximum(jnp.abs(den), e_neg[j:j + 1, h:h + 1]))
    for h, hd in enumerate(heads):
        hs = slice(h * HEAD_DIM, (h + 1) * HEAD_DIM)
        hh = jnp.concatenate(hd["rows"], axis=0)
        y = _rms_rows(hh) * mlnw_ref[:, hs] * jax.nn.sigmoid(cblk("ml_o", gw)[:, hs])
        y_ref[:, 1 * gw + h * HEAD_DIM:1 * gw + (h + 1) * HEAD_DIM] = y.astype(y_ref.dtype)

    beta_all = jax.nn.sigmoid(pre)
    eg_all = jnp.exp(-jnp.exp(al_ref[...]) * jax.nn.softplus(pre))
    conv = []
    for p, name in enumerate(("gd_q", "gd_k", "gd_v")):
        ps = slice(p * gw, (p + 1) * gw)
        x = cblk(name, gw)
        acc = x * wc_ref[CONV_W - 1:CONV_W, ps]
        for j in range(CONV_W - 1):
            acc = acc + conv_ref[j, :, ps] * wc_ref[j:j + 1, ps]
        conv.append(_silu(acc))
        for j in range(CONV_W - 2):
            conv_o[j, :, ps] = conv_ref[j + 1, :, ps]
        conv_o[CONV_W - 2, :, ps] = x
    heads = []
    for h in range(HEADS):
        hs = slice(h * HEAD_DIM, (h + 1) * HEAD_DIM)
        q = _l2n_rows(conv[0][:, hs]) * HEAD_DIM ** -0.5
        k = _l2n_rows(conv[1][:, hs])
        heads.append(dict(kt=_cols(k), kb16=k.astype(bf16), qb=q.astype(bf16), v=conv[2][:, hs],
                          bcol=beta_all[:, L_GDB + h:L_GDB + h + 1], ecol=eg_all[:, L_GDA + h:L_GDA + h + 1],
                          ks=[], rows=[]))
    for j in range(G):
        for h, hd in enumerate(heads):
            hd["ks"].append(_row(jnp.dot(hd["kb16"], gdn_ref[j, h].astype(bf16), preferred_element_type=f32), j))
    for hd in heads:
        ks = jnp.concatenate(hd["ks"], axis=0)
        hd["vpad"] = _rows_pad(hd["bcol"] * (hd["v"] - hd["ecol"] * ks)).astype(bf16)
    for j in range(G):
        for h, hd in enumerate(heads):
            s_new = hd["ecol"][j:j + 1, :] * gdn_ref[j, h] + _outer(hd["kt"], j, hd["vpad"])
            gdn_o[j, h] = s_new
            hd["rows"].append(_row(jnp.dot(hd["qb"], s_new.astype(bf16), preferred_element_type=f32), j))
    for h, hd in enumerate(heads):
        hs = slice(h * HEAD_DIM, (h + 1) * HEAD_DIM)
        o = jnp.concatenate(hd["rows"], axis=0)
        y = _rms_rows(o) * gdnw_ref[...] * _silu(cblk("gd_z", gw)[:, hs])
        y_ref[:, 2 * gw + h * HEAD_DIM:2 * gw + (h + 1) * HEAD_DIM] = y.astype(y_ref.dtype)

    gk = jax.nn.log_sigmoid(_dot_exact(small, w2_ref[...]) + glb_ref[...]) * (1.0 / GLA_GATE_DENOM)
    eg = jnp.exp(gk)
    q_all = cblk("gl_q", HEADS * dk) * dk ** -0.5
    k_all = cblk("gl_k", HEADS * dk)
    upper = lax.broadcasted_iota(jnp.int32, (LANES, 1), 0) < dk
    lane = lax.broadcasted_iota(jnp.int32, (1, LANES), 1)
    pairs = []
    for p in range(HEADS // 2):
        ps = slice(p * LANES, (p + 1) * LANES)
        kp, qp = k_all[:, ps], q_all[:, ps]
        v0 = cblk("gl_v", gw)[:, (2 * p) * HEAD_DIM:(2 * p + 1) * HEAD_DIM]
        v1 = cblk("gl_v", gw)[:, (2 * p + 1) * HEAD_DIM:(2 * p + 2) * HEAD_DIM]
        pairs.append(dict(
            kt2=_cols(jnp.concatenate([kp, kp], axis=0)),
            et=_cols(eg[:, ps]),
            vpad=_rows_pad(jnp.concatenate([v0, v1], axis=0)).astype(bf16),
            q2=jnp.concatenate([jnp.where(lane < dk, qp, 0.0), jnp.where(lane < dk, 0.0, qp)],
                               axis=0).astype(bf16),
            rows=([], [])))
    for j in range(G):
        keep = lane == jnp.where(upper, j, G + j)
        for p, pd in enumerate(pairs):
            u = jnp.dot(jnp.where(keep, pd["kt2"], 0.0).astype(bf16), pd["vpad"], preferred_element_type=f32)
            s_new = pd["et"][:, j:j + 1] * gla_ref[j, p] + u
            gla_o[j, p] = s_new
            res = jnp.dot(pd["q2"], s_new.astype(bf16), preferred_element_type=f32)
            pd["rows"][0].append(_row(res, j))
            pd["rows"][1].append(_row(res, G + j))
    for p, pd in enumerate(pairs):
        for hh in range(2):
            h = 2 * p + hh
            hs = slice(h * HEAD_DIM, (h + 1) * HEAD_DIM)
            o = jnp.concatenate(pd["rows"][hh], axis=0)
            y = _rms_rows(o) * glnw_ref[...] * _silu(cblk("gl_z", gw)[:, hs])
            y_ref[:, 3 * gw + h * HEAD_DIM:3 * gw + (h + 1) * HEAD_DIM] = y.astype(y_ref.dtype)


def _sample_mixers(c_all, lp, st, y_prev, st_prev, *, layer, depth, row0, nb):
    G = SAMPLE_G
    gw = GROUP_WIDTH
    kw = HEADS * GLA_KEY_DIM
    assert row0 % G == 0 and nb % G == 0
    blk0 = row0 // G

    def const2(shape):
        return pl.BlockSpec(shape, lambda i: (0, 0))

    def slab(shape, batch_axis=0):
        def index(i):
            idx = [0] * len(shape)
            idx[batch_axis] = i
            return (layer,) + tuple(idx)
        return pl.BlockSpec((None,) + shape, index)

    state_specs = [slab((G, HEADS, HEAD_DIM, HEAD_DIM)), slab((G, HEADS, HEAD_DIM, HEAD_DIM)),
                   slab((G, HEADS, HEAD_DIM)), slab((G, LANES)), slab((G, HEADS, HEAD_DIM, HEAD_DIM)),
                   slab((CONV_W - 1, G, 3 * gw), batch_axis=1), slab((G, HEADS // 2, LANES, HEAD_DIM))]
    order = ("ret", "ml_C", "ml_n", "ml_m", "gdn", "gdn_conv", "gla")
    states = [st[n] for n in order]
    y_spec = pl.BlockSpec((G, D_MODEL), lambda i: (blk0 + i, 0))

    args = [c_all, lp["cos_s"], lp["sin_s"], lp["gate_bias"], lp["alog_row"], lp["gd_conv"], lp["ml_norm"],
            lp["gd_norm"], lp["gl_norm"], lp["gl_w2p"], lp["gl_b"]] + states
    in_specs = [pl.BlockSpec((G, C_WIDTH), lambda i: (blk0 + i, 0)),
                const2((1, HEAD_DIM)), const2((1, HEAD_DIM)), const2((1, LANES)), const2((1, LANES)),
                const2((CONV_W, 3 * gw)), const2((1, gw)), const2((1, HEAD_DIM)), const2((1, HEAD_DIM)),
                const2((LANES, kw)), const2((1, kw))] + state_specs
    inplace = [y_prev] + ([st_prev[n] for n in order] if st_prev is not None else [])
    aliases = {len(args) + k: k for k in range(len(inplace))}
    args += inplace
    in_specs += [pl.BlockSpec(memory_space=pl.ANY)] * len(inplace)

    outs = pl.pallas_call(
        _sample_mixer_kernel,
        grid=(nb // G,),
        in_specs=in_specs,
        out_specs=[y_spec] + state_specs,
        out_shape=[jax.ShapeDtypeStruct(y_prev.shape, y_prev.dtype)] + [
            jax.ShapeDtypeStruct(s.shape, s.dtype) for s in states],
        input_output_aliases=aliases,
        compiler_params=_cparams(("parallel",)),
        name="sample_mixers",
    )(*args)
    return outs[0], dict(zip(order, outs[1:]))


def _rope_tables(pos):
    half = HEAD_DIM // 2
    inv = ROPE_BASE ** (-jnp.arange(half, dtype=f32) / half)
    ang = pos[:, None] * inv[None, :]
    cos, sin = jnp.cos(ang), jnp.sin(ang)
    return jnp.concatenate([cos, cos], axis=-1), jnp.concatenate([-sin, sin], axis=-1)


def _permute_w_in(w):
    big = jnp.concatenate([w[..., 0:4096], w[..., 4104:6152], w[..., 6160:7696]], axis=-1)
    small = jnp.concatenate([w[..., 4096:4104], w[..., 6152:6160], w[..., 7696:7712]], axis=-1)
    pad = jnp.zeros(w.shape[:-1] + (C_WIDTH - big.shape[-1] - small.shape[-1],), w.dtype)
    return jnp.concatenate([big, small, pad], axis=-1)


W_IN_RAW = 7712
W_IN_TILE = 512
W_IN_SHIFTS = ((8, 8, 12), (16, 12, 15))
W_IN_EDGE = 16
W_IN_GATE_ROWS = (4096, 6144, 7696)


def _w_in_prep_kernel(a_ref, b_ref, c0_ref, c1_ref, c2_ref, o_ref):
    j = pl.program_id(1)
    k = a_ref.shape[1]

    def emit(x):
        o_ref[...] = x.T.astype(bf16)

    @pl.when(j < W_IN_SHIFTS[0][1])
    def _():
        emit(a_ref[...])

    for shift, lo, hi in W_IN_SHIFTS:
        @pl.when(jnp.logical_and(j >= lo, j < hi))
        def _():
            emit(jnp.concatenate([a_ref[shift:, :], b_ref[:shift, :]], axis=0))

    @pl.when(j >= W_IN_SHIFTS[-1][2])
    def _():
        gates = jnp.concatenate([c0_ref[:L_GDB, :], c1_ref[L_GDB:L_GLR, :], c2_ref[...]], axis=0)
        emit(jnp.concatenate([gates, jnp.zeros((W_IN_TILE - gates.shape[0], k), f32)], axis=0))


def _w_in_prep(w_in):
    depth, d, n = w_in.shape
    assert n == W_IN_RAW and C_OFF["small"] == W_IN_SHIFTS[-1][2] * W_IN_TILE
    assert L_GLR + GLA_RANK == 2 * W_IN_EDGE and n % W_IN_EDGE == 0
    w_t = jnp.swapaxes(w_in, 1, 2)
    last_wide = W_IN_SHIFTS[-1][2] - 1
    per_tile = W_IN_TILE // W_IN_EDGE
    last_edge = n // W_IN_EDGE - 1

    def edge(index):
        return pl.BlockSpec((None, W_IN_EDGE, d), index)

    g0, g1, g2 = (r // W_IN_EDGE for r in W_IN_GATE_ROWS)
    return pl.pallas_call(
        _w_in_prep_kernel,
        grid=(depth, C_WIDTH // W_IN_TILE),
        in_specs=[pl.BlockSpec((None, W_IN_TILE, d), lambda l, j: (l, jnp.minimum(j, last_wide), 0)),
                  edge(lambda l, j: (l, jnp.minimum(per_tile * (j + 1), last_edge), 0)),
                  edge(lambda l, j: (l, g0, 0)), edge(lambda l, j: (l, g1, 0)), edge(lambda l, j: (l, g2, 0))],
        out_specs=pl.BlockSpec((None, d, W_IN_TILE), lambda l, j: (l, 0, j)),
        out_shape=jax.ShapeDtypeStruct((depth, d, C_WIDTH), bf16),
        compiler_params=_cparams(("parallel", "arbitrary")),
        name="w_in_prep",
    )(w_t, w_t, w_t, w_t, w_t)


def _lane_row(pieces):
    row = jnp.zeros((1, LANES), f32)
    for off, val in pieces:
        row = lax.dynamic_update_slice(row, val.reshape(1, -1).astype(f32), (0, off))
    return row


def kernel(x_prompt, x_sample, state_ret, state_mlstm_C, state_mlstm_n, state_mlstm_m, state_gdn, state_gdn_conv, state_gla, cache_mem_k, cache_mem_v, mem_prompt, norm_mix_pre, norm_mix_post, w_in, ml_ib, ml_fb, ml_norm, gd_conv, gd_A_log, gd_dt_bias, gd_norm, gl_w2, gl_b, gl_norm, w_out, norm_x_pre, norm_x_post, norm_mem, w_xq, w_xk, w_xv, w_xo, norm_mlp_pre, norm_mlp_post, w_up, w_down):
    bp, tp, d = x_prompt.shape
    bs, ts, _ = x_sample.shape
    depth = w_in.shape[0]
    assert ts == 1 and d == D_MODEL and tp % MIX_TB == 0
    rows_p = bp * tp
    rows = rows_p + bs
    assert rows % ROW_TILE == 0 and rows_p % SAMPLE_G == 0

    xh = _embed(x_prompt.reshape(rows_p, d), norm_mix_pre[0], None, row0=0, rows=rows, tm=512)
    x, h = _embed(x_sample.reshape(bs, d), norm_mix_pre[0], xh, row0=rows_p, rows=rows, tm=bs)
    w_in_b = _w_in_prep(w_in)
    w_out_b, w_xo_b, w_down_b = w_out.astype(bf16), w_xo.astype(bf16), w_down.astype(bf16)
    cos_p, sin_p = _rope_tables(jnp.arange(tp, dtype=f32))
    cos_s, sin_s = _rope_tables(jnp.arange(ts, dtype=f32) + PAST_LEN)
    conv_zero = jnp.zeros((bp, CONV_W - 1, 3 * GROUP_WIDTH), f32)
    mem2d = mem_prompt.reshape(bp * MEM_LEN, d)
    st_in = {"ret": state_ret, "ml_C": state_mlstm_C, "ml_n": state_mlstm_n,
             "ml_m": jnp.pad(state_mlstm_m, ((0, 0), (0, 0), (0, LANES - HEADS))),
             "gdn": state_gdn, "gdn_conv": jnp.transpose(state_gdn_conv, (0, 2, 1, 3)),
             "gla": state_gla.reshape(depth, bs, HEADS // 2, LANES, HEAD_DIM)}

    new_p = {n: [] for n in ("ret", "ml_C", "ml_n", "ml_m", "gdn", "gdn_conv", "gla")}
    st_s = None
    mk_p = mv_p = None
    for l in range(depth):
        lp = {
            "cos_p": cos_p, "sin_p": sin_p, "cos_s": cos_s, "sin_s": sin_s,
            "gate_bias": _lane_row([(L_MLI, ml_ib[l]), (L_MLF, ml_fb[l]), (L_GDA, gd_dt_bias[l])]),
            "alog_row": _lane_row([(L_GDA, gd_A_log[l])]),
            "ml_norm": ml_norm[l].reshape(1, GROUP_WIDTH),
            "gd_norm": gd_norm[l].reshape(1, HEAD_DIM),
            "gl_norm": gl_norm[l].reshape(1, HEAD_DIM),
            "gd_conv": gd_conv[l],
            "gl_w2p": jnp.zeros((LANES, HEADS * GLA_KEY_DIM), f32).at[L_GLR:L_GLR + GLA_RANK].set(gl_w2[l]),
            "gl_b": gl_b[l].reshape(1, HEADS * GLA_KEY_DIM),
        }
        mk_p = _norm_matmul_slab(mem2d, norm_mem[l], w_xk, mk_p, layer=l, tm=512, tn=1024, name="mem_k")
        mv_p = _norm_matmul_slab(mem2d, norm_mem[l], w_xv, mv_p, layer=l, tm=512, tn=1024, name="mem_v")

        c_all = _matmul(h, w_in_b, layer=l, tm=ROW_TILE, tn=1024, out_dtype=f32, name="w_in")
        y_all, st_p = _prompt_mixers(c_all, lp, batch=bp, seq=tp, conv_zero=conv_zero)
        y_all, st_s = _sample_mixers(c_all, lp, st_in, y_all, st_s, layer=l, depth=depth, row0=rows_p, nb=bs)
        x, h = _matmul_norm_res(y_all, w_out_b, norm_mix_post[l], x, norm_x_pre[l], layer=l,
                                tm=ROW_TILE, tk=D_MODEL, name="w_out")

        q_all = _matmul(h, w_xq, layer=l, tm=ROW_TILE, tn=1024, out_dtype=bf16, name="w_xq")
        o_all = _xattn_prompt(q_all, mk_p, mv_p, layer=l, batch=bp, seq=tp, tq=512)
        o_all = _xattn_sample(q_all, cache_mem_k, cache_mem_v, o_all, layer=l, row0=rows_p, nb=bs)
        x, h = _matmul_norm_res(o_all, w_xo_b, norm_x_post[l], x, norm_mlp_pre[l], layer=l,
                                tm=ROW_TILE, tk=D_MODEL, name="w_xo")

        u = _matmul(h, w_up, layer=l, tm=ROW_TILE, tn=1024, out_dtype=bf16, act="relu2", name="w_up")
        g_next = norm_mix_pre[l + 1] if l + 1 < depth else None
        x, h = _matmul_norm_res(u, w_down_b, norm_mlp_post[l], x, g_next, layer=l,
                                tm=ROW_TILE, tk=1024, name="w_down")

        for n in new_p:
            new_p[n].append(st_p[n])

    def stk(lst):
        return jnp.stack(lst, axis=0)

    xp = x[:rows_p].reshape(bp, tp, d)
    xs = x[rows_p:].reshape(bs, ts, d)
    return (xp, xs,
            stk(new_p["ret"]), stk(new_p["ml_C"]), stk(new_p["ml_n"]), stk(new_p["ml_m"]),
            stk(new_p["gdn"]), stk(new_p["gdn_conv"]), stk(new_p["gla"]),
            mk_p.reshape(depth, bp, MEM_LEN, d), mv_p.reshape(depth, bp, MEM_LEN, d),
            st_s["ret"], st_s["ml_C"], st_s["ml_n"], st_s["ml_m"][:, :, :HEADS],
            st_s["gdn"], jnp.transpose(st_s["gdn_conv"], (0, 2, 1, 3)),
            st_s["gla"].reshape(depth, bs, HEADS, GLA_KEY_DIM, HEAD_DIM))
```

```python
import functools
import math

import jax
import jax.numpy as jnp
from jax import lax
from jax.experimental import pallas as pl
from jax.experimental.pallas import tpu as pltpu

f32 = jnp.float32
bf16 = jnp.bfloat16

D_MODEL = 2048
HEADS = 4
HEAD_DIM = 128
GROUP_WIDTH = 512
GLA_KEY_DIM = 64
GLA_RANK = 16
GLA_GATE_DENOM = 16.0
CONV_W = 4
CHUNK = 64
ROPE_BASE = 10000.0
X_HEADS = 4
X_HEAD_DIM = 512
MEM_LEN = 256
D_FF = 8192
EPS = 1e-6
PAST_LEN = 16384
NEG_INF = float("-inf")

LANES = 128
SUBLANES = 8

C_OFF = {
    "ret_q": 0, "ret_k": 512, "ret_v": 1024, "ret_g": 1536,
    "ml_q": 2048, "ml_k": 2560, "ml_v": 3072, "ml_o": 3584,
    "gd_q": 4096, "gd_k": 4608, "gd_v": 5120, "gd_z": 5632,
    "gl_q": 6144, "gl_k": 6400, "gl_v": 6656, "gl_z": 7168,
    "small": 7680,
}
C_WIDTH = 8192
L_MLI, L_MLF, L_GDB, L_GDA, L_GLR = 0, 4, 8, 12, 16

ROW_TILE = 640
ROW_TILE_WIDE = 1040
MIX_TB = 256
SAMPLE_G = 8
GLA_SUB = 16
XATTN_S = 2
VMEM_LIMIT = 56 * 1024 * 1024


def _cparams(sem):
    return pltpu.CompilerParams(dimension_semantics=sem, vmem_limit_bytes=VMEM_LIMIT)


def _dot(a, b):
    return jnp.dot(a.astype(bf16), b.astype(bf16), preferred_element_type=f32)


def _dot_nt(a, b):
    return lax.dot_general(a.astype(bf16), b.astype(bf16), (((1,), (1,)), ((), ())),
                           preferred_element_type=f32)


def _dot_tn(a, b):
    return lax.dot_general(a.astype(bf16), b.astype(bf16), (((0,), (0,)), ((), ())),
                           preferred_element_type=f32)


def _dot_exact(a, b):
    return jnp.dot(a, b, preferred_element_type=f32, precision=lax.Precision.HIGHEST)


def _mask_dot_f32(mask, x):
    return _dot_exact(jnp.where(mask, 1.0, 0.0), x)


def _rms_rows(x):
    return x * lax.rsqrt(jnp.mean(x * x, axis=-1, keepdims=True) + EPS)


def _l2n_rows(x):
    return x * lax.rsqrt(jnp.sum(x * x, axis=-1, keepdims=True) + EPS)


def _silu(x):
    return x * jax.nn.sigmoid(x)


def _rope(x, cosf, sinf):
    return x * cosf + pltpu.roll(x, HEAD_DIM // 2, 1) * sinf


def _chunk_masks(tb):
    r = lax.broadcasted_iota(jnp.int32, (tb, tb), 0)
    c = lax.broadcasted_iota(jnp.int32, (tb, tb), 1)
    same = (r // CHUNK) == (c // CHUNK)
    tril = jnp.logical_and(same, r >= c)
    strict = jnp.logical_and(same, r > c)
    return r, c, same, tril, strict


def _rms_gain(x, g):
    return x * lax.rsqrt(jnp.mean(x * x, axis=-1, keepdims=True) + EPS) * g


def _embed_kernel(x_in_ref, g_ref, *rest):
    x_ref, h_ref = rest[-2:]
    x = x_in_ref[...]
    x_ref[...] = x
    h_ref[...] = _rms_gain(x, g_ref[...]).astype(bf16)


def _embed(x2d, g, prev, *, row0, rows, tm):
    m, d = x2d.shape
    assert m % tm == 0 and row0 % tm == 0
    blk0 = row0 // tm
    args = [x2d, g.reshape(1, d)] + (list(prev) if prev is not None else [])
    out_spec = pl.BlockSpec((tm, d), lambda i: (blk0 + i, 0))
    return pl.pallas_call(
        _embed_kernel,
        grid=(m // tm,),
        in_specs=[pl.BlockSpec((tm, d), lambda i: (i, 0)), pl.BlockSpec((1, d), lambda i: (0, 0))] + (
            [pl.BlockSpec(memory_space=pl.ANY)] * 2 if prev is not None else []),
        out_specs=[out_spec, out_spec],
        out_shape=[jax.ShapeDtypeStruct((rows, d), f32), jax.ShapeDtypeStruct((rows, d), bf16)],
        input_output_aliases={2: 0, 3: 1} if prev is not None else {},
        compiler_params=_cparams(("parallel",)),
        name="embed",
    )(*args)


def _matmul_kernel(h_ref, w_ref, *rest, act, cast_w):
    if cast_w:
        o_ref, wb_ref = rest

        @pl.when(pl.program_id(1) == 0)
        def _():
            wb_ref[...] = w_ref[...].astype(bf16)

        w = wb_ref[...]
    else:
        (o_ref,) = rest
        w = w_ref[...]
    y = jnp.dot(h_ref[...], w, preferred_element_type=f32)
    if act == "relu2":
        y = jnp.square(jnp.maximum(y, 0.0))
    o_ref[...] = y.astype(o_ref.dtype)


def _matmul(h, w, *, layer, tm, tn, out_dtype, name, act=None):
    m, k = h.shape
    n = w.shape[2]
    assert m % tm == 0 and n % tn == 0
    cast_w = w.dtype != bf16
    return pl.pallas_call(
        functools.partial(_matmul_kernel, act=act, cast_w=cast_w),
        grid=(n // tn, m // tm),
        in_specs=[pl.BlockSpec((tm, k), lambda j, i: (i, 0)),
                  pl.BlockSpec((None, k, tn), lambda j, i: (layer, 0, j))],
        out_specs=pl.BlockSpec((tm, tn), lambda j, i: (i, j)),
        out_shape=jax.ShapeDtypeStruct((m, n), out_dtype),
        scratch_shapes=[pltpu.VMEM((k, tn), bf16)] if cast_w else [],
        compiler_params=_cparams(("parallel", "arbitrary")),
        name=name,
    )(h, w)


def _norm_matmul_kernel(x_ref, g_ref, w_ref, *rest):
    o_ref, h_ref = rest[-2:]

    @pl.when(pl.program_id(1) == 0)
    def _():
        h_ref[...] = _rms_gain(x_ref[...], g_ref[...]).astype(bf16)

    o_ref[...] = jnp.dot(h_ref[...], w_ref[...].astype(bf16), preferred_element_type=f32)


def _norm_matmul_slab(x, g, w, prev, *, layer, tm, tn, name):
    m, k = x.shape
    depth, _, n = w.shape
    assert m % tm == 0 and n % tn == 0
    in_specs = [pl.BlockSpec((tm, k), lambda i, j: (i, 0)),
                pl.BlockSpec((1, k), lambda i, j: (0, 0)),
                pl.BlockSpec((None, k, tn), lambda i, j: (layer, 0, j))]
    args = [x, g.reshape(1, k), w]
    aliases = {}
    if prev is not None:
        in_specs.append(pl.BlockSpec(memory_space=pl.ANY))
        args.append(prev)
        aliases = {len(args) - 1: 0}
    return pl.pallas_call(
        _norm_matmul_kernel,
        grid=(m // tm, n // tn),
        in_specs=in_specs,
        out_specs=pl.BlockSpec((None, tm, tn), lambda i, j: (layer, i, j)),
        out_shape=jax.ShapeDtypeStruct((depth, m, n), f32),
        scratch_shapes=[pltpu.VMEM((tm, k), bf16)],
        input_output_aliases=aliases,
        compiler_params=_cparams(("parallel", "arbitrary")),
        name=name,
    )(*args)


def _matmul_norm_res_kernel(a_ref, w_ref, g_ref, r_ref, *rest, next_norm, single_k):
    rest = list(rest)
    gn_ref = rest.pop(0) if next_norm else None
    o_ref = rest.pop(0)
    hn_ref = rest.pop(0) if next_norm else None

    def finish(y):
        x_new = r_ref[...] + _rms_gain(y, g_ref[...])
        o_ref[...] = x_new
        if next_norm:
            hn_ref[...] = _rms_gain(x_new, gn_ref[...]).astype(bf16)

    if single_k:
        finish(jnp.dot(a_ref[...], w_ref[...], preferred_element_type=f32))
        return

    (acc_ref,) = rest
    kk = pl.program_id(1)

    @pl.when(kk == 0)
    def _():
        acc_ref[...] = jnp.zeros_like(acc_ref)

    acc_ref[...] += jnp.dot(a_ref[...], w_ref[...], preferred_element_type=f32)

    @pl.when(kk == pl.num_programs(1) - 1)
    def _():
        finish(acc_ref[...])


def _matmul_norm_res(a, w, g, res, g_next, *, layer, tm, tk, name):
    m, k = a.shape
    n = w.shape[2]
    assert m % tm == 0 and k % tk == 0
    next_norm = g_next is not None
    row = pl.BlockSpec((tm, n), lambda i, j: (i, 0))
    vec = pl.BlockSpec((1, n), lambda i, j: (0, 0))
    in_specs = [pl.BlockSpec((tm, tk), lambda i, j: (i, j)),
                pl.BlockSpec((None, tk, n), lambda i, j: (layer, j, 0)), vec, row]
    args = [a, w, g.reshape(1, n), res]
    out_specs = [row]
    out_shape = [jax.ShapeDtypeStruct((m, n), f32)]
    if next_norm:
        in_specs.append(vec)
        args.append(g_next.reshape(1, n))
        out_specs.append(row)
        out_shape.append(jax.ShapeDtypeStruct((m, n), bf16))
    single_k = k == tk
    outs = pl.pallas_call(
        functools.partial(_matmul_norm_res_kernel, next_norm=next_norm, single_k=single_k),
        grid=(m // tm, k // tk),
        in_specs=in_specs,
        out_specs=out_specs,
        out_shape=out_shape,
        scratch_shapes=[] if single_k else [pltpu.VMEM((tm, n), f32)],
        compiler_params=_cparams(("parallel", "arbitrary")),
        name=name,
    )(*args)
    return (outs[0], outs[1]) if next_norm else (outs[0], None)


def _xattn_prompt_kernel(q_ref, k_ref, v_ref, o_ref, kb_ref, vb_ref):
    @pl.when(pl.program_id(1) == 0)
    def _():
        kb_ref[...] = k_ref[...].astype(bf16)
        vb_ref[...] = v_ref[...].astype(bf16)

    scale = X_HEAD_DIM ** -0.5
    for h in range(X_HEADS):
        sl = slice(h * X_HEAD_DIM, (h + 1) * X_HEAD_DIM)
        s = lax.dot_general(q_ref[:, sl], kb_ref[:, sl], (((1,), (1,)), ((), ())),
                            preferred_element_type=f32) * scale
        m = jnp.max(s, axis=-1, keepdims=True)
        p = jnp.exp(s - m)
        l = jnp.sum(p, axis=-1, keepdims=True)
        o = jnp.dot(p.astype(bf16), vb_ref[:, sl], preferred_element_type=f32) / l
        o_ref[:, sl] = o.astype(o_ref.dtype)


def _xattn_prompt(q_all, mk, mv, *, layer, batch, seq, tq):
    nt = seq // tq
    return pl.pallas_call(
        _xattn_prompt_kernel,
        grid=(batch, nt),
        in_specs=[pl.BlockSpec((tq, D_MODEL), lambda b, t: (b * nt + t, 0)),
                  pl.BlockSpec((None, MEM_LEN, D_MODEL), lambda b, t: (layer, b, 0)),
                  pl.BlockSpec((None, MEM_LEN, D_MODEL), lambda b, t: (layer, b, 0))],
        out_specs=pl.BlockSpec((tq, D_MODEL), lambda b, t: (b * nt + t, 0)),
        out_shape=jax.ShapeDtypeStruct(q_all.shape, bf16),
        scratch_shapes=[pltpu.VMEM((MEM_LEN, D_MODEL), bf16), pltpu.VMEM((MEM_LEN, D_MODEL), bf16)],
        compiler_params=_cparams(("parallel", "arbitrary")),
        name="xattn_prompt",
    )(q_all, mk, mv)


def _xattn_sample_kernel(q_ref, k_ref, v_ref, prev_ref, o_ref, acc_ref):
    del prev_ref
    j = pl.program_id(1)
    scale = X_HEAD_DIM ** -0.5
    rows = lax.broadcasted_iota(jnp.int32, (SAMPLE_G, 1), 0)
    q_all = q_ref[...].astype(f32)
    for s_i in range(XATTN_S):
        row = j * XATTN_S + s_i
        q = jnp.sum(jnp.where(rows == row, q_all, 0.0), axis=0, keepdims=True)
        for h in range(X_HEADS):
            sl = slice(h * X_HEAD_DIM, (h + 1) * X_HEAD_DIM)
            s = jnp.sum(k_ref[s_i, :, sl] * q[:, sl], axis=-1, keepdims=True) * scale
            m = jnp.max(s, axis=0, keepdims=True)
            p = jnp.exp(s - m)
            l = jnp.sum(p, axis=0, keepdims=True)
            acc_ref[pl.ds(row, 1), sl] = jnp.sum(p * v_ref[s_i, :, sl], axis=0, keepdims=True) / l

    @pl.when(j == pl.num_programs(1) - 1)
    def _():
        o_ref[...] = acc_ref[...].astype(o_ref.dtype)


def _xattn_sample(q_all, ck, cv, o_prev, *, layer, row0, nb):
    G = SAMPLE_G
    blk0 = row0 // G
    per_g = G // XATTN_S
    kv_spec = pl.BlockSpec((None, XATTN_S, MEM_LEN, D_MODEL), lambda g, j: (layer, g * per_g + j, 0, 0))
    return pl.pallas_call(
        _xattn_sample_kernel,
        grid=(nb // G, per_g),
        in_specs=[pl.BlockSpec((G, D_MODEL), lambda g, j: (blk0 + g, 0)), kv_spec, kv_spec,
                  pl.BlockSpec(memory_space=pl.ANY)],
        out_specs=pl.BlockSpec((G, D_MODEL), lambda g, j: (blk0 + g, 0)),
        out_shape=jax.ShapeDtypeStruct(o_prev.shape, o_prev.dtype),
        scratch_shapes=[pltpu.VMEM((G, D_MODEL), f32)],
        input_output_aliases={3: 0},
        compiler_params=_cparams(("parallel", "arbitrary")),
        name="xattn_sample",
    )(q_all, ck, cv, o_prev)


def _ret_prompt_body(q_ref, k_ref, v_ref, g_ref, cos_ref, sin_ref, y_ref, s_out, s_s, *, tb, y_off):
    t = pl.program_id(1)
    nc = tb // CHUNK

    @pl.when(t == 0)
    def _():
        s_s[...] = jnp.zeros_like(s_s)

    cosf = cos_ref[...]
    sinf = sin_ref[...]
    r, c, _, tril, _ = _chunk_masks(tb)
    diff = (r - c).astype(f32)
    ridx = (lax.broadcasted_iota(jnp.int32, (tb, 1), 0) % CHUNK).astype(f32)
    heads = []
    for h in range(HEADS):
        hs = slice(h * HEAD_DIM, (h + 1) * HEAD_DIM)
        lg = math.log(1.0 - 2.0 ** (-5.0 - h))
        decay = jnp.where(tril, jnp.exp(jnp.maximum(diff, 0.0) * lg), 0.0)
        qr = _rope(q_ref[:, hs], cosf, sinf)
        kr = _rope(k_ref[:, hs], cosf, sinf) * HEAD_DIM ** -0.5
        vb = v_ref[:, hs].astype(bf16)
        heads.append(dict(
            vb=vb, o_in=_dot(_dot_nt(qr, kr) * decay, vb),
            qd=(qr * jnp.exp((ridx + 1.0) * lg)).astype(bf16),
            kd=(kr * jnp.exp((CHUNK - 1.0 - ridx) * lg)).astype(bf16),
            cdec=math.exp(CHUNK * lg), st=s_s[h], outs=[]))
    yield
    for cc in range(nc):
        sl = slice(cc * CHUNK, (cc + 1) * CHUNK)
        for hd in heads:
            hd["outs"].append(hd["o_in"][sl] + _dot(hd["qd"][sl], hd["st"]))
            hd["st"] = hd["st"] * hd["cdec"] + _dot_tn(hd["kd"][sl], hd["vb"][sl])
        yield
    for h, hd in enumerate(heads):
        hs = slice(h * HEAD_DIM, (h + 1) * HEAD_DIM)
        s_s[h] = hd["st"]
        o = jnp.concatenate(hd["outs"], axis=0)
        y = _rms_rows(o) * _silu(g_ref[:, hs])
        y_ref[:, y_off + h * HEAD_DIM:y_off + (h + 1) * HEAD_DIM] = y.astype(y_ref.dtype)

    @pl.when(t == pl.num_programs(1) - 1)
    def _():
        s_out[0] = s_s[...]


def _mlstm_prompt_body(q_ref, k_ref, v_ref, og_ref, sm_ref, gb_ref, nw_ref,
                       y_ref, c_out, n_out, m_out, c_s, n_s, m_s, *, tb, y_off):
    t = pl.program_id(1)
    nc = tb // CHUNK

    @pl.when(t == 0)
    def _():
        c_s[...] = jnp.zeros_like(c_s)
        n_s[...] = jnp.zeros_like(n_s)
        m_s[...] = jnp.zeros_like(m_s)

    _, _, _, tril, _ = _chunk_masks(tb)
    pre = sm_ref[...] + gb_ref[...]
    fall = _mask_dot_f32(tril, jax.nn.log_sigmoid(pre))
    imf_t = (pre - pltpu.roll(fall, LANES - (L_MLF - L_MLI), 1)).T
    heads = []
    for h in range(HEADS):
        hs = slice(h * HEAD_DIM, (h + 1) * HEAD_DIM)
        fcol = fall[:, L_MLF + h:L_MLF + h + 1]
        icol = pre[:, L_MLI + h:L_MLI + h + 1]
        dm = jnp.where(tril, fcol + imf_t[h:h + 1, :], NEG_INF)
        dmax = jnp.max(dm, axis=-1, keepdims=True)
        mp = m_s[h:h + 1, 0:1]
        m_prev, m_int, m_cur = [], [], []
        for cc in range(nc):
            sl = slice(cc * CHUNK, (cc + 1) * CHUNK)
            mi = mp + fcol[sl]
            mt = jnp.maximum(mi, dmax[sl])
            m_prev.append(mp)
            m_int.append(mi)
            m_cur.append(mt)
            mp = mt[CHUNK - 1:CHUNK, :]
        heads.append(dict(fcol=fcol, icol=icol, dm=dm, m_last=mp, m_prev=m_prev, m_int=m_int, m_cur=m_cur))
    yield
    for h, hd in enumerate(heads):
        hs = slice(h * HEAD_DIM, (h + 1) * HEAD_DIM)
        w = jnp.exp(hd["dm"] - jnp.concatenate(hd["m_cur"], axis=0))
        qh = q_ref[:, hs]
        kh = k_ref[:, hs] * HEAD_DIM ** -0.5
        vb = v_ref[:, hs].astype(bf16)
        s = _dot_nt(qh, kh) * w
        hd.update(qh=qh, kh=kh, vb=vb, num_in=_dot(s, vb), den_in=jnp.sum(s, axis=-1, keepdims=True),
                  cst=c_s[h], nst=n_s[h:h + 1, :], outs=[])
    yield
    for cc in range(nc):
        sl = slice(cc * CHUNK, (cc + 1) * CHUNK)
        for hd in heads:
            mi, mt = hd["m_int"][cc], hd["m_cur"][cc]
            gi = jnp.exp(mi - mt)
            qc = hd["qh"][sl]
            num = hd["num_in"][sl] + gi * _dot(qc, hd["cst"])
            den = hd["den_in"][sl] + gi * jnp.sum(qc * hd["nst"], axis=-1, keepdims=True)
            hd["outs"].append(num / jnp.maximum(jnp.abs(den), jnp.exp(-mt)))
            m_new = mt[CHUNK - 1:CHUNK, :]
            fl = hd["fcol"][sl][CHUNK - 1:CHUNK, :]
            wk = jnp.exp(fl - hd["fcol"][sl] + hd["icol"][sl] - m_new)
            dec = jnp.exp(hd["m_prev"][cc] + fl - m_new)
            kw = hd["kh"][sl] * wk
            hd["cst"] = dec * hd["cst"] + _dot_tn(kw, hd["vb"][sl])
            hd["nst"] = dec * hd["nst"] + jnp.sum(kw, axis=0, keepdims=True)
        yield
    for h, hd in enumerate(heads):
        hs = slice(h * HEAD_DIM, (h + 1) * HEAD_DIM)
        c_s[h] = hd["cst"]
        n_s[h:h + 1, :] = hd["nst"]
        m_s[h:h + 1, :] = jnp.broadcast_to(hd["m_last"], (1, LANES))
        hh = jnp.concatenate(hd["outs"], axis=0)
        y = _rms_rows(hh) * nw_ref[:, hs] * jax.nn.sigmoid(og_ref[:, hs])
        y_ref[:, y_off + h * HEAD_DIM:y_off + (h + 1) * HEAD_DIM] = y.astype(y_ref.dtype)

    @pl.when(t == pl.num_programs(1) - 1)
    def _():
        c_out[0] = c_s[...]
        n_out[0] = n_s[0:HEADS, :]
        m_out[0] = m_s[...]


def _unit_lower_inverse_minus_eye(a_list, r, c):
    def coupling(s):
        return jnp.logical_and((r // (2 * s)) == (c // (2 * s)), (r // s) != (c // s))

    eye = r == c
    first = coupling(1)
    xs = [jnp.where(eye, 1.0, jnp.where(first, -a, 0.0)) for a in a_list]
    s = 2
    while s < CHUNK:
        couple = coupling(s)
        ys = [_dot(jnp.where(couple, a, 0.0), x) for a, x in zip(a_list, xs)]
        xs = [x - _dot(x, y) for x, y in zip(xs, ys)]
        s *= 2
        yield
    return [jnp.where(eye, 0.0, x) for x in xs]


def _gdn_prompt_body(q_ref, k_ref, v_ref, z_ref, sm_ref, gb_ref, al_ref, wc_ref, cs_ref, nw_ref,
                     y_ref, s_out, conv_out, s_s, xb_s, vn_s, *, tb, y_off):
    t = pl.program_id(1)
    nc = tb // CHUNK
    gw = GROUP_WIDTH
    pad = SUBLANES
    hist = CONV_W - 1

    @pl.when(t == 0)
    def _():
        s_s[...] = jnp.zeros_like(s_s)
        for p in range(3):
            xb_s[p, 0:pad, :] = jnp.zeros((pad, gw), f32)
            xb_s[p, pad - hist:pad, :] = cs_ref[0, :, p * gw:(p + 1) * gw]

    @pl.when(t > 0)
    def _():
        for p in range(3):
            xb_s[p, 0:pad, :] = xb_s[p, tb:tb + pad, :]

    xb_s[0, pad:, :] = q_ref[...]
    xb_s[1, pad:, :] = k_ref[...]
    xb_s[2, pad:, :] = v_ref[...]

    conv = []
    for p in range(3):
        acc = xb_s[p, pad - hist:pad - hist + tb, :] * wc_ref[0:1, p * gw:(p + 1) * gw]
        for j in range(1, CONV_W):
            acc = acc + xb_s[p, pad - hist + j:pad - hist + j + tb, :] * wc_ref[j:j + 1, p * gw:(p + 1) * gw]
        conv.append(_silu(acc))

    r, c, _, tril, strict = _chunk_masks(tb)
    pre = sm_ref[...] + gb_ref[...]
    beta_all = jax.nn.sigmoid(pre)
    g_all = -jnp.exp(al_ref[...]) * jax.nn.softplus(pre)
    gall = _mask_dot_f32(tril, g_all)
    gall_t = gall.T
    heads = []
    for h in range(HEADS):
        hs = slice(h * HEAD_DIM, (h + 1) * HEAD_DIM)
        gcol = gall[:, L_GDA + h:L_GDA + h + 1]
        grow = gall_t[L_GDA + h:L_GDA + h + 1, :]
        bcol = beta_all[:, L_GDB + h:L_GDB + h + 1]
        decay = jnp.where(tril, jnp.exp(jnp.where(tril, gcol - grow, 0.0)), 0.0)
        qh = _l2n_rows(conv[0][:, hs]) * HEAD_DIM ** -0.5
        kh = _l2n_rows(conv[1][:, hs])
        kb = kh * bcol
        egc = jnp.exp(gcol)
        heads.append(dict(
            gcol=gcol, kh=kh, qe=qh * egc,
            a=jnp.where(strict, _dot_nt(kb, kh) * decay, 0.0),
            rhs=jnp.concatenate([conv[2][:, hs] * bcol, kb * egc], axis=1),
            attn=(_dot_nt(qh, kh) * decay).astype(bf16)))
    yield
    inv = yield from _unit_lower_inverse_minus_eye([hd["a"] for hd in heads], r, c)
    for h, hd in enumerate(heads):
        x = hd["rhs"] + _dot(inv[h], hd["rhs"])
        hd.update(u=x[:, :HEAD_DIM], w=x[:, HEAD_DIM:], st=s_s[h], outs=[])
        vn_s[h] = jnp.zeros((tb, HEAD_DIM), bf16)
    yield
    for cc in range(nc):
        sl = slice(cc * CHUNK, (cc + 1) * CHUNK)
        for h, hd in enumerate(heads):
            st, gcol = hd["st"], hd["gcol"]
            vnew = hd["u"][sl] - _dot(hd["w"][sl], st)
            vn_s[h, sl, :] = vnew.astype(bf16)
            hd["outs"].append(_dot(hd["qe"][sl], st)
                              + jnp.dot(hd["attn"][sl, :], vn_s[h], preferred_element_type=f32))
            gl = gcol[sl][CHUNK - 1:CHUNK, :]
            kdec = hd["kh"][sl] * jnp.exp(gl - gcol[sl])
            hd["st"] = st * jnp.exp(gl) + _dot_tn(kdec, vnew)
        yield
    for h, hd in enumerate(heads):
        hs = slice(h * HEAD_DIM, (h + 1) * HEAD_DIM)
        s_s[h] = hd["st"]
        o = jnp.concatenate(hd["outs"], axis=0)
        y = _rms_rows(o) * nw_ref[...] * _silu(z_ref[:, hs])
        y_ref[:, y_off + h * HEAD_DIM:y_off + (h + 1) * HEAD_DIM] = y.astype(y_ref.dtype)

    @pl.when(t == pl.num_programs(1) - 1)
    def _():
        s_out[0] = s_s[...]
        for p in range(3):
            conv_out[0, :, p * gw:(p + 1) * gw] = xb_s[p, tb + pad - hist:tb + pad, :]


def _gla_prompt_body(q_ref, k_ref, v_ref, z_ref, sm_ref, w2_ref, b_ref, nw_ref, seg_ref,
                     y_ref, s_out, s_s, *, tb, y_off):
    t = pl.program_id(1)
    nc = tb // CHUNK
    dk = GLA_KEY_DIM
    sub = GLA_SUB
    nsub = CHUNK // sub

    @pl.when(t == 0)
    def _():
        s_s[...] = jnp.zeros_like(s_s)

    r, c, same, tril, _ = _chunk_masks(tb)
    pre = _dot_exact(sm_ref[...], w2_ref[...]) + b_ref[...]
    gk = jax.nn.log_sigmoid(pre) * (1.0 / GLA_GATE_DENOM)
    g = _mask_dot_f32(tril, gk)
    q = q_ref[...] * dk ** -0.5
    k = k_ref[...]
    rpos = lax.broadcasted_iota(jnp.int32, (tb, 1), 0) % CHUNK
    lane = lax.broadcasted_iota(jnp.int32, (1, LANES), 1)
    lane_lo = lane < dk

    a_off = [jnp.zeros((tb, tb), f32) for _ in range(HEADS)]
    g3 = g.reshape(nc, CHUNK, HEADS * dk)
    for i in range(1, nsub):
        ri = jnp.broadcast_to(g3[:, sub * i:sub * i + 1, :], (nc, CHUNK, HEADS * dk)).reshape(tb, HEADS * dk)
        qs = q * jnp.exp(jnp.where(rpos // sub == i, g - ri, NEG_INF))
        ks = k * jnp.exp(jnp.where(rpos < sub * i, ri - g, NEG_INF))
        for h in range(HEADS):
            ps = slice((h // 2) * LANES, (h // 2 + 1) * LANES)
            keep = lane_lo if h % 2 == 0 else jnp.logical_not(lane_lo)
            a_off[h] = a_off[h] + _dot_nt(jnp.where(keep, qs[:, ps], 0.0), ks[:, ps])

    yield
    band = jnp.zeros((tb, LANES), f32)
    for j in range(sub):
        ksh = k if j == 0 else pltpu.roll(k, j, 0)
        gsh = g if j == 0 else pltpu.roll(g, j, 0)
        valid = (rpos % sub) >= j
        prod = q * ksh * jnp.exp(jnp.where(valid, g - gsh, NEG_INF))
        band = band + jnp.dot(prod.astype(bf16), seg_ref[j], preferred_element_type=f32)
    if tb > LANES:
        band = jnp.concatenate([band, jnp.zeros((tb, tb - LANES), f32)], axis=1)
    in_sub = jnp.logical_and((r // sub) == (c // sub), r >= c)
    a_diag = []
    for h in range(HEADS):
        shift = (tb - (sub - 1) - sub * h) % tb
        a_diag.append(jnp.where(in_sub, pltpu.roll(band, shift, 1, stride=1, stride_axis=0), 0.0))

    yield
    eg = jnp.exp(g)
    qe = q * eg
    g_t = g.T
    vbs = [v_ref[:, h * HEAD_DIM:(h + 1) * HEAD_DIM].astype(bf16) for h in range(HEADS)]
    o_in = [_dot(a_diag[h] + jnp.where(same, a_off[h], 0.0), vbs[h]) for h in range(HEADS)]
    sts = [s_s[p] for p in range(HEADS // 2)]
    outs = [[] for _ in range(HEADS)]
    for cc in range(nc):
        sl = slice(cc * CHUNK, (cc + 1) * CHUNK)
        last = cc * CHUNK + CHUNK - 1
        for p in range(HEADS // 2):
            ps = slice(p * LANES, (p + 1) * LANES)
            gl_row = g[last:last + 1, ps]
            gl_col = g_t[ps, last:last + 1]
            kdec = k[sl, ps] * jnp.exp(gl_row - g[sl, ps])
            upd = jnp.zeros((LANES, HEAD_DIM), f32)
            for hh in range(2):
                h = 2 * p + hh
                keep = lane_lo if hh == 0 else jnp.logical_not(lane_lo)
                outs[h].append(o_in[h][sl] + _dot(jnp.where(keep, qe[sl, ps], 0.0), sts[p]))
                upd = upd + _dot_tn(jnp.where(keep, kdec, 0.0), vbs[h][sl])
            sts[p] = sts[p] * jnp.exp(gl_col) + upd
        yield
    for p in range(HEADS // 2):
        s_s[p] = sts[p]
    for h in range(HEADS):
        hs = slice(h * HEAD_DIM, (h + 1) * HEAD_DIM)
        o = jnp.concatenate(outs[h], axis=0)
        y = _rms_rows(o) * nw_ref[...] * _silu(z_ref[:, hs])
        y_ref[:, y_off + h * HEAD_DIM:y_off + (h + 1) * HEAD_DIM] = y.astype(y_ref.dtype)

    @pl.when(t == pl.num_programs(1) - 1)
    def _():
        s_out[0] = s_s[...]


def _gla_segment_table():
    d = jnp.arange(HEADS * GLA_KEY_DIM)[None, :, None]
    p = jnp.arange(LANES)[None, None, :]
    j = jnp.arange(GLA_SUB)[:, None, None]
    return (p == (d // GLA_KEY_DIM) * GLA_SUB + (GLA_SUB - 1) - j).astype(bf16)


def _prompt_mixer_kernel(rq, rk, rv, rg, mq, mk, mv, mo, dq, dk, dv, dz, lq, lk, lv, lz, sm_ref,
                         cos_ref, sin_ref, gb_ref, al_ref, wc_ref, cs_ref, mlnw_ref, gdnw_ref, glnw_ref,
                         w2_ref, glb_ref, seg_ref,
                         y_ref, ret_o, mlc_o, mln_o, mlm_o, gdn_o, conv_o, gla_o,
                         ret_s, mlc_s, mln_s, mlm_s, gdn_s, xb_s, vn_s, gla_s, *, tb):
    gw = GROUP_WIDTH
    bodies = [
        _ret_prompt_body(rq, rk, rv, rg, cos_ref, sin_ref, y_ref, ret_o, ret_s, tb=tb, y_off=0),
        _mlstm_prompt_body(mq, mk, mv, mo, sm_ref, gb_ref, mlnw_ref, y_ref, mlc_o, mln_o, mlm_o,
                           mlc_s, mln_s, mlm_s, tb=tb, y_off=gw),
        _gdn_prompt_body(dq, dk, dv, dz, sm_ref, gb_ref, al_ref, wc_ref, cs_ref, gdnw_ref, y_ref, gdn_o, conv_o,
                         gdn_s, xb_s, vn_s, tb=tb, y_off=2 * gw),
        _gla_prompt_body(lq, lk, lv, lz, sm_ref, w2_ref, glb_ref, glnw_ref, seg_ref, y_ref, gla_o, gla_s,
                         tb=tb, y_off=3 * gw)]
    while bodies:
        for body in list(bodies):
            if next(body, StopIteration) is StopIteration:
                bodies.remove(body)


def _prompt_mixers(c_all, lp, *, batch, seq, conv_zero):
    tb = MIX_TB
    nt = seq // tb
    gw = GROUP_WIDTH
    kw = HEADS * GLA_KEY_DIM

    def cspec(name, width):
        blk = C_OFF[name] // width
        assert C_OFF[name] % width == 0
        return pl.BlockSpec((tb, width), lambda b, t: (b * nt + t, blk))

    def const2(shape):
        return pl.BlockSpec(shape, lambda b, t: (0, 0))

    def per_batch(shape):
        return pl.BlockSpec((1,) + shape, lambda b, t: (b,) + (0,) * len(shape))

    names = [("ret_q", gw), ("ret_k", gw), ("ret_v", gw), ("ret_g", gw),
             ("ml_q", gw), ("ml_k", gw), ("ml_v", gw), ("ml_o", gw),
             ("gd_q", gw), ("gd_k", gw), ("gd_v", gw), ("gd_z", gw),
             ("gl_q", kw), ("gl_k", kw), ("gl_v", gw), ("gl_z", gw), ("small", LANES)]
    in_specs = [cspec(n, w) for n, w in names] + [
        pl.BlockSpec((tb, HEAD_DIM), lambda b, t: (t, 0)), pl.BlockSpec((tb, HEAD_DIM), lambda b, t: (t, 0)),
        const2((1, LANES)), const2((1, LANES)), const2((CONV_W, 3 * gw)), per_batch((CONV_W - 1, 3 * gw)),
        const2((1, gw)), const2((1, HEAD_DIM)), const2((1, HEAD_DIM)), const2((LANES, kw)), const2((1, kw)),
        pl.BlockSpec((GLA_SUB, kw, LANES), lambda b, t: (0, 0, 0))]
    assert tb % LANES == 0
    state_shape = (HEADS, HEAD_DIM, HEAD_DIM)
    out_shapes = [(c_all.shape[0], D_MODEL), (batch,) + state_shape, (batch,) + state_shape,
                  (batch, HEADS, HEAD_DIM), (batch, SUBLANES, LANES), (batch,) + state_shape,
                  (batch, CONV_W - 1, 3 * gw), (batch, HEADS // 2, LANES, HEAD_DIM)]
    out_specs = [pl.BlockSpec((tb, D_MODEL), lambda b, t: (b * nt + t, 0))] + [
        per_batch(s[1:]) for s in out_shapes[1:]]
    outs = pl.pallas_call(
        functools.partial(_prompt_mixer_kernel, tb=tb),
        grid=(batch, nt),
        in_specs=in_specs,
        out_specs=out_specs,
        out_shape=[jax.ShapeDtypeStruct(out_shapes[0], bf16)] + [jax.ShapeDtypeStruct(s, f32) for s in out_shapes[1:]],
        scratch_shapes=[pltpu.VMEM(state_shape, f32), pltpu.VMEM(state_shape, f32),
                        pltpu.VMEM((SUBLANES, HEAD_DIM), f32), pltpu.VMEM((SUBLANES, LANES), f32),
                        pltpu.VMEM(state_shape, f32), pltpu.VMEM((3, tb + SUBLANES, gw), f32),
                        pltpu.VMEM((HEADS, tb, HEAD_DIM), bf16), pltpu.VMEM((HEADS // 2, LANES, HEAD_DIM), f32)],
        compiler_params=_cparams(("parallel", "arbitrary")),
        name="prompt_mixers",
    )(*([c_all] * len(names)), lp["cos_p"], lp["sin_p"], lp["gate_bias"], lp["alog_row"], lp["gd_conv"],
      conv_zero, lp["ml_norm"], lp["gd_norm"], lp["gl_norm"], lp["gl_w2p"], lp["gl_b"], _gla_segment_table())
    y, s_ret, s_c, s_n, s_m, s_gdn, s_conv, s_gla = outs
    states = {"ret": s_ret, "ml_C": s_c, "ml_n": s_n, "ml_m": s_m[:, :HEADS, 0],
              "gdn": s_gdn, "gdn_conv": s_conv,
              "gla": s_gla.reshape(batch, HEADS, GLA_KEY_DIM, HEAD_DIM)}
    return y, states


def _rows_pad(x):
    return jnp.concatenate([x, jnp.zeros((LANES - x.shape[0], LANES), f32)], axis=0)


def _cols(x):
    return _rows_pad(x).T


def _row(x, j):
    r = lax.broadcasted_iota(jnp.int32, (x.shape[0], 1), 0)
    return jnp.sum(jnp.where(r == j, x, 0.0), axis=0, keepdims=True)


def _outer(x_cols, j, y_rows):
    lane = lax.broadcasted_iota(jnp.int32, (1, LANES), 1)
    return jnp.dot(jnp.where(lane == j, x_cols, 0.0).astype(bf16), y_rows, preferred_element_type=f32)


def _sample_mixer_kernel(c_ref, cos_ref, sin_ref, gb_ref, al_ref, wc_ref, mlnw_ref, gdnw_ref, glnw_ref,
                         w2_ref, glb_ref,
                         ret_ref, mlc_ref, mln_ref, mlm_ref, gdn_ref, conv_ref, gla_ref, *rest):
    y_ref, ret_o, mlc_o, mln_o, mlm_o, gdn_o, conv_o, gla_o = rest[-8:]
    G = SAMPLE_G
    gw = GROUP_WIDTH
    dk = GLA_KEY_DIM

    def cblk(name, width):
        return c_ref[:, C_OFF[name]:C_OFF[name] + width]

    small = cblk("small", LANES)
    pre = small + gb_ref[...]
    cosf = cos_ref[...]
    sinf = sin_ref[...]

    heads = []
    for h in range(HEADS):
        hs = slice(h * HEAD_DIM, (h + 1) * HEAD_DIM)
        q = _rope(cblk("ret_q", gw)[:, hs], cosf, sinf)
        k = _rope(cblk("ret_k", gw)[:, hs], cosf, sinf) * HEAD_DIM ** -0.5
        heads.append(dict(kt=_cols(k), vpad=_rows_pad(cblk("ret_v", gw)[:, hs]).astype(bf16), qb=q.astype(bf16),
                          gamma=1.0 - 2.0 ** (-5.0 - h), rows=[]))
    for j in range(G):
        for h, hd in enumerate(heads):
            s_new = hd["gamma"] * ret_ref[j, h] + _outer(hd["kt"], j, hd["vpad"])
            ret_o[j, h] = s_new
            hd["rows"].append(_row(jnp.dot(hd["qb"], s_new.astype(bf16), preferred_element_type=f32), j))
    for h, hd in enumerate(heads):
        hs = slice(h * HEAD_DIM, (h + 1) * HEAD_DIM)
        o = jnp.concatenate(hd["rows"], axis=0)
        y = _rms_rows(o) * _silu(cblk("ret_g", gw)[:, hs])
        y_ref[:, 0 * gw + h * HEAD_DIM:0 * gw + (h + 1) * HEAD_DIM] = y.astype(y_ref.dtype)

    m_old = mlm_ref[...]
    f_al = pltpu.roll(jax.nn.log_sigmoid(pre), LANES - (L_MLF - L_MLI), 1)
    m_int = m_old + f_al
    m_t = jnp.maximum(m_int, pre)
    w_in = jnp.exp(pre - m_t)
    g_in = jnp.exp(m_int - m_t)
    e_neg = jnp.exp(-m_t)
    mlm_o[...] = m_t
    heads = []
    for h in range(HEADS):
        hs = slice(h * HEAD_DIM, (h + 1) * HEAD_DIM)
        q = cblk("ml_q", gw)[:, hs]
        kw = cblk("ml_k", gw)[:, hs] * HEAD_DIM ** -0.5 * w_in[:, h:h + 1]
        heads.append(dict(q=q, kw=kw, kt=_cols(kw), vpad=_rows_pad(cblk("ml_v", gw)[:, hs]).astype(bf16),
                          qb=q.astype(bf16), rows=[]))
    for j in range(G):
        for h, hd in enumerate(heads):
            gj = g_in[j:j + 1, h:h + 1]
            c_new = gj * mlc_ref[j, h] + _outer(hd["kt"], j, hd["vpad"])
            mlc_o[j, h] = c_new
            n_new = gj * mln_ref[j, h:h + 1, :] + hd["kw"][j:j + 1, :]
            mln_o[j, h:h + 1, :] = n_new
            num = _row(jnp.dot(hd["qb"], c_new.astype(bf16), preferred_element_type=f32), j)
            den = jnp.sum(hd["q"][j:j + 1, :] * n_new, axis=-1, keepdims=True)
            hd["rows"].append(num / jnp.maximum(jnp.abs(den), e_neg[j:j + 1, h:h + 1]))
    for h, hd in enumerate(heads):
        hs = slice(h * HEAD_DIM, (h + 1) * HEAD_DIM)
        hh = jnp.concatenate(hd["rows"], axis=0)
        y = _rms_rows(hh) * mlnw_ref[:, hs] * jax.nn.sigmoid(cblk("ml_o", gw)[:, hs])
        y_ref[:, 1 * gw + h * HEAD_DIM:1 * gw + (h + 1) * HEAD_DIM] = y.astype(y_ref.dtype)

    beta_all = jax.nn.sigmoid(pre)
    eg_all = jnp.exp(-jnp.exp(al_ref[...]) * jax.nn.softplus(pre))
    conv = []
    for p, name in enumerate(("gd_q", "gd_k", "gd_v")):
        ps = slice(p * gw, (p + 1) * gw)
        x = cblk(name, gw)
        acc = x * wc_ref[CONV_W - 1:CONV_W, ps]
        for j in range(CONV_W - 1):
            acc = acc + conv_ref[j, :, ps] * wc_ref[j:j + 1, ps]
        conv.append(_silu(acc))
        for j in range(CONV_W - 2):
            conv_o[j, :, ps] = conv_ref[j + 1, :, ps]
        conv_o[CONV_W - 2, :, ps] = x
    heads = []
    for h in range(HEADS):
        hs = slice(h * HEAD_DIM, (h + 1) * HEAD_DIM)
        q = _l2n_rows(conv[0][:, hs]) * HEAD_DIM ** -0.5
        k = _l2n_rows(conv[1][:, hs])
        heads.append(dict(kt=_cols(k), kb16=k.astype(bf16), qb=q.astype(bf16), v=conv[2][:, hs],
                          bcol=beta_all[:, L_GDB + h:L_GDB + h + 1], ecol=eg_all[:, L_GDA + h:L_GDA + h + 1],
                          ks=[], rows=[]))
    for j in range(G):
        for h, hd in enumerate(heads):
            hd["ks"].append(_row(jnp.dot(hd["kb16"], gdn_ref[j, h].astype(bf16), preferred_element_type=f32), j))
    for hd in heads:
        ks = jnp.concatenate(hd["ks"], axis=0)
        hd["vpad"] = _rows_pad(hd["bcol"] * (hd["v"] - hd["ecol"] * ks)).astype(bf16)
    for j in range(G):
        for h, hd in enumerate(heads):
            s_new = hd["ecol"][j:j + 1, :] * gdn_ref[j, h] + _outer(hd["kt"], j, hd["vpad"])
            gdn_o[j, h] = s_new
            hd["rows"].append(_row(jnp.dot(hd["qb"], s_new.astype(bf16), preferred_element_type=f32), j))
    for h, hd in enumerate(heads):
        hs = slice(h * HEAD_DIM, (h + 1) * HEAD_DIM)
        o = jnp.concatenate(hd["rows"], axis=0)
        y = _rms_rows(o) * gdnw_ref[...] * _silu(cblk("gd_z", gw)[:, hs])
        y_ref[:, 2 * gw + h * HEAD_DIM:2 * gw + (h + 1) * HEAD_DIM] = y.astype(y_ref.dtype)

    gk = jax.nn.log_sigmoid(_dot_exact(small, w2_ref[...]) + glb_ref[...]) * (1.0 / GLA_GATE_DENOM)
    eg = jnp.exp(gk)
    q_all = cblk("gl_q", HEADS * dk) * dk ** -0.5
    k_all = cblk("gl_k", HEADS * dk)
    upper = lax.broadcasted_iota(jnp.int32, (LANES, 1), 0) < dk
    lane = lax.broadcasted_iota(jnp.int32, (1, LANES), 1)
    pairs = []
    for p in range(HEADS // 2):
        ps = slice(p * LANES, (p + 1) * LANES)
        kp, qp = k_all[:, ps], q_all[:, ps]
        v0 = cblk("gl_v", gw)[:, (2 * p) * HEAD_DIM:(2 * p + 1) * HEAD_DIM]
        v1 = cblk("gl_v", gw)[:, (2 * p + 1) * HEAD_DIM:(2 * p + 2) * HEAD_DIM]
        pairs.append(dict(
            kt2=_cols(jnp.concatenate([kp, kp], axis=0)),
            et=_cols(eg[:, ps]),
            vpad=_rows_pad(jnp.concatenate([v0, v1], axis=0)).astype(bf16),
            q2=jnp.concatenate([jnp.where(lane < dk, qp, 0.0), jnp.where(lane < dk, 0.0, qp)],
                               axis=0).astype(bf16),
            rows=([], [])))
    for j in range(G):
        keep = lane == jnp.where(upper, j, G + j)
        for p, pd in enumerate(pairs):
            u = jnp.dot(jnp.where(keep, pd["kt2"], 0.0).astype(bf16), pd["vpad"], preferred_element_type=f32)
            s_new = pd["et"][:, j:j + 1] * gla_ref[j, p] + u
            gla_o[j, p] = s_new
            res = jnp.dot(pd["q2"], s_new.astype(bf16), preferred_element_type=f32)
            pd["rows"][0].append(_row(res, j))
            pd["rows"][1].append(_row(res, G + j))
    for p, pd in enumerate(pairs):
        for hh in range(2):
            h = 2 * p + hh
            hs = slice(h * HEAD_DIM, (h + 1) * HEAD_DIM)
            o = jnp.concatenate(pd["rows"][hh], axis=0)
            y = _rms_rows(o) * glnw_ref[...] * _silu(cblk("gl_z", gw)[:, hs])
            y_ref[:, 3 * gw + h * HEAD_DIM:3 * gw + (h + 1) * HEAD_DIM] = y.astype(y_ref.dtype)


def _sample_mixers(c_all, lp, st, y_prev, st_prev, *, layer, depth, row0, nb):
    G = SAMPLE_G
    gw = GROUP_WIDTH
    kw = HEADS * GLA_KEY_DIM
    assert row0 % G == 0 and nb % G == 0
    blk0 = row0 // G

    def const2(shape):
        return pl.BlockSpec(shape, lambda i: (0, 0))

    def slab(shape, batch_axis=0):
        def index(i):
            idx = [0] * len(shape)
            idx[batch_axis] = i
            return (layer,) + tuple(idx)
        return pl.BlockSpec((None,) + shape, index)

    state_specs = [slab((G, HEADS, HEAD_DIM, HEAD_DIM)), slab((G, HEADS, HEAD_DIM, HEAD_DIM)),
                   slab((G, HEADS, HEAD_DIM)), slab((G, LANES)), slab((G, HEADS, HEAD_DIM, HEAD_DIM)),
                   slab((CONV_W - 1, G, 3 * gw), batch_axis=1), slab((G, HEADS // 2, LANES, HEAD_DIM))]
    order = ("ret", "ml_C", "ml_n", "ml_m", "gdn", "gdn_conv", "gla")
    states = [st[n] for n in order]
    y_spec = pl.BlockSpec((G, D_MODEL), lambda i: (blk0 + i, 0))

    args = [c_all, lp["cos_s"], lp["sin_s"], lp["gate_bias"], lp["alog_row"], lp["gd_conv"], lp["ml_norm"],
            lp["gd_norm"], lp["gl_norm"], lp["gl_w2p"], lp["gl_b"]] + states
    in_specs = [pl.BlockSpec((G, C_WIDTH), lambda i: (blk0 + i, 0)),
                const2((1, HEAD_DIM)), const2((1, HEAD_DIM)), const2((1, LANES)), const2((1, LANES)),
                const2((CONV_W, 3 * gw)), const2((1, gw)), const2((1, HEAD_DIM)), const2((1, HEAD_DIM)),
                const2((LANES, kw)), const2((1, kw))] + state_specs
    inplace = [y_prev] + ([st_prev[n] for n in order] if st_prev is not None else [])
    aliases = {len(args) + k: k for k in range(len(inplace))}
    args += inplace
    in_specs += [pl.BlockSpec(memory_space=pl.ANY)] * len(inplace)

    outs = pl.pallas_call(
        _sample_mixer_kernel,
        grid=(nb // G,),
        in_specs=in_specs,
        out_specs=[y_spec] + state_specs,
        out_shape=[jax.ShapeDtypeStruct(y_prev.shape, y_prev.dtype)] + [
            jax.ShapeDtypeStruct(s.shape, s.dtype) for s in states],
        input_output_aliases=aliases,
        compiler_params=_cparams(("parallel",)),
        name="sample_mixers",
    )(*args)
    return outs[0], dict(zip(order, outs[1:]))


def _rope_tables(pos):
    half = HEAD_DIM // 2
    inv = ROPE_BASE ** (-jnp.arange(half, dtype=f32) / half)
    ang = pos[:, None] * inv[None, :]
    cos, sin = jnp.cos(ang), jnp.sin(ang)
    return jnp.concatenate([cos, cos], axis=-1), jnp.concatenate([-sin, sin], axis=-1)


def _permute_w_in(w):
    big = jnp.concatenate([w[..., 0:4096], w[..., 4104:6152], w[..., 6160:7696]], axis=-1)
    small = jnp.concatenate([w[..., 4096:4104], w[..., 6152:6160], w[..., 7696:7712]], axis=-1)
    pad = jnp.zeros(w.shape[:-1] + (C_WIDTH - big.shape[-1] - small.shape[-1],), w.dtype)
    return jnp.concatenate([big, small, pad], axis=-1)


W_IN_RAW = 7712
W_IN_TILE = 512
W_IN_SHIFTS = ((8, 8, 12), (16, 12, 15))
W_IN_EDGE = 16
W_IN_GATE_ROWS = (4096, 6144, 7696)


def _w_in_prep_kernel(a_ref, b_ref, c0_ref, c1_ref, c2_ref, o_ref):
    j = pl.program_id(1)
    k = a_ref.shape[1]

    def emit(x):
        o_ref[...] = x.T.astype(bf16)

    @pl.when(j < W_IN_SHIFTS[0][1])
    def _():
        emit(a_ref[...])

    for shift, lo, hi in W_IN_SHIFTS:
        @pl.when(jnp.logical_and(j >= lo, j < hi))
        def _():
            emit(jnp.concatenate([a_ref[shift:, :], b_ref[:shift, :]], axis=0))

    @pl.when(j >= W_IN_SHIFTS[-1][2])
    def _():
        gates = jnp.concatenate([c0_ref[:L_GDB, :], c1_ref[L_GDB:L_GLR, :], c2_ref[...]], axis=0)
        emit(jnp.concatenate([gates, jnp.zeros((W_IN_TILE - gates.shape[0], k), f32)], axis=0))


def _w_in_prep(w_in):
    depth, d, n = w_in.shape
    assert n == W_IN_RAW and C_OFF["small"] == W_IN_SHIFTS[-1][2] * W_IN_TILE
    assert L_GLR + GLA_RANK == 2 * W_IN_EDGE and n % W_IN_EDGE == 0
    w_t = jnp.swapaxes(w_in, 1, 2)
    last_wide = W_IN_SHIFTS[-1][2] - 1
    per_tile = W_IN_TILE // W_IN_EDGE
    last_edge = n // W_IN_EDGE - 1

    def edge(index):
        return pl.BlockSpec((None, W_IN_EDGE, d), index)

    g0, g1, g2 = (r // W_IN_EDGE for r in W_IN_GATE_ROWS)
    return pl.pallas_call(
        _w_in_prep_kernel,
        grid=(depth, C_WIDTH // W_IN_TILE),
        in_specs=[pl.BlockSpec((None, W_IN_TILE, d), lambda l, j: (l, jnp.minimum(j, last_wide), 0)),
                  edge(lambda l, j: (l, jnp.minimum(per_tile * (j + 1), last_edge), 0)),
                  edge(lambda l, j: (l, g0, 0)), edge(lambda l, j: (l, g1, 0)), edge(lambda l, j: (l, g2, 0))],
        out_specs=pl.BlockSpec((None, d, W_IN_TILE), lambda l, j: (l, 0, j)),
        out_shape=jax.ShapeDtypeStruct((depth, d, C_WIDTH), bf16),
        compiler_params=_cparams(("parallel", "arbitrary")),
        name="w_in_prep",
    )(w_t, w_t, w_t, w_t, w_t)


def _lane_row(pieces):
    row = jnp.zeros((1, LANES), f32)
    for off, val in pieces:
        row = lax.dynamic_update_slice(row, val.reshape(1, -1).astype(f32), (0, off))
    return row


def kernel(x_prompt, x_sample, state_ret, state_mlstm_C, state_mlstm_n, state_mlstm_m, state_gdn, state_gdn_conv, state_gla, cache_mem_k, cache_mem_v, mem_prompt, norm_mix_pre, norm_mix_post, w_in, ml_ib, ml_fb, ml_norm, gd_conv, gd_A_log, gd_dt_bias, gd_norm, gl_w2, gl_b, gl_norm, w_out, norm_x_pre, norm_x_post, norm_mem, w_xq, w_xk, w_xv, w_xo, norm_mlp_pre, norm_mlp_post, w_up, w_down):
    bp, tp, d = x_prompt.shape
    bs, ts, _ = x_sample.shape
    depth = w_in.shape[0]
    assert ts == 1 and d == D_MODEL and tp % MIX_TB == 0
    rows_p = bp * tp
    rows = rows_p + bs
    assert rows % ROW_TILE == 0 and rows % ROW_TILE_WIDE == 0 and rows_p % SAMPLE_G == 0

    xh = _embed(x_prompt.reshape(rows_p, d), norm_mix_pre[0], None, row0=0, rows=rows, tm=512)
    x, h = _embed(x_sample.reshape(bs, d), norm_mix_pre[0], xh, row0=rows_p, rows=rows, tm=bs)
    w_in_b = _w_in_prep(w_in)
    w_out_b, w_xo_b, w_down_b = w_out.astype(bf16), w_xo.astype(bf16), w_down.astype(bf16)
    cos_p, sin_p = _rope_tables(jnp.arange(tp, dtype=f32))
    cos_s, sin_s = _rope_tables(jnp.arange(ts, dtype=f32) + PAST_LEN)
    conv_zero = jnp.zeros((bp, CONV_W - 1, 3 * GROUP_WIDTH), f32)
    mem2d = mem_prompt.reshape(bp * MEM_LEN, d)
    st_in = {"ret": state_ret, "ml_C": state_mlstm_C, "ml_n": state_mlstm_n,
             "ml_m": jnp.pad(state_mlstm_m, ((0, 0), (0, 0), (0, LANES - HEADS))),
             "gdn": state_gdn, "gdn_conv": jnp.transpose(state_gdn_conv, (0, 2, 1, 3)),
             "gla": state_gla.reshape(depth, bs, HEADS // 2, LANES, HEAD_DIM)}

    new_p = {n: [] for n in ("ret", "ml_C", "ml_n", "ml_m", "gdn", "gdn_conv", "gla")}
    st_s = None
    mk_p = mv_p = None
    for l in range(depth):
        lp = {
            "cos_p": cos_p, "sin_p": sin_p, "cos_s": cos_s, "sin_s": sin_s,
            "gate_bias": _lane_row([(L_MLI, ml_ib[l]), (L_MLF, ml_fb[l]), (L_GDA, gd_dt_bias[l])]),
            "alog_row": _lane_row([(L_GDA, gd_A_log[l])]),
            "ml_norm": ml_norm[l].reshape(1, GROUP_WIDTH),
            "gd_norm": gd_norm[l].reshape(1, HEAD_DIM),
            "gl_norm": gl_norm[l].reshape(1, HEAD_DIM),
            "gd_conv": gd_conv[l],
            "gl_w2p": jnp.zeros((LANES, HEADS * GLA_KEY_DIM), f32).at[L_GLR:L_GLR + GLA_RANK].set(gl_w2[l]),
            "gl_b": gl_b[l].reshape(1, HEADS * GLA_KEY_DIM),
        }
        mk_p = _norm_matmul_slab(mem2d, norm_mem[l], w_xk, mk_p, layer=l, tm=512, tn=1024, name="mem_k")
        mv_p = _norm_matmul_slab(mem2d, norm_mem[l], w_xv, mv_p, layer=l, tm=512, tn=1024, name="mem_v")

        c_all = _matmul(h, w_in_b, layer=l, tm=ROW_TILE_WIDE, tn=2048, out_dtype=f32, name="w_in")
        y_all, st_p = _prompt_mixers(c_all, lp, batch=bp, seq=tp, conv_zero=conv_zero)
        y_all, st_s = _sample_mixers(c_all, lp, st_in, y_all, st_s, layer=l, depth=depth, row0=rows_p, nb=bs)
        x, h = _matmul_norm_res(y_all, w_out_b, norm_mix_post[l], x, norm_x_pre[l], layer=l,
                                tm=ROW_TILE, tk=D_MODEL, name="w_out")

        q_all = _matmul(h, w_xq, layer=l, tm=ROW_TILE_WIDE, tn=1024, out_dtype=bf16, name="w_xq")
        o_all = _xattn_prompt(q_all, mk_p, mv_p, layer=l, batch=bp, seq=tp, tq=512)
        o_all = _xattn_sample(q_all, cache_mem_k, cache_mem_v, o_all, layer=l, row0=rows_p, nb=bs)
        x, h = _matmul_norm_res(o_all, w_xo_b, norm_x_post[l], x, norm_mlp_pre[l], layer=l,
                                tm=ROW_TILE, tk=D_MODEL, name="w_xo")

        u = _matmul(h, w_up, layer=l, tm=ROW_TILE_WIDE, tn=1024, out_dtype=bf16, act="relu2", name="w_up")
        g_next = norm_mix_pre[l + 1] if l + 1 < depth else None
        x, h = _matmul_norm_res(u, w_down_b, norm_mlp_post[l], x, g_next, layer=l,
                                tm=ROW_TILE, tk=2048, name="w_down")

        for n in new_p:
            new_p[n].append(st_p[n])

    def stk(lst):
        return jnp.stack(lst, axis=0)

    xp = x[:rows_p].reshape(bp, tp, d)
    xs = x[rows_p:].reshape(bs, ts, d)
    return (xp, xs,
            stk(new_p["ret"]), stk(new_p["ml_C"]), stk(new_p["ml_n"]), stk(new_p["ml_m"]),
            stk(new_p["gdn"]), stk(new_p["gdn_conv"]), stk(new_p["gla"]),
            mk_p.reshape(depth, bp, MEM_LEN, d), mv_p.reshape(depth, bp, MEM_LEN, d),
            st_s["ret"], st_s["ml_C"], st_s["ml_n"], st_s["ml_m"][:, :, :HEADS],
            st_s["gdn"], jnp.transpose(st_s["gdn_conv"], (0, 2, 1, 3)),
            st_s["gla"].reshape(depth, bs, HEADS, GLA_KEY_DIM, HEAD_DIM))
```

```python
import functools
import math

import jax
import jax.numpy as jnp
from jax import lax
from jax.experimental import pallas as pl
from jax.experimental.pallas import tpu as pltpu

f32 = jnp.float32
bf16 = jnp.bfloat16

D_MODEL = 2048
HEADS = 4
HEAD_DIM = 128
GROUP_WIDTH = 512
GLA_KEY_DIM = 64
GLA_RANK = 16
GLA_GATE_DENOM = 16.0
CONV_W = 4
CHUNK = 64
ROPE_BASE = 10000.0
X_HEADS = 4
X_HEAD_DIM = 512
MEM_LEN = 256
D_FF = 8192
EPS = 1e-6
PAST_LEN = 16384
NEG_INF = float("-inf")

LANES = 128
SUBLANES = 8

C_OFF = {
    "ret_q": 0, "ret_k": 512, "ret_v": 1024, "ret_g": 1536,
    "ml_q": 2048, "ml_k": 2560, "ml_v": 3072, "ml_o": 3584,
    "gd_q": 4096, "gd_k": 4608, "gd_v": 5120, "gd_z": 5632,
    "gl_q": 6144, "gl_k": 6400, "gl_v": 6656, "gl_z": 7168,
    "small": 7680,
}
C_WIDTH = 8192
L_MLI, L_MLF, L_GDB, L_GDA, L_GLR = 0, 4, 8, 12, 16

ROW_TILE = 640
ROW_TILE_WIDE = 1040
MIX_TB = 256
SAMPLE_G = 8
GLA_SUB = 16
XATTN_S = 2
VMEM_LIMIT = 56 * 1024 * 1024


def _cparams(sem):
    return pltpu.CompilerParams(dimension_semantics=sem, vmem_limit_bytes=VMEM_LIMIT)


def _dot(a, b):
    return jnp.dot(a.astype(bf16), b.astype(bf16), preferred_element_type=f32)


def _dot_nt(a, b):
    return lax.dot_general(a.astype(bf16), b.astype(bf16), (((1,), (1,)), ((), ())),
                           preferred_element_type=f32)


def _dot_tn(a, b):
    return lax.dot_general(a.astype(bf16), b.astype(bf16), (((0,), (0,)), ((), ())),
                           preferred_element_type=f32)


def _dot_exact(a, b):
    return jnp.dot(a, b, preferred_element_type=f32, precision=lax.Precision.HIGHEST)


def _mask_dot_f32(mask, x):
    return _dot_exact(jnp.where(mask, 1.0, 0.0), x)


def _rms_rows(x):
    return x * lax.rsqrt(jnp.mean(x * x, axis=-1, keepdims=True) + EPS)


def _l2n_rows(x):
    return x * lax.rsqrt(jnp.sum(x * x, axis=-1, keepdims=True) + EPS)


def _silu(x):
    return x * jax.nn.sigmoid(x)


def _rope(x, cosf, sinf):
    return x * cosf + pltpu.roll(x, HEAD_DIM // 2, 1) * sinf


def _chunk_masks(tb):
    r = lax.broadcasted_iota(jnp.int32, (tb, tb), 0)
    c = lax.broadcasted_iota(jnp.int32, (tb, tb), 1)
    same = (r // CHUNK) == (c // CHUNK)
    tril = jnp.logical_and(same, r >= c)
    strict = jnp.logical_and(same, r > c)
    return r, c, same, tril, strict


def _rms_gain(x, g):
    return x * lax.rsqrt(jnp.mean(x * x, axis=-1, keepdims=True) + EPS) * g


def _embed_kernel(x_in_ref, g_ref, *rest):
    x_ref, h_ref = rest[-2:]
    x = x_in_ref[...]
    x_ref[...] = x
    h_ref[...] = _rms_gain(x, g_ref[...]).astype(bf16)


def _embed(x2d, g, prev, *, row0, rows, tm):
    m, d = x2d.shape
    assert m % tm == 0 and row0 % tm == 0
    blk0 = row0 // tm
    args = [x2d, g.reshape(1, d)] + (list(prev) if prev is not None else [])
    out_spec = pl.BlockSpec((tm, d), lambda i: (blk0 + i, 0))
    return pl.pallas_call(
        _embed_kernel,
        grid=(m // tm,),
        in_specs=[pl.BlockSpec((tm, d), lambda i: (i, 0)), pl.BlockSpec((1, d), lambda i: (0, 0))] + (
            [pl.BlockSpec(memory_space=pl.ANY)] * 2 if prev is not None else []),
        out_specs=[out_spec, out_spec],
        out_shape=[jax.ShapeDtypeStruct((rows, d), f32), jax.ShapeDtypeStruct((rows, d), bf16)],
        input_output_aliases={2: 0, 3: 1} if prev is not None else {},
        compiler_params=_cparams(("parallel",)),
        name="embed",
    )(*args)


def _matmul_kernel(h_ref, w_ref, *rest, act, cast_w):
    if cast_w:
        o_ref, wb_ref = rest

        @pl.when(pl.program_id(1) == 0)
        def _():
            wb_ref[...] = w_ref[...].astype(bf16)

        w = wb_ref[...]
    else:
        (o_ref,) = rest
        w = w_ref[...]
    y = jnp.dot(h_ref[...], w, preferred_element_type=f32)
    if act == "relu2":
        y = jnp.square(jnp.maximum(y, 0.0))
    o_ref[...] = y.astype(o_ref.dtype)


def _matmul(h, w, *, layer, tm, tn, out_dtype, name, act=None):
    m, k = h.shape
    n = w.shape[2]
    assert m % tm == 0 and n % tn == 0
    cast_w = w.dtype != bf16
    return pl.pallas_call(
        functools.partial(_matmul_kernel, act=act, cast_w=cast_w),
        grid=(n // tn, m // tm),
        in_specs=[pl.BlockSpec((tm, k), lambda j, i: (i, 0)),
                  pl.BlockSpec((None, k, tn), lambda j, i: (layer, 0, j))],
        out_specs=pl.BlockSpec((tm, tn), lambda j, i: (i, j)),
        out_shape=jax.ShapeDtypeStruct((m, n), out_dtype),
        scratch_shapes=[pltpu.VMEM((k, tn), bf16)] if cast_w else [],
        compiler_params=_cparams(("parallel", "arbitrary")),
        name=name,
    )(h, w)


def _norm_matmul_kernel(x_ref, g_ref, w_ref, *rest):
    o_ref, h_ref = rest[-2:]

    @pl.when(pl.program_id(1) == 0)
    def _():
        h_ref[...] = _rms_gain(x_ref[...], g_ref[...]).astype(bf16)

    o_ref[...] = jnp.dot(h_ref[...], w_ref[...].astype(bf16), preferred_element_type=f32)


def _norm_matmul_slab(x, g, w, prev, *, layer, tm, tn, name):
    m, k = x.shape
    depth, _, n = w.shape
    assert m % tm == 0 and n % tn == 0
    in_specs = [pl.BlockSpec((tm, k), lambda i, j: (i, 0)),
                pl.BlockSpec((1, k), lambda i, j: (0, 0)),
                pl.BlockSpec((None, k, tn), lambda i, j: (layer, 0, j))]
    args = [x, g.reshape(1, k), w]
    aliases = {}
    if prev is not None:
        in_specs.append(pl.BlockSpec(memory_space=pl.ANY))
        args.append(prev)
        aliases = {len(args) - 1: 0}
    return pl.pallas_call(
        _norm_matmul_kernel,
        grid=(m // tm, n // tn),
        in_specs=in_specs,
        out_specs=pl.BlockSpec((None, tm, tn), lambda i, j: (layer, i, j)),
        out_shape=jax.ShapeDtypeStruct((depth, m, n), f32),
        scratch_shapes=[pltpu.VMEM((tm, k), bf16)],
        input_output_aliases=aliases,
        compiler_params=_cparams(("parallel", "arbitrary")),
        name=name,
    )(*args)


def _matmul_norm_res_kernel(a_ref, w_ref, g_ref, r_ref, *rest, next_norm, single_k):
    rest = list(rest)
    gn_ref = rest.pop(0) if next_norm else None
    o_ref = rest.pop(0)
    hn_ref = rest.pop(0) if next_norm else None

    tm = a_ref.shape[0]
    halves = (slice(0, tm // 2), slice(tm // 2, tm))

    def finish(y, rows):
        x_new = r_ref[rows, :] + _rms_gain(y, g_ref[...])
        o_ref[rows, :] = x_new
        if next_norm:
            hn_ref[rows, :] = _rms_gain(x_new, gn_ref[...]).astype(bf16)

    if single_k:
        for rows in halves:
            finish(jnp.dot(a_ref[rows, :], w_ref[...], preferred_element_type=f32), rows)
        return

    (acc_ref,) = rest
    kk = pl.program_id(1)
    last = pl.num_programs(1) - 1

    @pl.when(kk == 0)
    def _():
        acc_ref[...] = jnp.zeros_like(acc_ref)

    @pl.when(kk < last)
    def _():
        acc_ref[...] += jnp.dot(a_ref[...], w_ref[...], preferred_element_type=f32)

    @pl.when(kk == last)
    def _():
        for rows in halves:
            finish(acc_ref[rows, :] + jnp.dot(a_ref[rows, :], w_ref[...], preferred_element_type=f32), rows)


def _matmul_norm_res(a, w, g, res, g_next, *, layer, tm, tk, name):
    m, k = a.shape
    n = w.shape[2]
    assert m % tm == 0 and k % tk == 0
    next_norm = g_next is not None
    row = pl.BlockSpec((tm, n), lambda i, j: (i, 0))
    vec = pl.BlockSpec((1, n), lambda i, j: (0, 0))
    in_specs = [pl.BlockSpec((tm, tk), lambda i, j: (i, j)),
                pl.BlockSpec((None, tk, n), lambda i, j: (layer, j, 0)), vec, row]
    args = [a, w, g.reshape(1, n), res]
    out_specs = [row]
    out_shape = [jax.ShapeDtypeStruct((m, n), f32)]
    if next_norm:
        in_specs.append(vec)
        args.append(g_next.reshape(1, n))
        out_specs.append(row)
        out_shape.append(jax.ShapeDtypeStruct((m, n), bf16))
    single_k = k == tk
    outs = pl.pallas_call(
        functools.partial(_matmul_norm_res_kernel, next_norm=next_norm, single_k=single_k),
        grid=(m // tm, k // tk),
        in_specs=in_specs,
        out_specs=out_specs,
        out_shape=out_shape,
        scratch_shapes=[] if single_k else [pltpu.VMEM((tm, n), f32)],
        compiler_params=_cparams(("parallel", "arbitrary")),
        name=name,
    )(*args)
    return (outs[0], outs[1]) if next_norm else (outs[0], None)


def _xattn_prompt_kernel(q_ref, k_ref, v_ref, o_ref, kb_ref, vb_ref):
    @pl.when(pl.program_id(1) == 0)
    def _():
        kb_ref[...] = k_ref[...].astype(bf16)
        vb_ref[...] = v_ref[...].astype(bf16)

    scale = X_HEAD_DIM ** -0.5
    for h in range(X_HEADS):
        sl = slice(h * X_HEAD_DIM, (h + 1) * X_HEAD_DIM)
        s = lax.dot_general(q_ref[:, sl], kb_ref[:, sl], (((1,), (1,)), ((), ())),
                            preferred_element_type=f32) * scale
        m = jnp.max(s, axis=-1, keepdims=True)
        p = jnp.exp(s - m)
        l = jnp.sum(p, axis=-1, keepdims=True)
        o = jnp.dot(p.astype(bf16), vb_ref[:, sl], preferred_element_type=f32) / l
        o_ref[:, sl] = o.astype(o_ref.dtype)


def _xattn_prompt(q_all, mk, mv, *, layer, batch, seq, tq):
    nt = seq // tq
    return pl.pallas_call(
        _xattn_prompt_kernel,
        grid=(batch, nt),
        in_specs=[pl.BlockSpec((tq, D_MODEL), lambda b, t: (b * nt + t, 0)),
                  pl.BlockSpec((None, MEM_LEN, D_MODEL), lambda b, t: (layer, b, 0)),
                  pl.BlockSpec((None, MEM_LEN, D_MODEL), lambda b, t: (layer, b, 0))],
        out_specs=pl.BlockSpec((tq, D_MODEL), lambda b, t: (b * nt + t, 0)),
        out_shape=jax.ShapeDtypeStruct(q_all.shape, bf16),
        scratch_shapes=[pltpu.VMEM((MEM_LEN, D_MODEL), bf16), pltpu.VMEM((MEM_LEN, D_MODEL), bf16)],
        compiler_params=_cparams(("parallel", "arbitrary")),
        name="xattn_prompt",
    )(q_all, mk, mv)


def _xattn_sample_kernel(q_ref, k_ref, v_ref, prev_ref, o_ref, acc_ref):
    del prev_ref
    j = pl.program_id(1)
    scale = X_HEAD_DIM ** -0.5
    rows = lax.broadcasted_iota(jnp.int32, (SAMPLE_G, 1), 0)
    q_all = q_ref[...].astype(f32)
    for s_i in range(XATTN_S):
        row = j * XATTN_S + s_i
        q = jnp.sum(jnp.where(rows == row, q_all, 0.0), axis=0, keepdims=True)
        for h in range(X_HEADS):
            sl = slice(h * X_HEAD_DIM, (h + 1) * X_HEAD_DIM)
            s = jnp.sum(k_ref[s_i, :, sl] * q[:, sl], axis=-1, keepdims=True) * scale
            m = jnp.max(s, axis=0, keepdims=True)
            p = jnp.exp(s - m)
            l = jnp.sum(p, axis=0, keepdims=True)
            acc_ref[pl.ds(row, 1), sl] = jnp.sum(p * v_ref[s_i, :, sl], axis=0, keepdims=True) / l

    @pl.when(j == pl.num_programs(1) - 1)
    def _():
        o_ref[...] = acc_ref[...].astype(o_ref.dtype)


def _xattn_sample(q_all, ck, cv, o_prev, *, layer, row0, nb):
    G = SAMPLE_G
    blk0 = row0 // G
    per_g = G // XATTN_S
    kv_spec = pl.BlockSpec((None, XATTN_S, MEM_LEN, D_MODEL), lambda g, j: (layer, g * per_g + j, 0, 0))
    return pl.pallas_call(
        _xattn_sample_kernel,
        grid=(nb // G, per_g),
        in_specs=[pl.BlockSpec((G, D_MODEL), lambda g, j: (blk0 + g, 0)), kv_spec, kv_spec,
                  pl.BlockSpec(memory_space=pl.ANY)],
        out_specs=pl.BlockSpec((G, D_MODEL), lambda g, j: (blk0 + g, 0)),
        out_shape=jax.ShapeDtypeStruct(o_prev.shape, o_prev.dtype),
        scratch_shapes=[pltpu.VMEM((G, D_MODEL), f32)],
        input_output_aliases={3: 0},
        compiler_params=_cparams(("parallel", "arbitrary")),
        name="xattn_sample",
    )(q_all, ck, cv, o_prev)


def _ret_prompt_body(q_ref, k_ref, v_ref, g_ref, cos_ref, sin_ref, y_ref, s_out, s_s, *, tb, y_off):
    t = pl.program_id(1)
    nc = tb // CHUNK

    @pl.when(t == 0)
    def _():
        s_s[...] = jnp.zeros_like(s_s)

    cosf = cos_ref[...]
    sinf = sin_ref[...]
    r, c, _, tril, _ = _chunk_masks(tb)
    diff = (r - c).astype(f32)
    ridx = (lax.broadcasted_iota(jnp.int32, (tb, 1), 0) % CHUNK).astype(f32)
    heads = []
    for h in range(HEADS):
        hs = slice(h * HEAD_DIM, (h + 1) * HEAD_DIM)
        lg = math.log(1.0 - 2.0 ** (-5.0 - h))
        decay = jnp.where(tril, jnp.exp(jnp.maximum(diff, 0.0) * lg), 0.0)
        qr = _rope(q_ref[:, hs], cosf, sinf)
        kr = _rope(k_ref[:, hs], cosf, sinf) * HEAD_DIM ** -0.5
        vb = v_ref[:, hs].astype(bf16)
        heads.append(dict(
            vb=vb, o_in=_dot(_dot_nt(qr, kr) * decay, vb),
            qd=(qr * jnp.exp((ridx + 1.0) * lg)).astype(bf16),
            kd=(kr * jnp.exp((CHUNK - 1.0 - ridx) * lg)).astype(bf16),
            cdec=math.exp(CHUNK * lg), st=s_s[h], outs=[]))
        yield
    for cc in range(nc):
        sl = slice(cc * CHUNK, (cc + 1) * CHUNK)
        for hd in heads:
            hd["outs"].append(hd["o_in"][sl] + _dot(hd["qd"][sl], hd["st"]))
            hd["st"] = hd["st"] * hd["cdec"] + _dot_tn(hd["kd"][sl], hd["vb"][sl])
        yield
    for h, hd in enumerate(heads):
        hs = slice(h * HEAD_DIM, (h + 1) * HEAD_DIM)
        s_s[h] = hd["st"]
        o = jnp.concatenate(hd["outs"], axis=0)
        y = _rms_rows(o) * _silu(g_ref[:, hs])
        y_ref[:, y_off + h * HEAD_DIM:y_off + (h + 1) * HEAD_DIM] = y.astype(y_ref.dtype)

    @pl.when(t == pl.num_programs(1) - 1)
    def _():
        s_out[0] = s_s[...]


def _mlstm_prompt_body(q_ref, k_ref, v_ref, og_ref, sm_ref, gb_ref, nw_ref,
                       y_ref, c_out, n_out, m_out, c_s, n_s, m_s, *, tb, y_off):
    t = pl.program_id(1)
    nc = tb // CHUNK

    @pl.when(t == 0)
    def _():
        c_s[...] = jnp.zeros_like(c_s)
        n_s[...] = jnp.zeros_like(n_s)
        m_s[...] = jnp.zeros_like(m_s)

    _, _, _, tril, _ = _chunk_masks(tb)
    pre = sm_ref[...] + gb_ref[...]
    fall = _mask_dot_f32(tril, jax.nn.log_sigmoid(pre))
    imf_t = (pre - pltpu.roll(fall, LANES - (L_MLF - L_MLI), 1)).T
    heads = []
    for h in range(HEADS):
        hs = slice(h * HEAD_DIM, (h + 1) * HEAD_DIM)
        fcol = fall[:, L_MLF + h:L_MLF + h + 1]
        icol = pre[:, L_MLI + h:L_MLI + h + 1]
        dm = jnp.where(tril, fcol + imf_t[h:h + 1, :], NEG_INF)
        dmax = jnp.max(dm, axis=-1, keepdims=True)
        mp = m_s[h:h + 1, 0:1]
        m_prev, m_int, m_cur = [], [], []
        for cc in range(nc):
            sl = slice(cc * CHUNK, (cc + 1) * CHUNK)
            mi = mp + fcol[sl]
            mt = jnp.maximum(mi, dmax[sl])
            m_prev.append(mp)
            m_int.append(mi)
            m_cur.append(mt)
            mp = mt[CHUNK - 1:CHUNK, :]
        heads.append(dict(fcol=fcol, icol=icol, dm=dm, m_last=mp, m_prev=m_prev, m_int=m_int, m_cur=m_cur))
    yield
    for h, hd in enumerate(heads):
        hs = slice(h * HEAD_DIM, (h + 1) * HEAD_DIM)
        w = jnp.exp(hd["dm"] - jnp.concatenate(hd["m_cur"], axis=0))
        qh = q_ref[:, hs]
        kh = k_ref[:, hs] * HEAD_DIM ** -0.5
        vb = v_ref[:, hs].astype(bf16)
        s = _dot_nt(qh, kh) * w
        hd.update(qh=qh, kh=kh, vb=vb, num_in=_dot(s, vb), den_in=jnp.sum(s, axis=-1, keepdims=True),
                  cst=c_s[h], nst=n_s[h:h + 1, :], outs=[])
        yield
    for cc in range(nc):
        sl = slice(cc * CHUNK, (cc + 1) * CHUNK)
        for hd in heads:
            mi, mt = hd["m_int"][cc], hd["m_cur"][cc]
            gi = jnp.exp(mi - mt)
            qc = hd["qh"][sl]
            num = hd["num_in"][sl] + gi * _dot(qc, hd["cst"])
            den = hd["den_in"][sl] + gi * jnp.sum(qc * hd["nst"], axis=-1, keepdims=True)
            hd["outs"].append(num / jnp.maximum(jnp.abs(den), jnp.exp(-mt)))
            m_new = mt[CHUNK - 1:CHUNK, :]
            fl = hd["fcol"][sl][CHUNK - 1:CHUNK, :]
            wk = jnp.exp(fl - hd["fcol"][sl] + hd["icol"][sl] - m_new)
            dec = jnp.exp(hd["m_prev"][cc] + fl - m_new)
            kw = hd["kh"][sl] * wk
            hd["cst"] = dec * hd["cst"] + _dot_tn(kw, hd["vb"][sl])
            hd["nst"] = dec * hd["nst"] + jnp.sum(kw, axis=0, keepdims=True)
        yield
    for h, hd in enumerate(heads):
        hs = slice(h * HEAD_DIM, (h + 1) * HEAD_DIM)
        c_s[h] = hd["cst"]
        n_s[h:h + 1, :] = hd["nst"]
        m_s[h:h + 1, :] = jnp.broadcast_to(hd["m_last"], (1, LANES))
        hh = jnp.concatenate(hd["outs"], axis=0)
        y = _rms_rows(hh) * nw_ref[:, hs] * jax.nn.sigmoid(og_ref[:, hs])
        y_ref[:, y_off + h * HEAD_DIM:y_off + (h + 1) * HEAD_DIM] = y.astype(y_ref.dtype)

    @pl.when(t == pl.num_programs(1) - 1)
    def _():
        c_out[0] = c_s[...]
        n_out[0] = n_s[0:HEADS, :]
        m_out[0] = m_s[...]


def _unit_lower_inverse_minus_eye(a_list, r, c):
    def coupling(s):
        return jnp.logical_and((r // (2 * s)) == (c // (2 * s)), (r // s) != (c // s))

    eye = r == c
    first = coupling(1)
    xs = [jnp.where(eye, 1.0, jnp.where(first, -a, 0.0)) for a in a_list]
    s = 2
    while s < CHUNK:
        couple = coupling(s)
        ys = [_dot(jnp.where(couple, a, 0.0), x) for a, x in zip(a_list, xs)]
        xs = [x - _dot(x, y) for x, y in zip(xs, ys)]
        s *= 2
        yield
    return [jnp.where(eye, 0.0, x) for x in xs]


def _gdn_prompt_body(q_ref, k_ref, v_ref, z_ref, sm_ref, gb_ref, al_ref, wc_ref, cs_ref, nw_ref,
                     y_ref, s_out, conv_out, s_s, xb_s, vn_s, *, tb, y_off):
    t = pl.program_id(1)
    nc = tb // CHUNK
    gw = GROUP_WIDTH
    pad = SUBLANES
    hist = CONV_W - 1

    @pl.when(t == 0)
    def _():
        s_s[...] = jnp.zeros_like(s_s)
        for p in range(3):
            xb_s[p, 0:pad, :] = jnp.zeros((pad, gw), f32)
            xb_s[p, pad - hist:pad, :] = cs_ref[0, :, p * gw:(p + 1) * gw]

    @pl.when(t > 0)
    def _():
        for p in range(3):
            xb_s[p, 0:pad, :] = xb_s[p, tb:tb + pad, :]

    xb_s[0, pad:, :] = q_ref[...]
    xb_s[1, pad:, :] = k_ref[...]
    xb_s[2, pad:, :] = v_ref[...]

    conv = []
    for p in range(3):
        acc = xb_s[p, pad - hist:pad - hist + tb, :] * wc_ref[0:1, p * gw:(p + 1) * gw]
        for j in range(1, CONV_W):
            acc = acc + xb_s[p, pad - hist + j:pad - hist + j + tb, :] * wc_ref[j:j + 1, p * gw:(p + 1) * gw]
        conv.append(_silu(acc))

    r, c, _, tril, strict = _chunk_masks(tb)
    pre = sm_ref[...] + gb_ref[...]
    beta_all = jax.nn.sigmoid(pre)
    g_all = -jnp.exp(al_ref[...]) * jax.nn.softplus(pre)
    gall = _mask_dot_f32(tril, g_all)
    gall_t = gall.T
    heads = []
    for h in range(HEADS):
        hs = slice(h * HEAD_DIM, (h + 1) * HEAD_DIM)
        gcol = gall[:, L_GDA + h:L_GDA + h + 1]
        grow = gall_t[L_GDA + h:L_GDA + h + 1, :]
        bcol = beta_all[:, L_GDB + h:L_GDB + h + 1]
        decay = jnp.where(tril, jnp.exp(jnp.where(tril, gcol - grow, 0.0)), 0.0)
        qh = _l2n_rows(conv[0][:, hs]) * HEAD_DIM ** -0.5
        kh = _l2n_rows(conv[1][:, hs])
        kb = kh * bcol
        egc = jnp.exp(gcol)
        heads.append(dict(
            gcol=gcol, kh=kh, qe=qh * egc,
            a=jnp.where(strict, _dot_nt(kb, kh) * decay, 0.0),
            rhs=jnp.concatenate([conv[2][:, hs] * bcol, kb * egc], axis=1),
            attn=(_dot_nt(qh, kh) * decay).astype(bf16)))
        yield
    inv = yield from _unit_lower_inverse_minus_eye([hd["a"] for hd in heads], r, c)
    for h, hd in enumerate(heads):
        x = hd["rhs"] + _dot(inv[h], hd["rhs"])
        hd.update(u=x[:, :HEAD_DIM], w=x[:, HEAD_DIM:], st=s_s[h], outs=[])
        vn_s[h] = jnp.zeros((tb, HEAD_DIM), bf16)
    yield
    for cc in range(nc):
        sl = slice(cc * CHUNK, (cc + 1) * CHUNK)
        for h, hd in enumerate(heads):
            st, gcol = hd["st"], hd["gcol"]
            vnew = hd["u"][sl] - _dot(hd["w"][sl], st)
            vn_s[h, sl, :] = vnew.astype(bf16)
            hd["outs"].append(_dot(hd["qe"][sl], st)
                              + jnp.dot(hd["attn"][sl, :], vn_s[h], preferred_element_type=f32))
            gl = gcol[sl][CHUNK - 1:CHUNK, :]
            kdec = hd["kh"][sl] * jnp.exp(gl - gcol[sl])
            hd["st"] = st * jnp.exp(gl) + _dot_tn(kdec, vnew)
        yield
    for h, hd in enumerate(heads):
        hs = slice(h * HEAD_DIM, (h + 1) * HEAD_DIM)
        s_s[h] = hd["st"]
        o = jnp.concatenate(hd["outs"], axis=0)
        y = _rms_rows(o) * nw_ref[...] * _silu(z_ref[:, hs])
        y_ref[:, y_off + h * HEAD_DIM:y_off + (h + 1) * HEAD_DIM] = y.astype(y_ref.dtype)

    @pl.when(t == pl.num_programs(1) - 1)
    def _():
        s_out[0] = s_s[...]
        for p in range(3):
            conv_out[0, :, p * gw:(p + 1) * gw] = xb_s[p, tb + pad - hist:tb + pad, :]


def _gla_prompt_body(q_ref, k_ref, v_ref, z_ref, sm_ref, w2_ref, b_ref, nw_ref, seg_ref,
                     y_ref, s_out, s_s, *, tb, y_off):
    t = pl.program_id(1)
    nc = tb // CHUNK
    dk = GLA_KEY_DIM
    sub = GLA_SUB
    nsub = CHUNK // sub

    @pl.when(t == 0)
    def _():
        s_s[...] = jnp.zeros_like(s_s)

    r, c, same, tril, _ = _chunk_masks(tb)
    pre = _dot_exact(sm_ref[...], w2_ref[...]) + b_ref[...]
    gk = jax.nn.log_sigmoid(pre) * (1.0 / GLA_GATE_DENOM)
    g = _mask_dot_f32(tril, gk)
    q = q_ref[...] * dk ** -0.5
    k = k_ref[...]
    rpos = lax.broadcasted_iota(jnp.int32, (tb, 1), 0) % CHUNK
    lane = lax.broadcasted_iota(jnp.int32, (1, LANES), 1)
    lane_lo = lane < dk

    a_off = [jnp.zeros((tb, tb), f32) for _ in range(HEADS)]
    g3 = g.reshape(nc, CHUNK, HEADS * dk)
    for i in range(1, nsub):
        ri = jnp.broadcast_to(g3[:, sub * i:sub * i + 1, :], (nc, CHUNK, HEADS * dk)).reshape(tb, HEADS * dk)
        qs = q * jnp.exp(jnp.where(rpos // sub == i, g - ri, NEG_INF))
        ks = k * jnp.exp(jnp.where(rpos < sub * i, ri - g, NEG_INF))
        for h in range(HEADS):
            ps = slice((h // 2) * LANES, (h // 2 + 1) * LANES)
            keep = lane_lo if h % 2 == 0 else jnp.logical_not(lane_lo)
            a_off[h] = a_off[h] + _dot_nt(jnp.where(keep, qs[:, ps], 0.0), ks[:, ps])
        yield

    yield
    band = jnp.zeros((tb, LANES), f32)
    for j in range(sub):
        ksh = k if j == 0 else pltpu.roll(k, j, 0)
        gsh = g if j == 0 else pltpu.roll(g, j, 0)
        valid = (rpos % sub) >= j
        prod = q * ksh * jnp.exp(jnp.where(valid, g - gsh, NEG_INF))
        band = band + jnp.dot(prod.astype(bf16), seg_ref[j], preferred_element_type=f32)
        if j % 4 == 3:
            yield
    if tb > LANES:
        band = jnp.concatenate([band, jnp.zeros((tb, tb - LANES), f32)], axis=1)
    in_sub = jnp.logical_and((r // sub) == (c // sub), r >= c)
    a_diag = []
    for h in range(HEADS):
        shift = (tb - (sub - 1) - sub * h) % tb
        a_diag.append(jnp.where(in_sub, pltpu.roll(band, shift, 1, stride=1, stride_axis=0), 0.0))

    yield
    eg = jnp.exp(g)
    qe = q * eg
    g_t = g.T
    vbs = [v_ref[:, h * HEAD_DIM:(h + 1) * HEAD_DIM].astype(bf16) for h in range(HEADS)]
    o_in = [_dot(a_diag[h] + jnp.where(same, a_off[h], 0.0), vbs[h]) for h in range(HEADS)]
    sts = [s_s[p] for p in range(HEADS // 2)]
    outs = [[] for _ in range(HEADS)]
    for cc in range(nc):
        sl = slice(cc * CHUNK, (cc + 1) * CHUNK)
        last = cc * CHUNK + CHUNK - 1
        for p in range(HEADS // 2):
            ps = slice(p * LANES, (p + 1) * LANES)
            gl_row = g[last:last + 1, ps]
            gl_col = g_t[ps, last:last + 1]
            kdec = k[sl, ps] * jnp.exp(gl_row - g[sl, ps])
            upd = jnp.zeros((LANES, HEAD_DIM), f32)
            for hh in range(2):
                h = 2 * p + hh
                keep = lane_lo if hh == 0 else jnp.logical_not(lane_lo)
                outs[h].append(o_in[h][sl] + _dot(jnp.where(keep, qe[sl, ps], 0.0), sts[p]))
                upd = upd + _dot_tn(jnp.where(keep, kdec, 0.0), vbs[h][sl])
            sts[p] = sts[p] * jnp.exp(gl_col) + upd
        yield
    for p in range(HEADS // 2):
        s_s[p] = sts[p]
    for h in range(HEADS):
        hs = slice(h * HEAD_DIM, (h + 1) * HEAD_DIM)
        o = jnp.concatenate(outs[h], axis=0)
        y = _rms_rows(o) * nw_ref[...] * _silu(z_ref[:, hs])
        y_ref[:, y_off + h * HEAD_DIM:y_off + (h + 1) * HEAD_DIM] = y.astype(y_ref.dtype)

    @pl.when(t == pl.num_programs(1) - 1)
    def _():
        s_out[0] = s_s[...]


def _gla_segment_table():
    d = jnp.arange(HEADS * GLA_KEY_DIM)[None, :, None]
    p = jnp.arange(LANES)[None, None, :]
    j = jnp.arange(GLA_SUB)[:, None, None]
    return (p == (d // GLA_KEY_DIM) * GLA_SUB + (GLA_SUB - 1) - j).astype(bf16)


def _prompt_mixer_kernel(rq, rk, rv, rg, mq, mk, mv, mo, dq, dk, dv, dz, lq, lk, lv, lz, sm_ref,
                         cos_ref, sin_ref, gb_ref, al_ref, wc_ref, cs_ref, mlnw_ref, gdnw_ref, glnw_ref,
                         w2_ref, glb_ref, seg_ref,
                         y_ref, ret_o, mlc_o, mln_o, mlm_o, gdn_o, conv_o, gla_o,
                         ret_s, mlc_s, mln_s, mlm_s, gdn_s, xb_s, vn_s, gla_s, *, tb):
    gw = GROUP_WIDTH
    bodies = [
        _ret_prompt_body(rq, rk, rv, rg, cos_ref, sin_ref, y_ref, ret_o, ret_s, tb=tb, y_off=0),
        _mlstm_prompt_body(mq, mk, mv, mo, sm_ref, gb_ref, mlnw_ref, y_ref, mlc_o, mln_o, mlm_o,
                           mlc_s, mln_s, mlm_s, tb=tb, y_off=gw),
        _gdn_prompt_body(dq, dk, dv, dz, sm_ref, gb_ref, al_ref, wc_ref, cs_ref, gdnw_ref, y_ref, gdn_o, conv_o,
                         gdn_s, xb_s, vn_s, tb=tb, y_off=2 * gw),
        _gla_prompt_body(lq, lk, lv, lz, sm_ref, w2_ref, glb_ref, glnw_ref, seg_ref, y_ref, gla_o, gla_s,
                         tb=tb, y_off=3 * gw)]
    while bodies:
        for body in list(bodies):
            if next(body, StopIteration) is StopIteration:
                bodies.remove(body)


def _prompt_mixers(c_all, lp, *, batch, seq, conv_zero):
    tb = MIX_TB
    nt = seq // tb
    gw = GROUP_WIDTH
    kw = HEADS * GLA_KEY_DIM

    def cspec(name, width):
        blk = C_OFF[name] // width
        assert C_OFF[name] % width == 0
        return pl.BlockSpec((tb, width), lambda b, t: (b * nt + t, blk))

    def const2(shape):
        return pl.BlockSpec(shape, lambda b, t: (0, 0))

    def per_batch(shape):
        return pl.BlockSpec((1,) + shape, lambda b, t: (b,) + (0,) * len(shape))

    names = [("ret_q", gw), ("ret_k", gw), ("ret_v", gw), ("ret_g", gw),
             ("ml_q", gw), ("ml_k", gw), ("ml_v", gw), ("ml_o", gw),
             ("gd_q", gw), ("gd_k", gw), ("gd_v", gw), ("gd_z", gw),
             ("gl_q", kw), ("gl_k", kw), ("gl_v", gw), ("gl_z", gw), ("small", LANES)]
    in_specs = [cspec(n, w) for n, w in names] + [
        pl.BlockSpec((tb, HEAD_DIM), lambda b, t: (t, 0)), pl.BlockSpec((tb, HEAD_DIM), lambda b, t: (t, 0)),
        const2((1, LANES)), const2((1, LANES)), const2((CONV_W, 3 * gw)), per_batch((CONV_W - 1, 3 * gw)),
        const2((1, gw)), const2((1, HEAD_DIM)), const2((1, HEAD_DIM)), const2((LANES, kw)), const2((1, kw)),
        pl.BlockSpec((GLA_SUB, kw, LANES), lambda b, t: (0, 0, 0))]
    assert tb % LANES == 0
    state_shape = (HEADS, HEAD_DIM, HEAD_DIM)
    out_shapes = [(c_all.shape[0], D_MODEL), (batch,) + state_shape, (batch,) + state_shape,
                  (batch, HEADS, HEAD_DIM), (batch, SUBLANES, LANES), (batch,) + state_shape,
                  (batch, CONV_W - 1, 3 * gw), (batch, HEADS // 2, LANES, HEAD_DIM)]
    out_specs = [pl.BlockSpec((tb, D_MODEL), lambda b, t: (b * nt + t, 0))] + [
        per_batch(s[1:]) for s in out_shapes[1:]]
    outs = pl.pallas_call(
        functools.partial(_prompt_mixer_kernel, tb=tb),
        grid=(batch, nt),
        in_specs=in_specs,
        out_specs=out_specs,
        out_shape=[jax.ShapeDtypeStruct(out_shapes[0], bf16)] + [jax.ShapeDtypeStruct(s, f32) for s in out_shapes[1:]],
        scratch_shapes=[pltpu.VMEM(state_shape, f32), pltpu.VMEM(state_shape, f32),
                        pltpu.VMEM((SUBLANES, HEAD_DIM), f32), pltpu.VMEM((SUBLANES, LANES), f32),
                        pltpu.VMEM(state_shape, f32), pltpu.VMEM((3, tb + SUBLANES, gw), f32),
                        pltpu.VMEM((HEADS, tb, HEAD_DIM), bf16), pltpu.VMEM((HEADS // 2, LANES, HEAD_DIM), f32)],
        compiler_params=_cparams(("parallel", "arbitrary")),
        name="prompt_mixers",
    )(*([c_all] * len(names)), lp["cos_p"], lp["sin_p"], lp["gate_bias"], lp["alog_row"], lp["gd_conv"],
      conv_zero, lp["ml_norm"], lp["gd_norm"], lp["gl_norm"], lp["gl_w2p"], lp["gl_b"], _gla_segment_table())
    y, s_ret, s_c, s_n, s_m, s_gdn, s_conv, s_gla = outs
    states = {"ret": s_ret, "ml_C": s_c, "ml_n": s_n, "ml_m": s_m[:, :HEADS, 0],
              "gdn": s_gdn, "gdn_conv": s_conv,
              "gla": s_gla.reshape(batch, HEADS, GLA_KEY_DIM, HEAD_DIM)}
    return y, states


def _rows_pad(x):
    return jnp.concatenate([x, jnp.zeros((LANES - x.shape[0], LANES), f32)], axis=0)


def _cols(x):
    return _rows_pad(x).T


def _row(x, j):
    r = lax.broadcasted_iota(jnp.int32, (x.shape[0], 1), 0)
    return jnp.sum(jnp.where(r == j, x, 0.0), axis=0, keepdims=True)


def _outer(x_cols, j, y_rows):
    lane = lax.broadcasted_iota(jnp.int32, (1, LANES), 1)
    return jnp.dot(jnp.where(lane == j, x_cols, 0.0).astype(bf16), y_rows, preferred_element_type=f32)


def _sample_mixer_kernel(c_ref, cos_ref, sin_ref, gb_ref, al_ref, wc_ref, mlnw_ref, gdnw_ref, glnw_ref,
                         w2_ref, glb_ref,
                         ret_ref, mlc_ref, mln_ref, mlm_ref, gdn_ref, conv_ref, gla_ref, *rest):
    y_ref, ret_o, mlc_o, mln_o, mlm_o, gdn_o, conv_o, gla_o = rest[-8:]
    G = SAMPLE_G
    gw = GROUP_WIDTH
    dk = GLA_KEY_DIM

    def cblk(name, width):
        return c_ref[:, C_OFF[name]:C_OFF[name] + width]

    small = cblk("small", LANES)
    pre = small + gb_ref[...]
    cosf = cos_ref[...]
    sinf = sin_ref[...]

    heads = []
    for h in range(HEADS):
        hs = slice(h * HEAD_DIM, (h + 1) * HEAD_DIM)
        q = _rope(cblk("ret_q", gw)[:, hs], cosf, sinf)
        k = _rope(cblk("ret_k", gw)[:, hs], cosf, sinf) * HEAD_DIM ** -0.5
        heads.append(dict(kt=_cols(k), vpad=_rows_pad(cblk("ret_v", gw)[:, hs]).astype(bf16), qb=q.astype(bf16),
                          gamma=1.0 - 2.0 ** (-5.0 - h), rows=[]))
    for j in range(G):
        for h, hd in enumerate(heads):
            s_new = hd["gamma"] * ret_ref[j, h] + _outer(hd["kt"], j, hd["vpad"])
            ret_o[j, h] = s_new
            hd["rows"].append(_row(jnp.dot(hd["qb"], s_new.astype(bf16), preferred_element_type=f32), j))
    for h, hd in enumerate(heads):
        hs = slice(h * HEAD_DIM, (h + 1) * HEAD_DIM)
        o = jnp.concatenate(hd["rows"], axis=0)
        y = _rms_rows(o) * _silu(cblk("ret_g", gw)[:, hs])
        y_ref[:, 0 * gw + h * HEAD_DIM:0 * gw + (h + 1) * HEAD_DIM] = y.astype(y_ref.dtype)

    m_old = mlm_ref[...]
    f_al = pltpu.roll(jax.nn.log_sigmoid(pre), LANES - (L_MLF - L_MLI), 1)
    m_int = m_old + f_al
    m_t = jnp.maximum(m_int, pre)
    w_in = jnp.exp(pre - m_t)
    g_in = jnp.exp(m_int - m_t)
    e_neg = jnp.exp(-m_t)
    mlm_o[...] = m_t
    heads = []
    for h in range(HEADS):
        hs = slice(h * HEAD_DIM, (h + 1) * HEAD_DIM)
        q = cblk("ml_q", gw)[:, hs]
        kw = cblk("ml_k", gw)[:, hs] * HEAD_DIM ** -0.5 * w_in[:, h:h + 1]
        heads.append(dict(q=q, kw=kw, kt=_cols(kw), vpad=_rows_pad(cblk("ml_v", gw)[:, hs]).astype(bf16),
                          qb=q.astype(bf16), rows=[]))
    for j in range(G):
        for h, hd in enumerate(heads):
            gj = g_in[j:j + 1, h:h + 1]
            c_new = gj * mlc_ref[j, h] + _outer(hd["kt"], j, hd["vpad"])
            mlc_o[j, h] = c_new
            n_new = gj * mln_ref[j, h:h + 1, :] + hd["kw"][j:j + 1, :]
            mln_o[j, h:h + 1, :] = n_new
            num = _row(jnp.dot(hd["qb"], c_new.astype(bf16), preferred_element_type=f32), j)
            den = jnp.sum(hd["q"][j:j + 1, :] * n_new, axis=-1, keepdims=True)
            hd["rows"].append(num / jnp.maximum(jnp.abs(den), e_neg[j:j + 1, h:h + 1]))
    for h, hd in enumerate(heads):
        hs = slice(h * HEAD_DIM, (h + 1) * HEAD_DIM)
        hh = jnp.concatenate(hd["rows"], axis=0)
        y = _rms_rows(hh) * mlnw_ref[:, hs] * jax.nn.sigmoid(cblk("ml_o", gw)[:, hs])
        y_ref[:, 1 * gw + h * HEAD_DIM:1 * gw + (h + 1) * HEAD_DIM] = y.astype(y_ref.dtype)

    beta_all = jax.nn.sigmoid(pre)
    eg_all = jnp.exp(-jnp.exp(al_ref[...]) * jax.nn.softplus(pre))
    conv = []
    for p, name in enumerate(("gd_q", "gd_k", "gd_v")):
        ps = slice(p * gw, (p + 1) * gw)
        x = cblk(name, gw)
        acc = x * wc_ref[CONV_W - 1:CONV_W, ps]
        for j in range(CONV_W - 1):
            acc = acc + conv_ref[j, :, ps] * wc_ref[j:j + 1, ps]
        conv.append(_silu(acc))
        for j in range(CONV_W - 2):
            conv_o[j, :, ps] = conv_ref[j + 1, :, ps]
        conv_o[CONV_W - 2, :, ps] = x
    heads = []
    for h in range(HEADS):
        hs = slice(h * HEAD_DIM, (h + 1) * HEAD_DIM)
        q = _l2n_rows(conv[0][:, hs]) * HEAD_DIM ** -0.5
        k = _l2n_rows(conv[1][:, hs])
        heads.append(dict(kt=_cols(k), kb16=k.astype(bf16), qb=q.astype(bf16), v=conv[2][:, hs],
                          bcol=beta_all[:, L_GDB + h:L_GDB + h + 1], ecol=eg_all[:, L_GDA + h:L_GDA + h + 1],
                          ks=[], rows=[]))
    for j in range(G):
        for h, hd in enumerate(heads):
            hd["ks"].append(_row(jnp.dot(hd["kb16"], gdn_ref[j, h].astype(bf16), preferred_element_type=f32), j))
    for hd in heads:
        ks = jnp.concatenate(hd["ks"], axis=0)
        hd["vpad"] = _rows_pad(hd["bcol"] * (hd["v"] - hd["ecol"] * ks)).astype(bf16)
    for j in range(G):
        for h, hd in enumerate(heads):
            s_new = hd["ecol"][j:j + 1, :] * gdn_ref[j, h] + _outer(hd["kt"], j, hd["vpad"])
            gdn_o[j, h] = s_new
            hd["rows"].append(_row(jnp.dot(hd["qb"], s_new.astype(bf16), preferred_element_type=f32), j))
    for h, hd in enumerate(heads):
        hs = slice(h * HEAD_DIM, (h + 1) * HEAD_DIM)
        o = jnp.concatenate(hd["rows"], axis=0)
        y = _rms_rows(o) * gdnw_ref[...] * _silu(cblk("gd_z", gw)[:, hs])
        y_ref[:, 2 * gw + h * HEAD_DIM:2 * gw + (h + 1) * HEAD_DIM] = y.astype(y_ref.dtype)

    gk = jax.nn.log_sigmoid(_dot_exact(small, w2_ref[...]) + glb_ref[...]) * (1.0 / GLA_GATE_DENOM)
    eg = jnp.exp(gk)
    q_all = cblk("gl_q", HEADS * dk) * dk ** -0.5
    k_all = cblk("gl_k", HEADS * dk)
    upper = lax.broadcasted_iota(jnp.int32, (LANES, 1), 0) < dk
    lane = lax.broadcasted_iota(jnp.int32, (1, LANES), 1)
    pairs = []
    for p in range(HEADS // 2):
        ps = slice(p * LANES, (p + 1) * LANES)
        kp, qp = k_all[:, ps], q_all[:, ps]
        v0 = cblk("gl_v", gw)[:, (2 * p) * HEAD_DIM:(2 * p + 1) * HEAD_DIM]
        v1 = cblk("gl_v", gw)[:, (2 * p + 1) * HEAD_DIM:(2 * p + 2) * HEAD_DIM]
        pairs.append(dict(
            kt2=_cols(jnp.concatenate([kp, kp], axis=0)),
            et=_cols(eg[:, ps]),
            vpad=_rows_pad(jnp.concatenate([v0, v1], axis=0)).astype(bf16),
            q2=jnp.concatenate([jnp.where(lane < dk, qp, 0.0), jnp.where(lane < dk, 0.0, qp)],
                               axis=0).astype(bf16),
            rows=([], [])))
    for j in range(G):
        keep = lane == jnp.where(upper, j, G + j)
        for p, pd in enumerate(pairs):
            u = jnp.dot(jnp.where(keep, pd["kt2"], 0.0).astype(bf16), pd["vpad"], preferred_element_type=f32)
            s_new = pd["et"][:, j:j + 1] * gla_ref[j, p] + u
            gla_o[j, p] = s_new
            res = jnp.dot(pd["q2"], s_new.astype(bf16), preferred_element_type=f32)
            pd["rows"][0].append(_row(res, j))
            pd["rows"][1].append(_row(res, G + j))
    for p, pd in enumerate(pairs):
        for hh in range(2):
            h = 2 * p + hh
            hs = slice(h * HEAD_DIM, (h + 1) * HEAD_DIM)
            o = jnp.concatenate(pd["rows"][hh], axis=0)
            y = _rms_rows(o) * glnw_ref[...] * _silu(cblk("gl_z", gw)[:, hs])
            y_ref[:, 3 * gw + h * HEAD_DIM:3 * gw + (h + 1) * HEAD_DIM] = y.astype(y_ref.dtype)


def _sample_mixers(c_all, lp, st, y_prev, st_prev, *, layer, depth, row0, nb):
    G = SAMPLE_G
    gw = GROUP_WIDTH
    kw = HEADS * GLA_KEY_DIM
    assert row0 % G == 0 and nb % G == 0
    blk0 = row0 // G

    def const2(shape):
        return pl.BlockSpec(shape, lambda i: (0, 0))

    def slab(shape, batch_axis=0):
        def index(i):
            idx = [0] * len(shape)
            idx[batch_axis] = i
            return (layer,) + tuple(idx)
        return pl.BlockSpec((None,) + shape, index)

    state_specs = [slab((G, HEADS, HEAD_DIM, HEAD_DIM)), slab((G, HEADS, HEAD_DIM, HEAD_DIM)),
                   slab((G, HEADS, HEAD_DIM)), slab((G, LANES)), slab((G, HEADS, HEAD_DIM, HEAD_DIM)),
                   slab((CONV_W - 1, G, 3 * gw), batch_axis=1), slab((G, HEADS // 2, LANES, HEAD_DIM))]
    order = ("ret", "ml_C", "ml_n", "ml_m", "gdn", "gdn_conv", "gla")
    states = [st[n] for n in order]
    y_spec = pl.BlockSpec((G, D_MODEL), lambda i: (blk0 + i, 0))

    args = [c_all, lp["cos_s"], lp["sin_s"], lp["gate_bias"], lp["alog_row"], lp["gd_conv"], lp["ml_norm"],
            lp["gd_norm"], lp["gl_norm"], lp["gl_w2p"], lp["gl_b"]] + states
    in_specs = [pl.BlockSpec((G, C_WIDTH), lambda i: (blk0 + i, 0)),
                const2((1, HEAD_DIM)), const2((1, HEAD_DIM)), const2((1, LANES)), const2((1, LANES)),
                const2((CONV_W, 3 * gw)), const2((1, gw)), const2((1, HEAD_DIM)), const2((1, HEAD_DIM)),
                const2((LANES, kw)), const2((1, kw))] + state_specs
    inplace = [y_prev] + ([st_prev[n] for n in order] if st_prev is not None else [])
    aliases = {len(args) + k: k for k in range(len(inplace))}
    args += inplace
    in_specs += [pl.BlockSpec(memory_space=pl.ANY)] * len(inplace)

    outs = pl.pallas_call(
        _sample_mixer_kernel,
        grid=(nb // G,),
        in_specs=in_specs,
        out_specs=[y_spec] + state_specs,
        out_shape=[jax.ShapeDtypeStruct(y_prev.shape, y_prev.dtype)] + [
            jax.ShapeDtypeStruct(s.shape, s.dtype) for s in states],
        input_output_aliases=aliases,
        compiler_params=_cparams(("parallel",)),
        name="sample_mixers",
    )(*args)
    return outs[0], dict(zip(order, outs[1:]))


def _rope_tables(pos):
    half = HEAD_DIM // 2
    inv = ROPE_BASE ** (-jnp.arange(half, dtype=f32) / half)
    ang = pos[:, None] * inv[None, :]
    cos, sin = jnp.cos(ang), jnp.sin(ang)
    return jnp.concatenate([cos, cos], axis=-1), jnp.concatenate([-sin, sin], axis=-1)


def _permute_w_in(w):
    big = jnp.concatenate([w[..., 0:4096], w[..., 4104:6152], w[..., 6160:7696]], axis=-1)
    small = jnp.concatenate([w[..., 4096:4104], w[..., 6152:6160], w[..., 7696:7712]], axis=-1)
    pad = jnp.zeros(w.shape[:-1] + (C_WIDTH - big.shape[-1] - small.shape[-1],), w.dtype)
    return jnp.concatenate([big, small, pad], axis=-1)


W_IN_RAW = 7712
W_IN_TILE = 512
W_IN_SHIFTS = ((8, 8, 12), (16, 12, 15))
W_IN_EDGE = 16
W_IN_GATE_ROWS = (4096, 6144, 7696)


def _w_in_prep_kernel(a_ref, b_ref, c0_ref, c1_ref, c2_ref, o_ref):
    j = pl.program_id(1)
    k = a_ref.shape[1]

    def emit(x):
        o_ref[...] = x.T.astype(bf16)

    @pl.when(j < W_IN_SHIFTS[0][1])
    def _():
        emit(a_ref[...])

    for shift, lo, hi in W_IN_SHIFTS:
        @pl.when(jnp.logical_and(j >= lo, j < hi))
        def _():
            emit(jnp.concatenate([a_ref[shift:, :], b_ref[:shift, :]], axis=0))

    @pl.when(j >= W_IN_SHIFTS[-1][2])
    def _():
        gates = jnp.concatenate([c0_ref[:L_GDB, :], c1_ref[L_GDB:L_GLR, :], c2_ref[...]], axis=0)
        emit(jnp.concatenate([gates, jnp.zeros((W_IN_TILE - gates.shape[0], k), f32)], axis=0))


def _w_in_prep(w_in):
    depth, d, n = w_in.shape
    assert n == W_IN_RAW and C_OFF["small"] == W_IN_SHIFTS[-1][2] * W_IN_TILE
    assert L_GLR + GLA_RANK == 2 * W_IN_EDGE and n % W_IN_EDGE == 0
    w_t = jnp.swapaxes(w_in, 1, 2)
    last_wide = W_IN_SHIFTS[-1][2] - 1
    per_tile = W_IN_TILE // W_IN_EDGE
    last_edge = n // W_IN_EDGE - 1

    def edge(index):
        return pl.BlockSpec((None, W_IN_EDGE, d), index)

    g0, g1, g2 = (r // W_IN_EDGE for r in W_IN_GATE_ROWS)
    return pl.pallas_call(
        _w_in_prep_kernel,
        grid=(depth, C_WIDTH // W_IN_TILE),
        in_specs=[pl.BlockSpec((None, W_IN_TILE, d), lambda l, j: (l, jnp.minimum(j, last_wide), 0)),
                  edge(lambda l, j: (l, jnp.minimum(per_tile * (j + 1), last_edge), 0)),
                  edge(lambda l, j: (l, g0, 0)), edge(lambda l, j: (l, g1, 0)), edge(lambda l, j: (l, g2, 0))],
        out_specs=pl.BlockSpec((None, d, W_IN_TILE), lambda l, j: (l, 0, j)),
        out_shape=jax.ShapeDtypeStruct((depth, d, C_WIDTH), bf16),
        compiler_params=_cparams(("parallel", "arbitrary")),
        name="w_in_prep",
    )(w_t, w_t, w_t, w_t, w_t)


def _lane_row(pieces):
    row = jnp.zeros((1, LANES), f32)
    for off, val in pieces:
        row = lax.dynamic_update_slice(row, val.reshape(1, -1).astype(f32), (0, off))
    return row


def kernel(x_prompt, x_sample, state_ret, state_mlstm_C, state_mlstm_n, state_mlstm_m, state_gdn, state_gdn_conv, state_gla, cache_mem_k, cache_mem_v, mem_prompt, norm_mix_pre, norm_mix_post, w_in, ml_ib, ml_fb, ml_norm, gd_conv, gd_A_log, gd_dt_bias, gd_norm, gl_w2, gl_b, gl_norm, w_out, norm_x_pre, norm_x_post, norm_mem, w_xq, w_xk, w_xv, w_xo, norm_mlp_pre, norm_mlp_post, w_up, w_down):
    bp, tp, d = x_prompt.shape
    bs, ts, _ = x_sample.shape
    depth = w_in.shape[0]
    assert ts == 1 and d == D_MODEL and tp % MIX_TB == 0
    rows_p = bp * tp
    rows = rows_p + bs
    assert rows % ROW_TILE == 0 and rows % ROW_TILE_WIDE == 0 and rows_p % SAMPLE_G == 0

    xh = _embed(x_prompt.reshape(rows_p, d), norm_mix_pre[0], None, row0=0, rows=rows, tm=512)
    x, h = _embed(x_sample.reshape(bs, d), norm_mix_pre[0], xh, row0=rows_p, rows=rows, tm=bs)
    w_in_b = _w_in_prep(w_in)
    w_out_b, w_xo_b, w_down_b = w_out.astype(bf16), w_xo.astype(bf16), w_down.astype(bf16)
    cos_p, sin_p = _rope_tables(jnp.arange(tp, dtype=f32))
    cos_s, sin_s = _rope_tables(jnp.arange(ts, dtype=f32) + PAST_LEN)
    conv_zero = jnp.zeros((bp, CONV_W - 1, 3 * GROUP_WIDTH), f32)
    mem2d = mem_prompt.reshape(bp * MEM_LEN, d)
    st_in = {"ret": state_ret, "ml_C": state_mlstm_C, "ml_n": state_mlstm_n,
             "ml_m": jnp.pad(state_mlstm_m, ((0, 0), (0, 0), (0, LANES - HEADS))),
             "gdn": state_gdn, "gdn_conv": jnp.transpose(state_gdn_conv, (0, 2, 1, 3)),
             "gla": state_gla.reshape(depth, bs, HEADS // 2, LANES, HEAD_DIM)}

    new_p = {n: [] for n in ("ret", "ml_C", "ml_n", "ml_m", "gdn", "gdn_conv", "gla")}
    st_s = None
    mk_p = mv_p = None
    for l in range(depth):
        lp = {
            "cos_p": cos_p, "sin_p": sin_p, "cos_s": cos_s, "sin_s": sin_s,
            "gate_bias": _lane_row([(L_MLI, ml_ib[l]), (L_MLF, ml_fb[l]), (L_GDA, gd_dt_bias[l])]),
            "alog_row": _lane_row([(L_GDA, gd_A_log[l])]),
            "ml_norm": ml_norm[l].reshape(1, GROUP_WIDTH),
            "gd_norm": gd_norm[l].reshape(1, HEAD_DIM),
            "gl_norm": gl_norm[l].reshape(1, HEAD_DIM),
            "gd_conv": gd_conv[l],
            "gl_w2p": jnp.zeros((LANES, HEADS * GLA_KEY_DIM), f32).at[L_GLR:L_GLR + GLA_RANK].set(gl_w2[l]),
            "gl_b": gl_b[l].reshape(1, HEADS * GLA_KEY_DIM),
        }
        mk_p = _norm_matmul_slab(mem2d, norm_mem[l], w_xk, mk_p, layer=l, tm=512, tn=1024, name="mem_k")
        mv_p = _norm_matmul_slab(mem2d, norm_mem[l], w_xv, mv_p, layer=l, tm=512, tn=1024, name="mem_v")

        c_all = _matmul(h, w_in_b, layer=l, tm=ROW_TILE_WIDE, tn=2048, out_dtype=f32, name="w_in")
        y_all, st_p = _prompt_mixers(c_all, lp, batch=bp, seq=tp, conv_zero=conv_zero)
        y_all, st_s = _sample_mixers(c_all, lp, st_in, y_all, st_s, layer=l, depth=depth, row0=rows_p, nb=bs)
        x, h = _matmul_norm_res(y_all, w_out_b, norm_mix_post[l], x, norm_x_pre[l], layer=l,
                                tm=ROW_TILE, tk=D_MODEL, name="w_out")

        q_all = _matmul(h, w_xq, layer=l, tm=ROW_TILE_WIDE, tn=1024, out_dtype=bf16, name="w_xq")
        o_all = _xattn_prompt(q_all, mk_p, mv_p, layer=l, batch=bp, seq=tp, tq=512)
        o_all = _xattn_sample(q_all, cache_mem_k, cache_mem_v, o_all, layer=l, row0=rows_p, nb=bs)
        x, h = _matmul_norm_res(o_all, w_xo_b, norm_x_post[l], x, norm_mlp_pre[l], layer=l,
                                tm=ROW_TILE, tk=D_MODEL, name="w_xo")

        u = _matmul(h, w_up, layer=l, tm=ROW_TILE_WIDE, tn=1024, out_dtype=bf16, act="relu2", name="w_up")
        g_next = norm_mix_pre[l + 1] if l + 1 < depth else None
        x, h = _matmul_norm_res(u, w_down_b, norm_mlp_post[l], x, g_next, layer=l,
                                tm=ROW_TILE, tk=2048, name="w_down")

        for n in new_p:
            new_p[n].append(st_p[n])

    def stk(lst):
        return jnp.stack(lst, axis=0)

    xp = x[:rows_p].reshape(bp, tp, d)
    xs = x[rows_p:].reshape(bs, ts, d)
    return (xp, xs,
            stk(new_p["ret"]), stk(new_p["ml_C"]), stk(new_p["ml_n"]), stk(new_p["ml_m"]),
            stk(new_p["gdn"]), stk(new_p["gdn_conv"]), stk(new_p["gla"]),
            mk_p.reshape(depth, bp, MEM_LEN, d), mv_p.reshape(depth, bp, MEM_LEN, d),
            st_s["ret"], st_s["ml_C"], st_s["ml_n"], st_s["ml_m"][:, :, :HEADS],
            st_s["gdn"], jnp.transpose(st_s["gdn_conv"], (0, 2, 1, 3)),
            st_s["gla"].reshape(depth, bs, HEADS, GLA_KEY_DIM, HEAD_DIM))
```

```python
import functools
import math

import jax
import jax.numpy as jnp
from jax import lax
from jax.experimental import pallas as pl
from jax.experimental.pallas import tpu as pltpu

f32 = jnp.float32
bf16 = jnp.bfloat16

D_MODEL = 2048
HEADS = 4
HEAD_DIM = 128
GROUP_WIDTH = 512
GLA_KEY_DIM = 64
GLA_RANK = 16
GLA_GATE_DENOM = 16.0
CONV_W = 4
CHUNK = 64
ROPE_BASE = 10000.0
X_HEADS = 4
X_HEAD_DIM = 512
MEM_LEN = 256
D_FF = 8192
EPS = 1e-6
PAST_LEN = 16384
NEG_INF = float("-inf")

LANES = 128
SUBLANES = 8

C_OFF = {
    "ret_q": 0, "ret_k": 512, "ret_v": 1024, "ret_g": 1536,
    "ml_q": 2048, "ml_k": 2560, "ml_v": 3072, "ml_o": 3584,
    "gd_q": 4096, "gd_k": 4608, "gd_v": 5120, "gd_z": 5632,
    "gl_q": 6144, "gl_k": 6400, "gl_v": 6656, "gl_z": 7168,
    "small": 7680,
}
C_WIDTH = 8192
L_MLI, L_MLF, L_GDB, L_GDA, L_GLR = 0, 4, 8, 12, 16

ROW_TILE = 640
ROW_TILE_WIDE = 1040
MIX_TB = 256
SAMPLE_G = 8
GLA_SUB = 16
XATTN_S = 2
VMEM_LIMIT = 56 * 1024 * 1024


def _cparams(sem):
    return pltpu.CompilerParams(dimension_semantics=sem, vmem_limit_bytes=VMEM_LIMIT)


def _dot(a, b):
    return jnp.dot(a.astype(bf16), b.astype(bf16), preferred_element_type=f32)


def _dot_nt(a, b):
    return lax.dot_general(a.astype(bf16), b.astype(bf16), (((1,), (1,)), ((), ())),
                           preferred_element_type=f32)


def _dot_tn(a, b):
    return lax.dot_general(a.astype(bf16), b.astype(bf16), (((0,), (0,)), ((), ())),
                           preferred_element_type=f32)


def _dot_exact(a, b):
    return jnp.dot(a, b, preferred_element_type=f32, precision=lax.Precision.HIGHEST)


def _mask_dot_f32(mask, x):
    return _dot_exact(jnp.where(mask, 1.0, 0.0), x)


def _rms_rows(x):
    return x * lax.rsqrt(jnp.mean(x * x, axis=-1, keepdims=True) + EPS)


def _l2n_rows(x):
    return x * lax.rsqrt(jnp.sum(x * x, axis=-1, keepdims=True) + EPS)


def _silu(x):
    return x * jax.nn.sigmoid(x)


def _rope(x, cosf, sinf):
    return x * cosf + pltpu.roll(x, HEAD_DIM // 2, 1) * sinf


def _chunk_masks(tb):
    r = lax.broadcasted_iota(jnp.int32, (tb, tb), 0)
    c = lax.broadcasted_iota(jnp.int32, (tb, tb), 1)
    same = (r // CHUNK) == (c // CHUNK)
    tril = jnp.logical_and(same, r >= c)
    strict = jnp.logical_and(same, r > c)
    return r, c, same, tril, strict


def _rms_gain(x, g):
    return x * lax.rsqrt(jnp.mean(x * x, axis=-1, keepdims=True) + EPS) * g


def _embed_kernel(x_in_ref, g_ref, *rest):
    x_ref, h_ref = rest[-2:]
    x = x_in_ref[...]
    x_ref[...] = x
    h_ref[...] = _rms_gain(x, g_ref[...]).astype(bf16)


def _embed(x2d, g, prev, *, row0, rows, tm):
    m, d = x2d.shape
    assert m % tm == 0 and row0 % tm == 0
    blk0 = row0 // tm
    args = [x2d, g.reshape(1, d)] + (list(prev) if prev is not None else [])
    out_spec = pl.BlockSpec((tm, d), lambda i: (blk0 + i, 0))
    return pl.pallas_call(
        _embed_kernel,
        grid=(m // tm,),
        in_specs=[pl.BlockSpec((tm, d), lambda i: (i, 0)), pl.BlockSpec((1, d), lambda i: (0, 0))] + (
            [pl.BlockSpec(memory_space=pl.ANY)] * 2 if prev is not None else []),
        out_specs=[out_spec, out_spec],
        out_shape=[jax.ShapeDtypeStruct((rows, d), f32), jax.ShapeDtypeStruct((rows, d), bf16)],
        input_output_aliases={2: 0, 3: 1} if prev is not None else {},
        compiler_params=_cparams(("parallel",)),
        name="embed",
    )(*args)


def _matmul_kernel(h_ref, w_ref, *rest, act, cast_w):
    if cast_w:
        o_ref, wb_ref = rest

        @pl.when(pl.program_id(1) == 0)
        def _():
            wb_ref[...] = w_ref[...].astype(bf16)

        w = wb_ref[...]
    else:
        (o_ref,) = rest
        w = w_ref[...]
    y = jnp.dot(h_ref[...], w, preferred_element_type=f32)
    if act == "relu2":
        y = jnp.square(jnp.maximum(y, 0.0))
    o_ref[...] = y.astype(o_ref.dtype)


def _matmul(h, w, *, layer, tm, tn, out_dtype, name, act=None):
    m, k = h.shape
    n = w.shape[2]
    assert m % tm == 0 and n % tn == 0
    cast_w = w.dtype != bf16
    return pl.pallas_call(
        functools.partial(_matmul_kernel, act=act, cast_w=cast_w),
        grid=(n // tn, m // tm),
        in_specs=[pl.BlockSpec((tm, k), lambda j, i: (i, 0)),
                  pl.BlockSpec((None, k, tn), lambda j, i: (layer, 0, j))],
        out_specs=pl.BlockSpec((tm, tn), lambda j, i: (i, j)),
        out_shape=jax.ShapeDtypeStruct((m, n), out_dtype),
        scratch_shapes=[pltpu.VMEM((k, tn), bf16)] if cast_w else [],
        compiler_params=_cparams(("parallel", "arbitrary")),
        name=name,
    )(h, w)


def _norm_matmul_kernel(x_ref, g_ref, w_ref, *rest):
    o_ref, h_ref = rest[-2:]

    @pl.when(pl.program_id(1) == 0)
    def _():
        h_ref[...] = _rms_gain(x_ref[...], g_ref[...]).astype(bf16)

    o_ref[...] = jnp.dot(h_ref[...], w_ref[...].astype(bf16), preferred_element_type=f32)


def _norm_matmul_slab(x, g, w, prev, *, layer, tm, tn, name):
    m, k = x.shape
    depth, _, n = w.shape
    assert m % tm == 0 and n % tn == 0
    in_specs = [pl.BlockSpec((tm, k), lambda i, j: (i, 0)),
                pl.BlockSpec((1, k), lambda i, j: (0, 0)),
                pl.BlockSpec((None, k, tn), lambda i, j: (layer, 0, j))]
    args = [x, g.reshape(1, k), w]
    aliases = {}
    if prev is not None:
        in_specs.append(pl.BlockSpec(memory_space=pl.ANY))
        args.append(prev)
        aliases = {len(args) - 1: 0}
    return pl.pallas_call(
        _norm_matmul_kernel,
        grid=(m // tm, n // tn),
        in_specs=in_specs,
        out_specs=pl.BlockSpec((None, tm, tn), lambda i, j: (layer, i, j)),
        out_shape=jax.ShapeDtypeStruct((depth, m, n), f32),
        scratch_shapes=[pltpu.VMEM((tm, k), bf16)],
        input_output_aliases=aliases,
        compiler_params=_cparams(("parallel", "arbitrary")),
        name=name,
    )(*args)


def _matmul_norm_res_kernel(a_ref, w_ref, g_ref, r_ref, *rest, next_norm, single_k):
    rest = list(rest)
    gn_ref = rest.pop(0) if next_norm else None
    o_ref = rest.pop(0)
    hn_ref = rest.pop(0) if next_norm else None

    tm = a_ref.shape[0]
    halves = (slice(0, tm // 2), slice(tm // 2, tm))

    def finish(y, rows):
        x_new = r_ref[rows, :] + _rms_gain(y, g_ref[...])
        o_ref[rows, :] = x_new
        if next_norm:
            hn_ref[rows, :] = _rms_gain(x_new, gn_ref[...]).astype(bf16)

    if single_k:
        for rows in halves:
            finish(jnp.dot(a_ref[rows, :], w_ref[...], preferred_element_type=f32), rows)
        return

    (acc_ref,) = rest
    kk = pl.program_id(1)
    last = pl.num_programs(1) - 1

    @pl.when(kk == 0)
    def _():
        acc_ref[...] = jnp.zeros_like(acc_ref)

    @pl.when(kk < last)
    def _():
        acc_ref[...] += jnp.dot(a_ref[...], w_ref[...], preferred_element_type=f32)

    @pl.when(kk == last)
    def _():
        for rows in halves:
            finish(acc_ref[rows, :] + jnp.dot(a_ref[rows, :], w_ref[...], preferred_element_type=f32), rows)


def _matmul_norm_res(a, w, g, res, g_next, *, layer, tm, tk, name):
    m, k = a.shape
    n = w.shape[2]
    assert m % tm == 0 and k % tk == 0
    next_norm = g_next is not None
    row = pl.BlockSpec((tm, n), lambda i, j: (i, 0))
    vec = pl.BlockSpec((1, n), lambda i, j: (0, 0))
    in_specs = [pl.BlockSpec((tm, tk), lambda i, j: (i, j)),
                pl.BlockSpec((None, tk, n), lambda i, j: (layer, j, 0)), vec, row]
    args = [a, w, g.reshape(1, n), res]
    out_specs = [row]
    out_shape = [jax.ShapeDtypeStruct((m, n), f32)]
    if next_norm:
        in_specs.append(vec)
        args.append(g_next.reshape(1, n))
        out_specs.append(row)
        out_shape.append(jax.ShapeDtypeStruct((m, n), bf16))
    single_k = k == tk
    outs = pl.pallas_call(
        functools.partial(_matmul_norm_res_kernel, next_norm=next_norm, single_k=single_k),
        grid=(m // tm, k // tk),
        in_specs=in_specs,
        out_specs=out_specs,
        out_shape=out_shape,
        scratch_shapes=[] if single_k else [pltpu.VMEM((tm, n), f32)],
        compiler_params=_cparams(("parallel", "arbitrary")),
        name=name,
    )(*args)
    return (outs[0], outs[1]) if next_norm else (outs[0], None)


def _xattn_prompt_kernel(q_ref, k_ref, v_ref, o_ref, kb_ref, vb_ref):
    @pl.when(pl.program_id(1) == 0)
    def _():
        kb_ref[...] = k_ref[...].astype(bf16)
        vb_ref[...] = v_ref[...].astype(bf16)

    scale = X_HEAD_DIM ** -0.5
    for h in range(X_HEADS):
        sl = slice(h * X_HEAD_DIM, (h + 1) * X_HEAD_DIM)
        s = lax.dot_general(q_ref[:, sl], kb_ref[:, sl], (((1,), (1,)), ((), ())),
                            preferred_element_type=f32) * scale
        m = jnp.max(s, axis=-1, keepdims=True)
        p = jnp.exp(s - m)
        l = jnp.sum(p, axis=-1, keepdims=True)
        o = jnp.dot(p.astype(bf16), vb_ref[:, sl], preferred_element_type=f32) / l
        o_ref[:, sl] = o.astype(o_ref.dtype)


def _xattn_prompt(q_all, mk, mv, *, layer, batch, seq, tq):
    nt = seq // tq
    return pl.pallas_call(
        _xattn_prompt_kernel,
        grid=(batch, nt),
        in_specs=[pl.BlockSpec((tq, D_MODEL), lambda b, t: (b * nt + t, 0)),
                  pl.BlockSpec((None, MEM_LEN, D_MODEL), lambda b, t: (layer, b, 0)),
                  pl.BlockSpec((None, MEM_LEN, D_MODEL), lambda b, t: (layer, b, 0))],
        out_specs=pl.BlockSpec((tq, D_MODEL), lambda b, t: (b * nt + t, 0)),
        out_shape=jax.ShapeDtypeStruct(q_all.shape, bf16),
        scratch_shapes=[pltpu.VMEM((MEM_LEN, D_MODEL), bf16), pltpu.VMEM((MEM_LEN, D_MODEL), bf16)],
        compiler_params=_cparams(("parallel", "arbitrary")),
        name="xattn_prompt",
    )(q_all, mk, mv)


def _xattn_sample_kernel(q_ref, k_ref, v_ref, prev_ref, o_ref, acc_ref):
    del prev_ref
    j = pl.program_id(1)
    scale = X_HEAD_DIM ** -0.5
    rows = lax.broadcasted_iota(jnp.int32, (SAMPLE_G, 1), 0)
    q_all = q_ref[...].astype(f32)
    for s_i in range(XATTN_S):
        row = j * XATTN_S + s_i
        q = jnp.sum(jnp.where(rows == row, q_all, 0.0), axis=0, keepdims=True)
        for h in range(X_HEADS):
            sl = slice(h * X_HEAD_DIM, (h + 1) * X_HEAD_DIM)
            s = jnp.sum(k_ref[s_i, :, sl] * q[:, sl], axis=-1, keepdims=True) * scale
            m = jnp.max(s, axis=0, keepdims=True)
            p = jnp.exp(s - m)
            l = jnp.sum(p, axis=0, keepdims=True)
            acc_ref[pl.ds(row, 1), sl] = jnp.sum(p * v_ref[s_i, :, sl], axis=0, keepdims=True) / l

    @pl.when(j == pl.num_programs(1) - 1)
    def _():
        o_ref[...] = acc_ref[...].astype(o_ref.dtype)


def _xattn_sample(q_all, ck, cv, o_prev, *, layer, row0, nb):
    G = SAMPLE_G
    blk0 = row0 // G
    per_g = G // XATTN_S
    kv_spec = pl.BlockSpec((None, XATTN_S, MEM_LEN, D_MODEL), lambda g, j: (layer, g * per_g + j, 0, 0))
    return pl.pallas_call(
        _xattn_sample_kernel,
        grid=(nb // G, per_g),
        in_specs=[pl.BlockSpec((G, D_MODEL), lambda g, j: (blk0 + g, 0)), kv_spec, kv_spec,
                  pl.BlockSpec(memory_space=pl.ANY)],
        out_specs=pl.BlockSpec((G, D_MODEL), lambda g, j: (blk0 + g, 0)),
        out_shape=jax.ShapeDtypeStruct(o_prev.shape, o_prev.dtype),
        scratch_shapes=[pltpu.VMEM((G, D_MODEL), f32)],
        input_output_aliases={3: 0},
        compiler_params=_cparams(("parallel", "arbitrary")),
        name="xattn_sample",
    )(q_all, ck, cv, o_prev)


def _ret_prompt_body(q_ref, k_ref, v_ref, g_ref, cos_ref, sin_ref, y_ref, s_out, s_s, *, tb, y_off):
    t = pl.program_id(1)
    nc = tb // CHUNK

    @pl.when(t == 0)
    def _():
        s_s[...] = jnp.zeros_like(s_s)

    cosf = cos_ref[...]
    sinf = sin_ref[...]
    r, c, _, tril, _ = _chunk_masks(tb)
    diff = (r - c).astype(f32)
    ridx = (lax.broadcasted_iota(jnp.int32, (tb, 1), 0) % CHUNK).astype(f32)
    heads = []
    for h in range(HEADS):
        hs = slice(h * HEAD_DIM, (h + 1) * HEAD_DIM)
        lg = math.log(1.0 - 2.0 ** (-5.0 - h))
        decay = jnp.where(tril, jnp.exp(jnp.maximum(diff, 0.0) * lg), 0.0)
        qr = _rope(q_ref[:, hs], cosf, sinf)
        kr = _rope(k_ref[:, hs], cosf, sinf) * HEAD_DIM ** -0.5
        vb = v_ref[:, hs].astype(bf16)
        heads.append(dict(
            vb=vb, o_in=_dot(_dot_nt(qr, kr) * decay, vb),
            qd=(qr * jnp.exp((ridx + 1.0) * lg)).astype(bf16),
            kd=(kr * jnp.exp((CHUNK - 1.0 - ridx) * lg)).astype(bf16),
            cdec=math.exp(CHUNK * lg), st=s_s[h], outs=[]))
    yield
    for cc in range(nc):
        sl = slice(cc * CHUNK, (cc + 1) * CHUNK)
        for hd in heads:
            hd["outs"].append(hd["o_in"][sl] + _dot(hd["qd"][sl], hd["st"]))
            hd["st"] = hd["st"] * hd["cdec"] + _dot_tn(hd["kd"][sl], hd["vb"][sl])
        yield
    for h, hd in enumerate(heads):
        hs = slice(h * HEAD_DIM, (h + 1) * HEAD_DIM)
        s_s[h] = hd["st"]
        o = jnp.concatenate(hd["outs"], axis=0)
        y = _rms_rows(o) * _silu(g_ref[:, hs])
        y_ref[:, y_off + h * HEAD_DIM:y_off + (h + 1) * HEAD_DIM] = y.astype(y_ref.dtype)

    @pl.when(t == pl.num_programs(1) - 1)
    def _():
        s_out[0] = s_s[...]


def _mlstm_prompt_body(q_ref, k_ref, v_ref, og_ref, sm_ref, gb_ref, nw_ref,
                       y_ref, c_out, n_out, m_out, c_s, n_s, m_s, *, tb, y_off):
    t = pl.program_id(1)
    nc = tb // CHUNK

    @pl.when(t == 0)
    def _():
        c_s[...] = jnp.zeros_like(c_s)
        n_s[...] = jnp.zeros_like(n_s)
        m_s[...] = jnp.zeros_like(m_s)

    _, _, _, tril, _ = _chunk_masks(tb)
    pre = sm_ref[...] + gb_ref[...]
    fall = _mask_dot_f32(tril, jax.nn.log_sigmoid(pre))
    imf_t = (pre - pltpu.roll(fall, LANES - (L_MLF - L_MLI), 1)).T
    heads = []
    for h in range(HEADS):
        hs = slice(h * HEAD_DIM, (h + 1) * HEAD_DIM)
        fcol = fall[:, L_MLF + h:L_MLF + h + 1]
        icol = pre[:, L_MLI + h:L_MLI + h + 1]
        dm = jnp.where(tril, fcol + imf_t[h:h + 1, :], NEG_INF)
        dmax = jnp.max(dm, axis=-1, keepdims=True)
        mp = m_s[h:h + 1, 0:1]
        m_prev, m_int, m_cur = [], [], []
        for cc in range(nc):
            sl = slice(cc * CHUNK, (cc + 1) * CHUNK)
            mi = mp + fcol[sl]
            mt = jnp.maximum(mi, dmax[sl])
            m_prev.append(mp)
            m_int.append(mi)
            m_cur.append(mt)
            mp = mt[CHUNK - 1:CHUNK, :]
        heads.append(dict(fcol=fcol, icol=icol, dm=dm, m_last=mp, m_prev=m_prev, m_int=m_int, m_cur=m_cur))
    yield
    for h, hd in enumerate(heads):
        hs = slice(h * HEAD_DIM, (h + 1) * HEAD_DIM)
        w = jnp.exp(hd["dm"] - jnp.concatenate(hd["m_cur"], axis=0))
        qh = q_ref[:, hs]
        kh = k_ref[:, hs] * HEAD_DIM ** -0.5
        vb = v_ref[:, hs].astype(bf16)
        s = _dot_nt(qh, kh) * w
        hd.update(qh=qh, kh=kh, vb=vb, num_in=_dot(s, vb), den_in=jnp.sum(s, axis=-1, keepdims=True),
                  cst=c_s[h], nst=n_s[h:h + 1, :], outs=[])
    yield
    for cc in range(nc):
        sl = slice(cc * CHUNK, (cc + 1) * CHUNK)
        for hd in heads:
            mi, mt = hd["m_int"][cc], hd["m_cur"][cc]
            gi = jnp.exp(mi - mt)
            qc = hd["qh"][sl]
            num = hd["num_in"][sl] + gi * _dot(qc, hd["cst"])
            den = hd["den_in"][sl] + gi * jnp.sum(qc * hd["nst"], axis=-1, keepdims=True)
            hd["outs"].append(num / jnp.maximum(jnp.abs(den), jnp.exp(-mt)))
            m_new = mt[CHUNK - 1:CHUNK, :]
            fl = hd["fcol"][sl][CHUNK - 1:CHUNK, :]
            wk = jnp.exp(fl - hd["fcol"][sl] + hd["icol"][sl] - m_new)
            dec = jnp.exp(hd["m_prev"][cc] + fl - m_new)
            kw = hd["kh"][sl] * wk
            hd["cst"] = dec * hd["cst"] + _dot_tn(kw, hd["vb"][sl])
            hd["nst"] = dec * hd["nst"] + jnp.sum(kw, axis=0, keepdims=True)
        yield
    for h, hd in enumerate(heads):
        hs = slice(h * HEAD_DIM, (h + 1) * HEAD_DIM)
        c_s[h] = hd["cst"]
        n_s[h:h + 1, :] = hd["nst"]
        m_s[h:h + 1, :] = jnp.broadcast_to(hd["m_last"], (1, LANES))
        hh = jnp.concatenate(hd["outs"], axis=0)
        y = _rms_rows(hh) * nw_ref[:, hs] * jax.nn.sigmoid(og_ref[:, hs])
        y_ref[:, y_off + h * HEAD_DIM:y_off + (h + 1) * HEAD_DIM] = y.astype(y_ref.dtype)

    @pl.when(t == pl.num_programs(1) - 1)
    def _():
        c_out[0] = c_s[...]
        n_out[0] = n_s[0:HEADS, :]
        m_out[0] = m_s[...]


def _unit_lower_inverse_minus_eye(a_list, r, c):
    def coupling(s):
        return jnp.logical_and((r // (2 * s)) == (c // (2 * s)), (r // s) != (c // s))

    eye = r == c
    first = coupling(1)
    xs = [jnp.where(eye, 1.0, jnp.where(first, -a, 0.0)) for a in a_list]
    s = 2
    while s < CHUNK:
        couple = coupling(s)
        ys = [_dot(jnp.where(couple, a, 0.0), x) for a, x in zip(a_list, xs)]
        xs = [x - _dot(x, y) for x, y in zip(xs, ys)]
        s *= 2
        yield
    return [jnp.where(eye, 0.0, x) for x in xs]


def _gdn_prompt_body(q_ref, k_ref, v_ref, z_ref, sm_ref, gb_ref, al_ref, wc_ref, cs_ref, nw_ref,
                     y_ref, s_out, conv_out, s_s, xb_s, vn_s, *, tb, y_off):
    t = pl.program_id(1)
    nc = tb // CHUNK
    gw = GROUP_WIDTH
    pad = SUBLANES
    hist = CONV_W - 1

    @pl.when(t == 0)
    def _():
        s_s[...] = jnp.zeros_like(s_s)
        for p in range(3):
            xb_s[p, 0:pad, :] = jnp.zeros((pad, gw), f32)
            xb_s[p, pad - hist:pad, :] = cs_ref[0, :, p * gw:(p + 1) * gw]

    @pl.when(t > 0)
    def _():
        for p in range(3):
            xb_s[p, 0:pad, :] = xb_s[p, tb:tb + pad, :]

    xb_s[0, pad:, :] = q_ref[...]
    xb_s[1, pad:, :] = k_ref[...]
    xb_s[2, pad:, :] = v_ref[...]

    conv = []
    for p in range(3):
        acc = xb_s[p, pad - hist:pad - hist + tb, :] * wc_ref[0:1, p * gw:(p + 1) * gw]
        for j in range(1, CONV_W):
            acc = acc + xb_s[p, pad - hist + j:pad - hist + j + tb, :] * wc_ref[j:j + 1, p * gw:(p + 1) * gw]
        conv.append(_silu(acc))

    r, c, _, tril, strict = _chunk_masks(tb)
    pre = sm_ref[...] + gb_ref[...]
    beta_all = jax.nn.sigmoid(pre)
    g_all = -jnp.exp(al_ref[...]) * jax.nn.softplus(pre)
    gall = _mask_dot_f32(tril, g_all)
    gall_t = gall.T
    heads = []
    for h in range(HEADS):
        hs = slice(h * HEAD_DIM, (h + 1) * HEAD_DIM)
        gcol = gall[:, L_GDA + h:L_GDA + h + 1]
        grow = gall_t[L_GDA + h:L_GDA + h + 1, :]
        bcol = beta_all[:, L_GDB + h:L_GDB + h + 1]
        decay = jnp.where(tril, jnp.exp(jnp.where(tril, gcol - grow, 0.0)), 0.0)
        qh = _l2n_rows(conv[0][:, hs]) * HEAD_DIM ** -0.5
        kh = _l2n_rows(conv[1][:, hs])
        kb = kh * bcol
        egc = jnp.exp(gcol)
        heads.append(dict(
            gcol=gcol, kh=kh, qe=qh * egc,
            a=jnp.where(strict, _dot_nt(kb, kh) * decay, 0.0),
            rhs=jnp.concatenate([conv[2][:, hs] * bcol, kb * egc], axis=1),
            attn=(_dot_nt(qh, kh) * decay).astype(bf16)))
    yield
    inv = yield from _unit_lower_inverse_minus_eye([hd["a"] for hd in heads], r, c)
    for h, hd in enumerate(heads):
        x = hd["rhs"] + _dot(inv[h], hd["rhs"])
        hd.update(u=x[:, :HEAD_DIM], w=x[:, HEAD_DIM:], st=s_s[h], outs=[])
        vn_s[h] = jnp.zeros((tb, HEAD_DIM), bf16)
    yield
    for cc in range(nc):
        sl = slice(cc * CHUNK, (cc + 1) * CHUNK)
        for h, hd in enumerate(heads):
            st, gcol = hd["st"], hd["gcol"]
            vnew = hd["u"][sl] - _dot(hd["w"][sl], st)
            vn_s[h, sl, :] = vnew.astype(bf16)
            hd["outs"].append(_dot(hd["qe"][sl], st)
                              + jnp.dot(hd["attn"][sl, :], vn_s[h], preferred_element_type=f32))
            gl = gcol[sl][CHUNK - 1:CHUNK, :]
            kdec = hd["kh"][sl] * jnp.exp(gl - gcol[sl])
            hd["st"] = st * jnp.exp(gl) + _dot_tn(kdec, vnew)
        yield
    for h, hd in enumerate(heads):
        hs = slice(h * HEAD_DIM, (h + 1) * HEAD_DIM)
        s_s[h] = hd["st"]
        o = jnp.concatenate(hd["outs"], axis=0)
        y = _rms_rows(o) * nw_ref[...] * _silu(z_ref[:, hs])
        y_ref[:, y_off + h * HEAD_DIM:y_off + (h + 1) * HEAD_DIM] = y.astype(y_ref.dtype)

    @pl.when(t == pl.num_programs(1) - 1)
    def _():
        s_out[0] = s_s[...]
        for p in range(3):
            conv_out[0, :, p * gw:(p + 1) * gw] = xb_s[p, tb + pad - hist:tb + pad, :]


def _gla_prompt_body(q_ref, k_ref, v_ref, z_ref, sm_ref, w2_ref, b_ref, nw_ref, seg_ref,
                     y_ref, s_out, s_s, *, tb, y_off):
    t = pl.program_id(1)
    nc = tb // CHUNK
    dk = GLA_KEY_DIM
    sub = GLA_SUB
    nsub = CHUNK // sub

    @pl.when(t == 0)
    def _():
        s_s[...] = jnp.zeros_like(s_s)

    r, c, same, tril, _ = _chunk_masks(tb)
    pre = _dot_exact(sm_ref[...], w2_ref[...]) + b_ref[...]
    gk = jax.nn.log_sigmoid(pre) * (1.0 / GLA_GATE_DENOM)
    g = _mask_dot_f32(tril, gk)
    q = q_ref[...] * dk ** -0.5
    k = k_ref[...]
    rpos = lax.broadcasted_iota(jnp.int32, (tb, 1), 0) % CHUNK
    lane = lax.broadcasted_iota(jnp.int32, (1, LANES), 1)
    lane_lo = lane < dk

    a_off = [jnp.zeros((tb, tb), f32) for _ in range(HEADS)]
    g3 = g.reshape(nc, CHUNK, HEADS * dk)
    for i in range(1, nsub):
        ri = jnp.broadcast_to(g3[:, sub * i:sub * i + 1, :], (nc, CHUNK, HEADS * dk)).reshape(tb, HEADS * dk)
        qs = q * jnp.exp(jnp.where(rpos // sub == i, g - ri, NEG_INF))
        ks = k * jnp.exp(jnp.where(rpos < sub * i, ri - g, NEG_INF))
        for h in range(HEADS):
            ps = slice((h // 2) * LANES, (h // 2 + 1) * LANES)
            keep = lane_lo if h % 2 == 0 else jnp.logical_not(lane_lo)
            a_off[h] = a_off[h] + _dot_nt(jnp.where(keep, qs[:, ps], 0.0), ks[:, ps])

    yield
    band = jnp.zeros((tb, LANES), f32)
    for j in range(sub):
        ksh = k if j == 0 else pltpu.roll(k, j, 0)
        gsh = g if j == 0 else pltpu.roll(g, j, 0)
        valid = (rpos % sub) >= j
        prod = q * ksh * jnp.exp(jnp.where(valid, g - gsh, NEG_INF))
        band = band + jnp.dot(prod.astype(bf16), seg_ref[j], preferred_element_type=f32)
    if tb > LANES:
        band = jnp.concatenate([band, jnp.zeros((tb, tb - LANES), f32)], axis=1)
    in_sub = jnp.logical_and((r // sub) == (c // sub), r >= c)
    a_diag = []
    for h in range(HEADS):
        shift = (tb - (sub - 1) - sub * h) % tb
        a_diag.append(jnp.where(in_sub, pltpu.roll(band, shift, 1, stride=1, stride_axis=0), 0.0))

    yield
    eg = jnp.exp(g)
    qe = q * eg
    g_t = g.T
    vbs = [v_ref[:, h * HEAD_DIM:(h + 1) * HEAD_DIM].astype(bf16) for h in range(HEADS)]
    o_in = [_dot(a_diag[h] + jnp.where(same, a_off[h], 0.0), vbs[h]) for h in range(HEADS)]
    sts = [s_s[p] for p in range(HEADS // 2)]
    outs = [[] for _ in range(HEADS)]
    for cc in range(nc):
        sl = slice(cc * CHUNK, (cc + 1) * CHUNK)
        last = cc * CHUNK + CHUNK - 1
        for p in range(HEADS // 2):
            ps = slice(p * LANES, (p + 1) * LANES)
            gl_row = g[last:last + 1, ps]
            gl_col = g_t[ps, last:last + 1]
            kdec = k[sl, ps] * jnp.exp(gl_row - g[sl, ps])
            upd = jnp.zeros((LANES, HEAD_DIM), f32)
            for hh in range(2):
                h = 2 * p + hh
                keep = lane_lo if hh == 0 else jnp.logical_not(lane_lo)
                outs[h].append(o_in[h][sl] + _dot(jnp.where(keep, qe[sl, ps], 0.0), sts[p]))
                upd = upd + _dot_tn(jnp.where(keep, kdec, 0.0), vbs[h][sl])
            sts[p] = sts[p] * jnp.exp(gl_col) + upd
        yield
    for p in range(HEADS // 2):
        s_s[p] = sts[p]
    for h in range(HEADS):
        hs = slice(h * HEAD_DIM, (h + 1) * HEAD_DIM)
        o = jnp.concatenate(outs[h], axis=0)
        y = _rms_rows(o) * nw_ref[...] * _silu(z_ref[:, hs])
        y_ref[:, y_off + h * HEAD_DIM:y_off + (h + 1) * HEAD_DIM] = y.astype(y_ref.dtype)

    @pl.when(t == pl.num_programs(1) - 1)
    def _():
        s_out[0] = s_s[...]


def _gla_segment_table():
    d = jnp.arange(HEADS * GLA_KEY_DIM)[None, :, None]
    p = jnp.arange(LANES)[None, None, :]
    j = jnp.arange(GLA_SUB)[:, None, None]
    return (p == (d // GLA_KEY_DIM) * GLA_SUB + (GLA_SUB - 1) - j).astype(bf16)


def _prompt_mixer_kernel(rq, rk, rv, rg, mq, mk, mv, mo, dq, dk, dv, dz, lq, lk, lv, lz, sm_ref,
                         cos_ref, sin_ref, gb_ref, al_ref, wc_ref, cs_ref, mlnw_ref, gdnw_ref, glnw_ref,
                         w2_ref, glb_ref, seg_ref,
                         y_ref, ret_o, mlc_o, mln_o, mlm_o, gdn_o, conv_o, gla_o,
                         ret_s, mlc_s, mln_s, mlm_s, gdn_s, xb_s, vn_s, gla_s, *, tb):
    gw = GROUP_WIDTH
    bodies = [
        _ret_prompt_body(rq, rk, rv, rg, cos_ref, sin_ref, y_ref, ret_o, ret_s, tb=tb, y_off=0),
        _mlstm_prompt_body(mq, mk, mv, mo, sm_ref, gb_ref, mlnw_ref, y_ref, mlc_o, mln_o, mlm_o,
                           mlc_s, mln_s, mlm_s, tb=tb, y_off=gw),
        _gdn_prompt_body(dq, dk, dv, dz, sm_ref, gb_ref, al_ref, wc_ref, cs_ref, gdnw_ref, y_ref, gdn_o, conv_o,
                         gdn_s, xb_s, vn_s, tb=tb, y_off=2 * gw),
        _gla_prompt_body(lq, lk, lv, lz, sm_ref, w2_ref, glb_ref, glnw_ref, seg_ref, y_ref, gla_o, gla_s,
                         tb=tb, y_off=3 * gw)]
    while bodies:
        for body in list(bodies):
            if next(body, StopIteration) is StopIteration:
                bodies.remove(body)


def _prompt_mixers(c_all, lp, *, batch, seq, conv_zero):
    tb = MIX_TB
    nt = seq // tb
    gw = GROUP_WIDTH
    kw = HEADS * GLA_KEY_DIM

    def cspec(name, width):
        blk = C_OFF[name] // width
        assert C_OFF[name] % width == 0
        return pl.BlockSpec((tb, width), lambda b, t: (b * nt + t, blk))

    def const2(shape):
        return pl.BlockSpec(shape, lambda b, t: (0, 0))

    def per_batch(shape):
        return pl.BlockSpec((1,) + shape, lambda b, t: (b,) + (0,) * len(shape))

    names = [("ret_q", gw), ("ret_k", gw), ("ret_v", gw), ("ret_g", gw),
             ("ml_q", gw), ("ml_k", gw), ("ml_v", gw), ("ml_o", gw),
             ("gd_q", gw), ("gd_k", gw), ("gd_v", gw), ("gd_z", gw),
             ("gl_q", kw), ("gl_k", kw), ("gl_v", gw), ("gl_z", gw), ("small", LANES)]
    in_specs = [cspec(n, w) for n, w in names] + [
        pl.BlockSpec((tb, HEAD_DIM), lambda b, t: (t, 0)), pl.BlockSpec((tb, HEAD_DIM), lambda b, t: (t, 0)),
        const2((1, LANES)), const2((1, LANES)), const2((CONV_W, 3 * gw)), per_batch((CONV_W - 1, 3 * gw)),
        const2((1, gw)), const2((1, HEAD_DIM)), const2((1, HEAD_DIM)), const2((LANES, kw)), const2((1, kw)),
        pl.BlockSpec((GLA_SUB, kw, LANES), lambda b, t: (0, 0, 0))]
    assert tb % LANES == 0
    state_shape = (HEADS, HEAD_DIM, HEAD_DIM)
    out_shapes = [(c_all.shape[0], D_MODEL), (batch,) + state_shape, (batch,) + state_shape,
                  (batch, HEADS, HEAD_DIM), (batch, SUBLANES, LANES), (batch,) + state_shape,
                  (batch, CONV_W - 1, 3 * gw), (batch, HEADS // 2, LANES, HEAD_DIM)]
    out_specs = [pl.BlockSpec((tb, D_MODEL), lambda b, t: (b * nt + t, 0))] + [
        per_batch(s[1:]) for s in out_shapes[1:]]
    outs = pl.pallas_call(
        functools.partial(_prompt_mixer_kernel, tb=tb),
        grid=(batch, nt),
        in_specs=in_specs,
        out_specs=out_specs,
        out_shape=[jax.ShapeDtypeStruct(out_shapes[0], bf16)] + [jax.ShapeDtypeStruct(s, f32) for s in out_shapes[1:]],
        scratch_shapes=[pltpu.VMEM(state_shape, f32), pltpu.VMEM(state_shape, f32),
                        pltpu.VMEM((SUBLANES, HEAD_DIM), f32), pltpu.VMEM((SUBLANES, LANES), f32),
                        pltpu.VMEM(state_shape, f32), pltpu.VMEM((3, tb + SUBLANES, gw), f32),
                        pltpu.VMEM((HEADS, tb, HEAD_DIM), bf16), pltpu.VMEM((HEADS // 2, LANES, HEAD_DIM), f32)],
        compiler_params=_cparams(("parallel", "arbitrary")),
        name="prompt_mixers",
    )(*([c_all] * len(names)), lp["cos_p"], lp["sin_p"], lp["gate_bias"], lp["alog_row"], lp["gd_conv"],
      conv_zero, lp["ml_norm"], lp["gd_norm"], lp["gl_norm"], lp["gl_w2p"], lp["gl_b"], _gla_segment_table())
    y, s_ret, s_c, s_n, s_m, s_gdn, s_conv, s_gla = outs
    states = {"ret": s_ret, "ml_C": s_c, "ml_n": s_n, "ml_m": s_m[:, :HEADS, 0],
              "gdn": s_gdn, "gdn_conv": s_conv,
              "gla": s_gla.reshape(batch, HEADS, GLA_KEY_DIM, HEAD_DIM)}
    return y, states


def _rows_pad(x):
    return jnp.concatenate([x, jnp.zeros((LANES - x.shape[0], LANES), f32)], axis=0)


def _cols(x):
    return _rows_pad(x).T


def _row(x, j):
    r = lax.broadcasted_iota(jnp.int32, (x.shape[0], 1), 0)
    return jnp.sum(jnp.where(r == j, x, 0.0), axis=0, keepdims=True)


def _outer(x_cols, j, y_rows):
    lane = lax.broadcasted_iota(jnp.int32, (1, LANES), 1)
    return jnp.dot(jnp.where(lane == j, x_cols, 0.0).astype(bf16), y_rows, preferred_element_type=f32)


def _sample_mixer_kernel(c_ref, cos_ref, sin_ref, gb_ref, al_ref, wc_ref, mlnw_ref, gdnw_ref, glnw_ref,
                         w2_ref, glb_ref,
                         ret_ref, mlc_ref, mln_ref, mlm_ref, gdn_ref, conv_ref, gla_ref, *rest):
    y_ref, ret_o, mlc_o, mln_o, mlm_o, gdn_o, conv_o, gla_o = rest[-8:]
    G = SAMPLE_G
    gw = GROUP_WIDTH
    dk = GLA_KEY_DIM

    def cblk(name, width):
        return c_ref[:, C_OFF[name]:C_OFF[name] + width]

    small = cblk("small", LANES)
    pre = small + gb_ref[...]
    cosf = cos_ref[...]
    sinf = sin_ref[...]

    heads = []
    for h in range(HEADS):
        hs = slice(h * HEAD_DIM, (h + 1) * HEAD_DIM)
        q = _rope(cblk("ret_q", gw)[:, hs], cosf, sinf)
        k = _rope(cblk("ret_k", gw)[:, hs], cosf, sinf) * HEAD_DIM ** -0.5
        heads.append(dict(kt=_cols(k), vpad=_rows_pad(cblk("ret_v", gw)[:, hs]).astype(bf16), qb=q.astype(bf16),
                          gamma=1.0 - 2.0 ** (-5.0 - h), rows=[]))
    for j in range(G):
        for h, hd in enumerate(heads):
            s_new = hd["gamma"] * ret_ref[j, h] + _outer(hd["kt"], j, hd["vpad"])
            ret_o[j, h] = s_new
            hd["rows"].append(_row(jnp.dot(hd["qb"], s_new.astype(bf16), preferred_element_type=f32), j))
    for h, hd in enumerate(heads):
        hs = slice(h * HEAD_DIM, (h + 1) * HEAD_DIM)
        o = jnp.concatenate(hd["rows"], axis=0)
        y = _rms_rows(o) * _silu(cblk("ret_g", gw)[:, hs])
        y_ref[:, 0 * gw + h * HEAD_DIM:0 * gw + (h + 1) * HEAD_DIM] = y.astype(y_ref.dtype)

    m_old = mlm_ref[...]
    f_al = pltpu.roll(jax.nn.log_sigmoid(pre), LANES - (L_MLF - L_MLI), 1)
    m_int = m_old + f_al
    m_t = jnp.maximum(m_int, pre)
    w_in = jnp.exp(pre - m_t)
    g_in = jnp.exp(m_int - m_t)
    e_neg = jnp.exp(-m_t)
    mlm_o[...] = m_t
    heads = []
    for h in range(HEADS):
        hs = slice(h * HEAD_DIM, (h + 1) * HEAD_DIM)
        q = cblk("ml_q", gw)[:, hs]
        kw = cblk("ml_k", gw)[:, hs] * HEAD_DIM ** -0.5 * w_in[:, h:h + 1]
        heads.append(dict(q=q, kw=kw, kt=_cols(kw), vpad=_rows_pad(cblk("ml_v", gw)[:, hs]).astype(bf16),
                          qb=q.astype(bf16), rows=[]))
    for j in range(G):
        for h, hd in enumerate(heads):
            gj = g_in[j:j + 1, h:h + 1]
            c_new = gj * mlc_ref[j, h] + _outer(hd["kt"], j, hd["vpad"])
            mlc_o[j, h] = c_new
            n_new = gj * mln_ref[j, h:h + 1, :] + hd["kw"][j:j + 1, :]
            mln_o[j, h:h + 1, :] = n_new
            num = _row(jnp.dot(hd["qb"], c_new.astype(bf16), preferred_element_type=f32), j)
            den = jnp.sum(hd["q"][j:j + 1, :] * n_new, axis=-1, keepdims=True)
            hd["rows"].append(num / jnp.maximum(jnp.abs(den), e_neg[j:j + 1, h:h + 1]))
    for h, hd in enumerate(heads):
        hs = slice(h * HEAD_DIM, (h + 1) * HEAD_DIM)
        hh = jnp.concatenate(hd["rows"], axis=0)
        y = _rms_rows(hh) * mlnw_ref[:, hs] * jax.nn.sigmoid(cblk("ml_o", gw)[:, hs])
        y_ref[:, 1 * gw + h * HEAD_DIM:1 * gw + (h + 1) * HEAD_DIM] = y.astype(y_ref.dtype)

    beta_all = jax.nn.sigmoid(pre)
    eg_all = jnp.exp(-jnp.exp(al_ref[...]) * jax.nn.softplus(pre))
    conv = []
    for p, name in enumerate(("gd_q", "gd_k", "gd_v")):
        ps = slice(p * gw, (p + 1) * gw)
        x = cblk(name, gw)
        acc = x * wc_ref[CONV_W - 1:CONV_W, ps]
        for j in range(CONV_W - 1):
            acc = acc + conv_ref[j, :, ps] * wc_ref[j:j + 1, ps]
        conv.append(_silu(acc))
        for j in range(CONV_W - 2):
            conv_o[j, :, ps] = conv_ref[j + 1, :, ps]
        conv_o[CONV_W - 2, :, ps] = x
    heads = []
    for h in range(HEADS):
        hs = slice(h * HEAD_DIM, (h + 1) * HEAD_DIM)
        q = _l2n_rows(conv[0][:, hs]) * HEAD_DIM ** -0.5
        k = _l2n_rows(conv[1][:, hs])
        heads.append(dict(kt=_cols(k), kb16=k.astype(bf16), qb=q.astype(bf16), v=conv[2][:, hs],
                          bcol=beta_all[:, L_GDB + h:L_GDB + h + 1], ecol=eg_all[:, L_GDA + h:L_GDA + h + 1],
                          ks=[], rows=[]))
    for j in range(G):
        for h, hd in enumerate(heads):
            hd["ks"].append(_row(jnp.dot(hd["kb16"], gdn_ref[j, h].astype(bf16), preferred_element_type=f32), j))
    for hd in heads:
        ks = jnp.concatenate(hd["ks"], axis=0)
        hd["vpad"] = _rows_pad(hd["bcol"] * (hd["v"] - hd["ecol"] * ks)).astype(bf16)
    for j in range(G):
        for h, hd in enumerate(heads):
            s_new = hd["ecol"][j:j + 1, :] * gdn_ref[j, h] + _outer(hd["kt"], j, hd["vpad"])
            gdn_o[j, h] = s_new
            hd["rows"].append(_row(jnp.dot(hd["qb"], s_new.astype(bf16), preferred_element_type=f32), j))
    for h, hd in enumerate(heads):
        hs = slice(h * HEAD_DIM, (h + 1) * HEAD_DIM)
        o = jnp.concatenate(hd["rows"], axis=0)
        y = _rms_rows(o) * gdnw_ref[...] * _silu(cblk("gd_z", gw)[:, hs])
        y_ref[:, 2 * gw + h * HEAD_DIM:2 * gw + (h + 1) * HEAD_DIM] = y.astype(y_ref.dtype)

    gk = jax.nn.log_sigmoid(_dot_exact(small, w2_ref[...]) + glb_ref[...]) * (1.0 / GLA_GATE_DENOM)
    eg = jnp.exp(gk)
    q_all = cblk("gl_q", HEADS * dk) * dk ** -0.5
    k_all = cblk("gl_k", HEADS * dk)
    upper = lax.broadcasted_iota(jnp.int32, (LANES, 1), 0) < dk
    lane = lax.broadcasted_iota(jnp.int32, (1, LANES), 1)
    pairs = []
    for p in range(HEADS // 2):
        ps = slice(p * LANES, (p + 1) * LANES)
        kp, qp = k_all[:, ps], q_all[:, ps]
        v0 = cblk("gl_v", gw)[:, (2 * p) * HEAD_DIM:(2 * p + 1) * HEAD_DIM]
        v1 = cblk("gl_v", gw)[:, (2 * p + 1) * HEAD_DIM:(2 * p + 2) * HEAD_DIM]
        pairs.append(dict(
            kt2=_cols(jnp.concatenate([kp, kp], axis=0)),
            et=_cols(eg[:, ps]),
            vpad=_rows_pad(jnp.concatenate([v0, v1], axis=0)).astype(bf16),
            q2=jnp.concatenate([jnp.where(lane < dk, qp, 0.0), jnp.where(lane < dk, 0.0, qp)],
                               axis=0).astype(bf16),
            rows=([], [])))
    for j in range(G):
        keep = lane == jnp.where(upper, j, G + j)
        for p, pd in enumerate(pairs):
            u = jnp.dot(jnp.where(keep, pd["kt2"], 0.0).astype(bf16), pd["vpad"], preferred_element_type=f32)
            s_new = pd["et"][:, j:j + 1] * gla_ref[j, p] + u
            gla_o[j, p] = s_new
            res = jnp.dot(pd["q2"], s_new.astype(bf16), preferred_element_type=f32)
            pd["rows"][0].append(_row(res, j))
            pd["rows"][1].append(_row(res, G + j))
    for p, pd in enumerate(pairs):
        for hh in range(2):
            h = 2 * p + hh
            hs = slice(h * HEAD_DIM, (h + 1) * HEAD_DIM)
            o = jnp.concatenate(pd["rows"][hh], axis=0)
            y = _rms_rows(o) * glnw_ref[...] * _silu(cblk("gl_z", gw)[:, hs])
            y_ref[:, 3 * gw + h * HEAD_DIM:3 * gw + (h + 1) * HEAD_DIM] = y.astype(y_ref.dtype)


def _sample_mixers(c_all, lp, st, y_prev, st_prev, *, layer, depth, row0, nb):
    G = SAMPLE_G
    gw = GROUP_WIDTH
    kw = HEADS * GLA_KEY_DIM
    assert row0 % G == 0 and nb % G == 0
    blk0 = row0 // G

    def const2(shape):
        return pl.BlockSpec(shape, lambda i: (0, 0))

    def slab(shape, batch_axis=0):
        def index(i):
            idx = [0] * len(shape)
            idx[batch_axis] = i
            return (layer,) + tuple(idx)
        return pl.BlockSpec((None,) + shape, index)

    state_specs = [slab((G, HEADS, HEAD_DIM, HEAD_DIM)), slab((G, HEADS, HEAD_DIM, HEAD_DIM)),
                   slab((G, HEADS, HEAD_DIM)), slab((G, LANES)), slab((G, HEADS, HEAD_DIM, HEAD_DIM)),
                   slab((CONV_W - 1, G, 3 * gw), batch_axis=1), slab((G, HEADS // 2, LANES, HEAD_DIM))]
    order = ("ret", "ml_C", "ml_n", "ml_m", "gdn", "gdn_conv", "gla")
    states = [st[n] for n in order]
    y_spec = pl.BlockSpec((G, D_MODEL), lambda i: (blk0 + i, 0))

    args = [c_all, lp["cos_s"], lp["sin_s"], lp["gate_bias"], lp["alog_row"], lp["gd_conv"], lp["ml_norm"],
            lp["gd_norm"], lp["gl_norm"], lp["gl_w2p"], lp["gl_b"]] + states
    in_specs = [pl.BlockSpec((G, C_WIDTH), lambda i: (blk0 + i, 0)),
                const2((1, HEAD_DIM)), const2((1, HEAD_DIM)), const2((1, LANES)), const2((1, LANES)),
                const2((CONV_W, 3 * gw)), const2((1, gw)), const2((1, HEAD_DIM)), const2((1, HEAD_DIM)),
                const2((LANES, kw)), const2((1, kw))] + state_specs
    inplace = [y_prev] + ([st_prev[n] for n in order] if st_prev is not None else [])
    aliases = {len(args) + k: k for k in range(len(inplace))}
    args += inplace
    in_specs += [pl.BlockSpec(memory_space=pl.ANY)] * len(inplace)

    outs = pl.pallas_call(
        _sample_mixer_kernel,
        grid=(nb // G,),
        in_specs=in_specs,
        out_specs=[y_spec] + state_specs,
        out_shape=[jax.ShapeDtypeStruct(y_prev.shape, y_prev.dtype)] + [
            jax.ShapeDtypeStruct(s.shape, s.dtype) for s in states],
        input_output_aliases=aliases,
        compiler_params=_cparams(("parallel",)),
        name="sample_mixers",
    )(*args)
    return outs[0], dict(zip(order, outs[1:]))


def _rope_tables(pos):
    half = HEAD_DIM // 2
    inv = ROPE_BASE ** (-jnp.arange(half, dtype=f32) / half)
    ang = pos[:, None] * inv[None, :]
    cos, sin = jnp.cos(ang), jnp.sin(ang)
    return jnp.concatenate([cos, cos], axis=-1), jnp.concatenate([-sin, sin], axis=-1)


def _permute_w_in(w):
    big = jnp.concatenate([w[..., 0:4096], w[..., 4104:6152], w[..., 6160:7696]], axis=-1)
    small = jnp.concatenate([w[..., 4096:4104], w[..., 6152:6160], w[..., 7696:7712]], axis=-1)
    pad = jnp.zeros(w.shape[:-1] + (C_WIDTH - big.shape[-1] - small.shape[-1],), w.dtype)
    return jnp.concatenate([big, small, pad], axis=-1)


W_IN_RAW = 7712
W_IN_TILE = 512
W_IN_SHIFTS = ((8, 8, 12), (16, 12, 15))
W_IN_EDGE = 16
W_IN_GATE_ROWS = (4096, 6144, 7696)


def _w_in_prep_kernel(a_ref, b_ref, c0_ref, c1_ref, c2_ref, o_ref):
    j = pl.program_id(1)
    k = a_ref.shape[1]

    def emit(x):
        o_ref[...] = x.T.astype(bf16)

    @pl.when(j < W_IN_SHIFTS[0][1])
    def _():
        emit(a_ref[...])

    for shift, lo, hi in W_IN_SHIFTS:
        @pl.when(jnp.logical_and(j >= lo, j < hi))
        def _():
            emit(jnp.concatenate([a_ref[shift:, :], b_ref[:shift, :]], axis=0))

    @pl.when(j >= W_IN_SHIFTS[-1][2])
    def _():
        gates = jnp.concatenate([c0_ref[:L_GDB, :], c1_ref[L_GDB:L_GLR, :], c2_ref[...]], axis=0)
        emit(jnp.concatenate([gates, jnp.zeros((W_IN_TILE - gates.shape[0], k), f32)], axis=0))


def _w_in_prep(w_in):
    depth, d, n = w_in.shape
    assert n == W_IN_RAW and C_OFF["small"] == W_IN_SHIFTS[-1][2] * W_IN_TILE
    assert L_GLR + GLA_RANK == 2 * W_IN_EDGE and n % W_IN_EDGE == 0
    w_t = jnp.swapaxes(w_in, 1, 2)
    last_wide = W_IN_SHIFTS[-1][2] - 1
    per_tile = W_IN_TILE // W_IN_EDGE
    last_edge = n // W_IN_EDGE - 1

    def edge(index):
        return pl.BlockSpec((None, W_IN_EDGE, d), index)

    g0, g1, g2 = (r // W_IN_EDGE for r in W_IN_GATE_ROWS)
    return pl.pallas_call(
        _w_in_prep_kernel,
        grid=(depth, C_WIDTH // W_IN_TILE),
        in_specs=[pl.BlockSpec((None, W_IN_TILE, d), lambda l, j: (l, jnp.minimum(j, last_wide), 0)),
                  edge(lambda l, j: (l, jnp.minimum(per_tile * (j + 1), last_edge), 0)),
                  edge(lambda l, j: (l, g0, 0)), edge(lambda l, j: (l, g1, 0)), edge(lambda l, j: (l, g2, 0))],
        out_specs=pl.BlockSpec((None, d, W_IN_TILE), lambda l, j: (l, 0, j)),
        out_shape=jax.ShapeDtypeStruct((depth, d, C_WIDTH), bf16),
        compiler_params=_cparams(("parallel", "arbitrary")),
        name="w_in_prep",
    )(w_t, w_t, w_t, w_t, w_t)


def _lane_row(pieces):
    row = jnp.zeros((1, LANES), f32)
    for off, val in pieces:
        row = lax.dynamic_update_slice(row, val.reshape(1, -1).astype(f32), (0, off))
    return row


def kernel(x_prompt, x_sample, state_ret, state_mlstm_C, state_mlstm_n, state_mlstm_m, state_gdn, state_gdn_conv, state_gla, cache_mem_k, cache_mem_v, mem_prompt, norm_mix_pre, norm_mix_post, w_in, ml_ib, ml_fb, ml_norm, gd_conv, gd_A_log, gd_dt_bias, gd_norm, gl_w2, gl_b, gl_norm, w_out, norm_x_pre, norm_x_post, norm_mem, w_xq, w_xk, w_xv, w_xo, norm_mlp_pre, norm_mlp_post, w_up, w_down):
    bp, tp, d = x_prompt.shape
    bs, ts, _ = x_sample.shape
    depth = w_in.shape[0]
    assert ts == 1 and d == D_MODEL and tp % MIX_TB == 0
    rows_p = bp * tp
    rows = rows_p + bs
    assert rows % ROW_TILE == 0 and rows % ROW_TILE_WIDE == 0 and rows_p % SAMPLE_G == 0

    xh = _embed(x_prompt.reshape(rows_p, d), norm_mix_pre[0], None, row0=0, rows=rows, tm=512)
    x, h = _embed(x_sample.reshape(bs, d), norm_mix_pre[0], xh, row0=rows_p, rows=rows, tm=bs)
    w_in_b = _w_in_prep(w_in)
    w_out_b, w_xo_b, w_down_b = w_out.astype(bf16), w_xo.astype(bf16), w_down.astype(bf16)
    cos_p, sin_p = _rope_tables(jnp.arange(tp, dtype=f32))
    cos_s, sin_s = _rope_tables(jnp.arange(ts, dtype=f32) + PAST_LEN)
    conv_zero = jnp.zeros((bp, CONV_W - 1, 3 * GROUP_WIDTH), f32)
    mem2d = mem_prompt.reshape(bp * MEM_LEN, d)
    st_in = {"ret": state_ret, "ml_C": state_mlstm_C, "ml_n": state_mlstm_n,
             "ml_m": jnp.pad(state_mlstm_m, ((0, 0), (0, 0), (0, LANES - HEADS))),
             "gdn": state_gdn, "gdn_conv": jnp.transpose(state_gdn_conv, (0, 2, 1, 3)),
             "gla": state_gla.reshape(depth, bs, HEADS // 2, LANES, HEAD_DIM)}

    new_p = {n: [] for n in ("ret", "ml_C", "ml_n", "ml_m", "gdn", "gdn_conv", "gla")}
    st_s = None
    mk_p = mv_p = None
    for l in range(depth):
        lp = {
            "cos_p": cos_p, "sin_p": sin_p, "cos_s": cos_s, "sin_s": sin_s,
            "gate_bias": _lane_row([(L_MLI, ml_ib[l]), (L_MLF, ml_fb[l]), (L_GDA, gd_dt_bias[l])]),
            "alog_row": _lane_row([(L_GDA, gd_A_log[l])]),
            "ml_norm": ml_norm[l].reshape(1, GROUP_WIDTH),
            "gd_norm": gd_norm[l].reshape(1, HEAD_DIM),
            "gl_norm": gl_norm[l].reshape(1, HEAD_DIM),
            "gd_conv": gd_conv[l],
            "gl_w2p": jnp.zeros((LANES, HEADS * GLA_KEY_DIM), f32).at[L_GLR:L_GLR + GLA_RANK].set(gl_w2[l]),
            "gl_b": gl_b[l].reshape(1, HEADS * GLA_KEY_DIM),
        }
        mk_p = _norm_matmul_slab(mem2d, norm_mem[l], w_xk, mk_p, layer=l, tm=512, tn=1024, name="mem_k")
        mv_p = _norm_matmul_slab(mem2d, norm_mem[l], w_xv, mv_p, layer=l, tm=512, tn=1024, name="mem_v")

        c_all = _matmul(h, w_in_b, layer=l, tm=ROW_TILE_WIDE, tn=2048, out_dtype=f32, name="w_in")
        y_all, st_p = _prompt_mixers(c_all, lp, batch=bp, seq=tp, conv_zero=conv_zero)
        y_all, st_s = _sample_mixers(c_all, lp, st_in, y_all, st_s, layer=l, depth=depth, row0=rows_p, nb=bs)
        x, h = _matmul_norm_res(y_all, w_out_b, norm_mix_post[l], x, norm_x_pre[l], layer=l,
                                tm=ROW_TILE, tk=D_MODEL, name="w_out")

        q_all = _matmul(h, w_xq, layer=l, tm=ROW_TILE_WIDE, tn=1024, out_dtype=bf16, name="w_xq")
        o_all = _xattn_prompt(q_all, mk_p, mv_p, layer=l, batch=bp, seq=tp, tq=512)
        o_all = _xattn_sample(q_all, cache_mem_k, cache_mem_v, o_all, layer=l, row0=rows_p, nb=bs)
        x, h = _matmul_norm_res(o_all, w_xo_b, norm_x_post[l], x, norm_mlp_pre[l], layer=l,
                                tm=ROW_TILE, tk=D_MODEL, name="w_xo")

        u = _matmul(h, w_up, layer=l, tm=ROW_TILE_WIDE, tn=1024, out_dtype=bf16, act="relu2", name="w_up")
        g_next = norm_mix_pre[l + 1] if l + 1 < depth else None
        x, h = _matmul_norm_res(u, w_down_b, norm_mlp_post[l], x, g_next, layer=l,
                                tm=ROW_TILE, tk=2048, name="w_down")

        for n in new_p:
            new_p[n].append(st_p[n])

    def stk(lst):
        return jnp.stack(lst, axis=0)

    xp = x[:rows_p].reshape(bp, tp, d)
    xs = x[rows_p:].reshape(bs, ts, d)
    return (xp, xs,
            stk(new_p["ret"]), stk(new_p["ml_C"]), stk(new_p["ml_n"]), stk(new_p["ml_m"]),
            stk(new_p["gdn"]), stk(new_p["gdn_conv"]), stk(new_p["gla"]),
            mk_p.reshape(depth, bp, MEM_LEN, d), mv_p.reshape(depth, bp, MEM_LEN, d),
            st_s["ret"], st_s["ml_C"], st_s["ml_n"], st_s["ml_m"][:, :, :HEADS],
            st_s["gdn"], jnp.transpose(st_s["gdn_conv"], (0, 2, 1, 3)),
            st_s["gla"].reshape(depth, bs, HEADS, GLA_KEY_DIM, HEAD_DIM))
```
